```python
import jax, jax.numpy as jnp
from jax import lax
import numpy as np

D_MODEL = 1024
BATCH = 8
SEQ = 4096
DEPTH = 1

N_MEM = 256
RMS_EPS = 1e-6
GN_EPS = 1e-5
ATT_HEADS = 8
ATT_HEAD_DIM = 64
ATT_WIDTH = ATT_HEADS * ATT_HEAD_DIM
ROT_DIM = ATT_HEAD_DIM // 4
ROPE_THETA = 500000.0
DILATED_PATTERNS = ((128, 1), (512, 4), (2048, 16))
RET_HEADS = 4
RET_HEAD_DIM = 128
RET_WIDTH = RET_HEADS * RET_HEAD_DIM
RET_CHUNK = 128
RET_ROPE_THETA = 10000.0
MIX_WIDTH = ATT_WIDTH + RET_WIDTH
IN_PROJ_WIDTH = 3 * ATT_WIDTH + 4 * RET_WIDTH
XATT_HEADS = 4
XATT_HEAD_DIM = D_MODEL // XATT_HEADS
FFN_HIDDEN = -(-(8 * D_MODEL) // (3 * 256)) * 256

kernel_name = "hymba_dilated_retention_hybrid_layer"


def rms_norm(x, g):
    xf = x.astype(jnp.float32)
    y = xf * lax.rsqrt(jnp.mean(xf * xf, axis=-1, keepdims=True) + RMS_EPS)
    return (y * g.astype(jnp.float32)).astype(x.dtype)


def rope(x, pos, rot_dim, theta):
    inv = theta ** (-(jnp.arange(0, rot_dim, 2, dtype=jnp.float32) / rot_dim))
    ang = pos.astype(jnp.float32)[:, None] * inv[None, :]
    cos, sin = jnp.cos(ang), jnp.sin(ang)
    xr = x[..., :rot_dim].astype(jnp.float32)
    x1, x2 = xr[..., : rot_dim // 2], xr[..., rot_dim // 2:]
    rot = jnp.concatenate([x1 * cos - x2 * sin, x2 * cos + x1 * sin], axis=-1)
    return jnp.concatenate([rot.astype(x.dtype), x[..., rot_dim:]], axis=-1)


def window_attn_by_stride(q, k, v, window, dilation):
    B, H, S, dh = q.shape
    n_keys = window // dilation
    L = S // dilation
    nb = -(-L // n_keys)
    Lp = nb * n_keys

    def to_classes(t):
        t = t.reshape(B, H, L, dilation, dh).transpose(0, 1, 3, 2, 4)
        return jnp.pad(t, [(0, 0)] * 3 + [(0, Lp - L), (0, 0)])

    def with_prev(t):
        cur = t.reshape(B, H, dilation, nb, n_keys, dh)
        prev = jnp.pad(cur, [(0, 0)] * 3 + [(1, 0), (0, 0), (0, 0)])[:, :, :, :nb]
        return jnp.concatenate([prev, cur], axis=-2)

    qb = to_classes(q).reshape(B, H, dilation, nb, n_keys, dh)
    kb = with_prev(to_classes(k))
    vb = with_prev(to_classes(v)).astype(jnp.float32)
    s = jnp.einsum('bhrnid,bhrnjd->bhrnij', qb, kb).astype(jnp.float32) * (dh ** -0.5)
    i = jnp.arange(n_keys)[:, None]
    j = jnp.arange(2 * n_keys)[None, :]
    band = (j >= i) & (j <= i + n_keys)
    not_before_start = (jnp.arange(nb)[:, None, None] > 0) | (j[None] >= n_keys)
    mask = band[None] & not_before_start
    s = jnp.where(mask, s, -jnp.inf)
    m = jnp.max(s, axis=-1)
    p = jnp.exp(s - m[..., None])
    l = jnp.sum(p, axis=-1)
    o = jnp.einsum('bhrnij,bhrnjd->bhrnid', p, vb) / l[..., None]

    def back(t, has_d):
        if has_d:
            t = t.reshape(B, H, dilation, Lp, dh)[:, :, :, :L]
            return t.transpose(0, 1, 3, 2, 4).reshape(B, H, S, dh)
        t = t.reshape(B, H, dilation, Lp)[:, :, :, :L]
        return t.transpose(0, 1, 3, 2).reshape(B, H, S)

    return back(o, True), back(m, False), back(l, False)


def dilated_attention(q, k, v):
    outs = [window_attn_by_stride(q, k, v, w, r) for (w, r) in DILATED_PATTERNS]
    m_all = jnp.max(jnp.stack([m for _, m, _ in outs], axis=0), axis=0)
    wts = [l * jnp.exp(m - m_all) for _, m, l in outs]
    num = sum(wt[..., None] * o for wt, (o, _, _) in zip(wts, outs))
    den = sum(wts)
    return num / den[..., None]


def retention(q, k, v):
    B, H, S, dk = q.shape
    dv = v.shape[-1]
    C = RET_CHUNK
    nc = S // C
    log_g = jnp.log(1.0 - 2.0 ** (-5.0 - jnp.arange(H, dtype=jnp.float32)))
    n = jnp.arange(C, dtype=jnp.float32)
    rel = n[:, None] - n[None, :]
    decay_mask = jnp.where(rel >= 0, jnp.exp(log_g[:, None, None] * jnp.maximum(rel, 0.0)), 0.0)
    xi = jnp.exp(log_g[:, None] * (n + 1.0))
    zeta = jnp.exp(log_g[:, None] * (C - 1.0 - n))
    chunk_decay = jnp.exp(log_g * C)

    def chunks(t):
        return t.reshape(B, H, nc, C, t.shape[-1]).transpose(2, 0, 1, 3, 4)

    def step(R, qkv):
        qc, kc, vc = qkv
        inner = jnp.einsum('bhnd,bhmd->bhnm', qc, kc) * decay_mask
        o = jnp.einsum('bhnm,bhme->bhne', inner, vc)
        o = o + jnp.einsum('bhnd,bhde->bhne', qc, R) * xi[:, :, None]
        R = R * chunk_decay[:, None, None] + jnp.einsum('bhmd,bhme->bhde', kc * zeta[:, :, None], vc)
        return R, o

    R0 = jnp.zeros((B, H, dk, dv), jnp.float32)
    _, o = lax.scan(step, R0, (chunks(q), chunks(k), chunks(v)))
    return o.transpose(1, 2, 0, 3, 4).reshape(B, H, S, dv)


def hybrid_mixer(h, w_in, attn_gn_g, ret_gn_g, w_out):
    B, S, _ = h.shape
    proj = h @ w_in
    splits = list(np.cumsum([ATT_WIDTH] * 3 + [RET_WIDTH] * 3))
    aq, ak, av, rq, rk, rv, rg = jnp.split(proj, splits, axis=-1)
    pos = jnp.arange(S)

    def heads(t, nh, d):
        return t.reshape(B, S, nh, d).transpose(0, 2, 1, 3)

    aq = rope(heads(aq, ATT_HEADS, ATT_HEAD_DIM), pos, ROT_DIM, ROPE_THETA)
    ak = rope(heads(ak, ATT_HEADS, ATT_HEAD_DIM), pos, ROT_DIM, ROPE_THETA)
    a = dilated_attention(aq, ak, heads(av, ATT_HEADS, ATT_HEAD_DIM))
    a = a * lax.rsqrt(jnp.mean(a * a, axis=-1, keepdims=True) + RMS_EPS)
    a = a.transpose(0, 2, 1, 3).reshape(B, S, ATT_WIDTH) * attn_gn_g.astype(jnp.float32)

    rq = rope(heads(rq, RET_HEADS, RET_HEAD_DIM).astype(jnp.float32), pos, RET_HEAD_DIM, RET_ROPE_THETA)
    rk = rope(heads(rk, RET_HEADS, RET_HEAD_DIM).astype(jnp.float32), pos, RET_HEAD_DIM, RET_ROPE_THETA)
    rk = rk * (RET_HEAD_DIM ** -0.5)
    r = retention(rq, rk, heads(rv, RET_HEADS, RET_HEAD_DIM).astype(jnp.float32))
    mu = jnp.mean(r, axis=-1, keepdims=True)
    var = jnp.mean(jnp.square(r - mu), axis=-1, keepdims=True)
    r = (r - mu) * lax.rsqrt(var + GN_EPS)
    r = r.transpose(0, 2, 1, 3).reshape(B, S, RET_WIDTH) * ret_gn_g.astype(jnp.float32)
    r = jax.nn.silu(rg.astype(jnp.float32)) * r

    y = jnp.concatenate([a, r], axis=-1).astype(h.dtype)
    return y @ w_out


def memory_cross_attn(h, mem, mem_norm_g, w_q, w_kv, w_o):
    B, S, _ = h.shape
    mh = rms_norm(mem, mem_norm_g)
    q = (h @ w_q).reshape(B, S, XATT_HEADS, XATT_HEAD_DIM)
    k, v = jnp.split(mh @ w_kv, 2, axis=-1)
    k = k.reshape(B, N_MEM, XATT_HEADS, XATT_HEAD_DIM)
    v = v.reshape(B, N_MEM, XATT_HEADS, XATT_HEAD_DIM)
    s = jnp.einsum('bshd,bmhd->bhsm', q, k).astype(jnp.float32) * (XATT_HEAD_DIM ** -0.5)
    p = jax.nn.softmax(s, axis=-1)
    o = jnp.einsum('bhsm,bmhd->bshd', p.astype(v.dtype), v).reshape(B, S, D_MODEL)
    return o @ w_o


def swiglu(h, w_gate_up, w_down):
    g, u = jnp.split(h @ w_gate_up, 2, axis=-1)
    return (jax.nn.silu(g) * u) @ w_down


def setup_inputs(seed: int = 0) -> dict:
    key = jax.random.key(seed)
    ks = jax.random.split(key, 20)

    def w(k, shape, fan_in):
        return jax.random.normal(k, shape, jnp.float32) * (fan_in ** -0.5)

    def gain(k, n):
        return 1.0 + 0.05 * jax.random.normal(k, (DEPTH, n), jnp.float32)

    return {
        "x": jax.random.normal(ks[0], (BATCH, SEQ, D_MODEL), jnp.float32),
        "mem": jax.random.normal(ks[1], (BATCH, N_MEM, D_MODEL), jnp.float32),
        "pre_mix_g": gain(ks[2], D_MODEL),
        "post_mix_g": gain(ks[3], D_MODEL),
        "w_in": w(ks[4], (DEPTH, D_MODEL, IN_PROJ_WIDTH), D_MODEL),
        "attn_gn_g": gain(ks[5], ATT_WIDTH),
        "ret_gn_g": gain(ks[6], RET_WIDTH),
        "w_out": w(ks[7], (DEPTH, MIX_WIDTH, D_MODEL), MIX_WIDTH),
        "pre_mem_g": gain(ks[8], D_MODEL),
        "post_mem_g": gain(ks[9], D_MODEL),
        "mem_norm_g": gain(ks[10], D_MODEL),
        "w_q_mem": w(ks[11], (DEPTH, D_MODEL, D_MODEL), D_MODEL),
        "w_kv_mem": w(ks[12], (DEPTH, D_MODEL, 2 * D_MODEL), D_MODEL),
        "w_o_mem": w(ks[13], (DEPTH, D_MODEL, D_MODEL), D_MODEL),
        "pre_ffn_g": gain(ks[14], D_MODEL),
        "post_ffn_g": gain(ks[15], D_MODEL),
        "w_gate_up": w(ks[16], (DEPTH, D_MODEL, 2 * FFN_HIDDEN), D_MODEL),
        "w_down": w(ks[17], (DEPTH, FFN_HIDDEN, D_MODEL), FFN_HIDDEN),
    }


def reference(x, mem, pre_mix_g, post_mix_g, w_in, attn_gn_g, ret_gn_g, w_out,
              pre_mem_g, post_mem_g, mem_norm_g, w_q_mem, w_kv_mem, w_o_mem,
              pre_ffn_g, post_ffn_g, w_gate_up, w_down):
    for l in range(DEPTH):
        y = hybrid_mixer(rms_norm(x, pre_mix_g[l]), w_in[l], attn_gn_g[l], ret_gn_g[l], w_out[l])
        x = x + rms_norm(y, post_mix_g[l])
        y = memory_cross_attn(rms_norm(x, pre_mem_g[l]), mem, mem_norm_g[l],
                              w_q_mem[l], w_kv_mem[l], w_o_mem[l])
        x = x + rms_norm(y, post_mem_g[l])
        y = swiglu(rms_norm(x, pre_ffn_g[l]), w_gate_up[l], w_down[l])
        x = x + rms_norm(y, post_ffn_g[l])
    return x
```

```python
import functools

import numpy as np
import jax
import jax.numpy as jnp
from jax import lax
from jax.experimental import pallas as pl
from jax.experimental.pallas import tpu as pltpu

F32 = jnp.float32
BF16 = jnp.bfloat16

RMS_EPS = 1e-6
GN_EPS = 1e-5
ATT_HEADS = 8
ATT_HEAD_DIM = 64
ATT_WIDTH = ATT_HEADS * ATT_HEAD_DIM
ROT_DIM = ATT_HEAD_DIM // 4
ROPE_THETA = 500000.0
DILATED_PATTERNS = ((128, 1), (512, 4), (2048, 16))
RET_HEADS = 4
RET_HEAD_DIM = 128
RET_WIDTH = RET_HEADS * RET_HEAD_DIM
RET_CHUNK = 128
RET_ROPE_THETA = 10000.0
XATT_HEADS = 4

LANES = 128
BAND = 128
MASKED = -1e30
VMEM_LIMIT = 56 * 1024 * 1024


def _resident(shape):
    zeros = (0,) * len(shape)
    return pl.BlockSpec(shape, lambda *_: zeros, pipeline_mode=pl.Buffered(1))


def _params(*sem):
    return pltpu.CompilerParams(dimension_semantics=sem, vmem_limit_bytes=VMEM_LIMIT)


def _rms(x, g):
    return x * lax.rsqrt(jnp.mean(x * x, axis=-1, keepdims=True) + RMS_EPS) * g


def _dot(a, b):
    return jnp.dot(a, b, preferred_element_type=F32)


def _dot_nt(a, b):
    return lax.dot_general(a, b, (((1,), (1,)), ((), ())), preferred_element_type=F32)


def _dot_tn(a, b):
    return lax.dot_general(a, b, (((0,), (0,)), ((), ())), preferred_element_type=F32)


def _in_proj_kernel(x_ref, g_ref, w_ref, ca_ref, sa1_ref, sa2_ref, cr_ref, sr_ref,
                    aq_ref, ak_ref, av_ref, rq_ref, rk_ref, rv_ref, rg_ref):
    h = _rms(x_ref[...], g_ref[...]).astype(BF16)

    def proj(j, width):
        return _dot(h, w_ref[:, j:j + width])

    def rope_attn(t):
        outs = []
        for c in range(0, ATT_WIDTH, LANES):
            tg = t[:, c:c + LANES]
            outs.append(tg * ca_ref[...]
                        + pltpu.roll(tg, LANES - ROT_DIM // 2, 1) * sa1_ref[...]
                        + pltpu.roll(tg, ROT_DIM // 2, 1) * sa2_ref[...])
        return jnp.concatenate(outs, axis=1)

    def rope_ret(t):
        outs = []
        for c in range(0, RET_WIDTH, LANES):
            tg = t[:, c:c + LANES]
            outs.append(tg * cr_ref[...] + pltpu.roll(tg, RET_HEAD_DIM // 2, 1) * sr_ref[...])
        return jnp.concatenate(outs, axis=1)

    o = 0
    aq_ref[...] = (rope_attn(proj(o, ATT_WIDTH)) * (ATT_HEAD_DIM ** -0.5)).astype(BF16)
    o += ATT_WIDTH
    ak_ref[...] = rope_attn(proj(o, ATT_WIDTH)).astype(BF16)
    o += ATT_WIDTH
    av_ref[...] = proj(o, ATT_WIDTH).astype(BF16)
    o += ATT_WIDTH
    rq_ref[...] = rope_ret(proj(o, RET_WIDTH)).astype(BF16)
    o += RET_WIDTH
    rk_ref[...] = (rope_ret(proj(o, RET_WIDTH)) * (RET_HEAD_DIM ** -0.5)).astype(BF16)
    o += RET_WIDTH
    rv_ref[...] = proj(o, RET_WIDTH).astype(BF16)
    o += RET_WIDTH
    rg_ref[...] = proj(o, RET_WIDTH).astype(BF16)


def _rope_tables(seq):
    pos = jnp.arange(seq, dtype=F32)[:, None]
    half = ROT_DIM // 2
    inv = ROPE_THETA ** (-(jnp.arange(0, ROT_DIM, 2, dtype=F32) / ROT_DIM))
    ang = pos * inv[None, :]
    cos, sin = jnp.cos(ang), jnp.sin(ang)
    one = jnp.ones((seq, ATT_HEAD_DIM - ROT_DIM), F32)
    zero = jnp.zeros((seq, ATT_HEAD_DIM - ROT_DIM), F32)
    zh = jnp.zeros((seq, half), F32)
    ca = jnp.concatenate([cos, cos, one], axis=1)
    sa1 = jnp.concatenate([-sin, zh, zero], axis=1)
    sa2 = jnp.concatenate([zh, sin, zero], axis=1)
    reps = LANES // ATT_HEAD_DIM
    ca, sa1, sa2 = (jnp.tile(t, (1, reps)) for t in (ca, sa1, sa2))
    inv_r = RET_ROPE_THETA ** (-(jnp.arange(0, RET_HEAD_DIM, 2, dtype=F32) / RET_HEAD_DIM))
    ang_r = pos * inv_r[None, :]
    cr = jnp.concatenate([jnp.cos(ang_r), jnp.cos(ang_r)], axis=1)
    sr = jnp.concatenate([-jnp.sin(ang_r), jnp.sin(ang_r)], axis=1)
    return ca, sa1, sa2, cr, sr


def _in_proj(x2d, g, w_in, seq, tm=512):
    tokens, d = x2d.shape
    width = w_in.shape[1]
    tiles_per_seq = seq // tm
    tables = _rope_tables(seq)
    row = lambda t: (t, 0)
    tab = pl.BlockSpec((tm, LANES), lambda t: (t % tiles_per_seq, 0))
    out = jax.ShapeDtypeStruct((tokens, ATT_WIDTH), BF16)
    return pl.pallas_call(
        _in_proj_kernel,
        out_shape=(out,) * 7,
        grid=(tokens // tm,),
        in_specs=[pl.BlockSpec((tm, d), row), _resident((1, d)), _resident((d, width))] + [tab] * 5,
        out_specs=(pl.BlockSpec((tm, ATT_WIDTH), row),) * 7,
        compiler_params=_params("parallel"),
        name="in_proj",
    )(x2d, g, w_in, *tables)


def _attn_kernel(*refs, length, has_prev, is_last):
    refs = list(refs)
    q_ref, k_ref, v_ref = refs[:3]
    del refs[:3]
    if has_prev:
        po_ref, pl_ref = refs[:2]
        del refs[:2]
    if is_last:
        g_ref = refs.pop(0)
    o_ref = refs.pop(0)
    if not is_last:
        l_ref = refs.pop(0)

    row = lax.broadcasted_iota(jnp.int32, (BAND, 2 * BAND), 0)
    col = lax.broadcasted_iota(jnp.int32, (BAND, 2 * BAND), 1)
    mask_first = col <= row
    mask_band = (col >= row) & (col <= row + BAND)
    lane = lax.broadcasted_iota(jnp.int32, (BAND, LANES), 1)
    lo = lane < ATT_HEAD_DIM
    pair = LANES // ATT_HEAD_DIM

    def block(qstart, kstart, mask):
        qs = pl.ds(qstart, BAND)
        ks = pl.ds(kstart, 2 * BAND)
        stats = jnp.zeros((BAND, LANES), F32)
        if has_prev:
            pstats = pl_ref[0, qs, :]
        for c in range(0, ATT_WIDTH, LANES):
            qp = q_ref[0, qs, c:c + LANES]
            kp = k_ref[0, ks, c:c + LANES]
            vp = v_ref[0, ks, c:c + LANES]
            if has_prev:
                prev_o = po_ref[0, qs, c:c + LANES].astype(F32)
            halves = []
            for hh in range(pair):
                head = c // ATT_HEAD_DIM + hh
                sel = lo if hh == 0 else jnp.logical_not(lo)
                s = _dot_nt(jnp.where(sel, qp, jnp.zeros_like(qp)), kp)
                s = jnp.where(mask, s, MASKED)
                m = jnp.max(s, axis=-1, keepdims=True)
                p = jnp.exp(s - m)
                l = jnp.sum(p, axis=-1, keepdims=True)
                o = _dot(p.astype(BF16), vp) / l
                lse = m + jnp.log(l)
                if has_prev:
                    lp = pstats[:, head:head + 1]
                    top = jnp.maximum(lp, lse)
                    wp = jnp.exp(lp - top)
                    wn = jnp.exp(lse - top)
                    o = (wp * prev_o + wn * o) / (wp + wn)
                    lse = top + jnp.log(wp + wn)
                halves.append(o)
                stats = jnp.where(lane == head, lse, stats)
            o = jnp.where(lo, halves[0], halves[1])
            if is_last:
                sq = o * o
                ms_lo = jnp.sum(jnp.where(lo, sq, 0.0), axis=-1, keepdims=True)
                ms_hi = jnp.sum(jnp.where(lo, 0.0, sq), axis=-1, keepdims=True)
                ms = jnp.where(lo, ms_lo, ms_hi) * (1.0 / ATT_HEAD_DIM)
                o = o * lax.rsqrt(ms + RMS_EPS) * g_ref[:, c:c + LANES]
            o_ref[0, qs, c:c + LANES] = o.astype(o_ref.dtype)
        if not is_last:
            l_ref[0, qs, :] = stats

    block(0, 0, mask_first)

    def body(n, carry):
        block(pl.multiple_of(n * BAND, BAND), pl.multiple_of((n - 1) * BAND, BAND), mask_band)
        return carry

    lax.fori_loop(1, length // BAND, body, 0)


def _attention(aq, ak, av, gn_g):
    batch, seq, width = aq.shape
    prev = None
    for idx, (window, r) in enumerate(DILATED_PATTERNS):
        assert window // r == BAND and seq % (r * 2 * BAND) == 0
        length = seq // r
        is_last = idx == len(DILATED_PATTERNS) - 1
        view = lambda t, w: t.reshape(batch, length, r * w)
        blk = pl.BlockSpec((1, length, width), lambda b, c: (b, 0, c))
        sblk = pl.BlockSpec((1, length, LANES), lambda b, c: (b, 0, c))
        args = [view(aq, width), view(ak, width), view(av, width)]
        in_specs = [blk, blk, blk]
        if prev is not None:
            args += [view(prev[0], width), view(prev[1], LANES)]
            in_specs += [blk, sblk]
        if is_last:
            args.append(gn_g)
            in_specs.append(_resident((1, width)))
            out_shape = jax.ShapeDtypeStruct((batch, length, r * width), BF16)
            out_specs = blk
        else:
            out_shape = (jax.ShapeDtypeStruct((batch, length, r * width), BF16),
                         jax.ShapeDtypeStruct((batch, length, r * LANES), F32))
            out_specs = (blk, sblk)
        res = pl.pallas_call(
            functools.partial(_attn_kernel, length=length, has_prev=prev is not None, is_last=is_last),
            out_shape=out_shape,
            grid=(batch, r),
            in_specs=in_specs,
            out_specs=out_specs,
            compiler_params=_params("parallel", "parallel"),
            name=f"attn_r{r}",
        )(*args)
        if is_last:
            return res.reshape(batch, seq, width)
        prev = (res[0].reshape(batch, seq, width), res[1].reshape(batch, seq, LANES))


def _retention_kernel(q_ref, k_ref, v_ref, gate_ref, dmask_ref, xi_ref, zeta_ref, cdec_ref, g_ref,
                      o_ref, state_ref):
    @pl.when(pl.program_id(1) == 0)
    def _():
        state_ref[...] = jnp.zeros_like(state_ref)

    for h in range(RET_HEADS):
        c = slice(h * RET_HEAD_DIM, (h + 1) * RET_HEAD_DIM)
        q = q_ref[0, :, c]
        k = k_ref[0, :, c]
        v = v_ref[0, :, c]
        state = state_ref[h]
        inner = _dot_nt(q, k) * dmask_ref[h]
        o = _dot(inner.astype(BF16), v) + _dot(q, state.astype(BF16)) * xi_ref[h]
        kz = (k.astype(F32) * zeta_ref[h]).astype(BF16)
        state_ref[h] = state * cdec_ref[h] + _dot_tn(kz, v)
        mu = jnp.mean(o, axis=-1, keepdims=True)
        var = jnp.mean(jnp.square(o - mu), axis=-1, keepdims=True)
        y = (o - mu) * lax.rsqrt(var + GN_EPS) * g_ref[:, c]
        gate = gate_ref[0, :, c].astype(F32)
        o_ref[0, :, c] = (gate * (1.0 / (1.0 + jnp.exp(-gate))) * y).astype(o_ref.dtype)


def _retention_tables():
    c = RET_CHUNK
    log_g = jnp.log(1.0 - 2.0 ** (-5.0 - jnp.arange(RET_HEADS, dtype=F32)))
    n = jnp.arange(c, dtype=F32)
    rel = n[:, None] - n[None, :]
    dmask = jnp.where(rel >= 0, jnp.exp(log_g[:, None, None] * jnp.maximum(rel, 0.0)), 0.0)
    xi = jnp.exp(log_g[:, None] * (n + 1.0))
    zeta = jnp.exp(log_g[:, None] * (c - 1.0 - n))
    cdec = jnp.exp(log_g * c)
    wide = lambda t: jnp.broadcast_to(t[:, :, None], (RET_HEADS, c, RET_HEAD_DIM))
    cdec = jnp.broadcast_to(cdec[:, None, None], (RET_HEADS, RET_HEAD_DIM, RET_HEAD_DIM))
    return dmask, wide(xi), wide(zeta), cdec


def _retention(rq, rk, rv, rg, gn_g):
    batch, seq, width = rq.shape
    c = RET_CHUNK
    blk = pl.BlockSpec((1, c, width), lambda b, n: (b, n, 0))
    tab = _resident((RET_HEADS, c, RET_HEAD_DIM))
    return pl.pallas_call(
        _retention_kernel,
        out_shape=jax.ShapeDtypeStruct((batch, seq, width), BF16),
        grid=(batch, seq // c),
        in_specs=[blk] * 4 + [tab] * 4 + [_resident((1, width))],
        out_specs=blk,
        scratch_shapes=[pltpu.VMEM((RET_HEADS, RET_HEAD_DIM, RET_HEAD_DIM), F32)],
        compiler_params=_params("parallel", "arbitrary"),
        name="retention",
    )(rq, rk, rv, rg, *_retention_tables(), gn_g)


def _mem_kv_kernel(mem_ref, g_ref, w_ref, k_ref, v_ref):
    d = mem_ref.shape[-1]
    kv = _dot(_rms(mem_ref[0], g_ref[...]).astype(BF16), w_ref[...])
    k_ref[0] = kv[:, :d].astype(BF16)
    v_ref[0] = kv[:, d:].astype(BF16)


def _mem_kv(mem, g, w_kv):
    batch, n_mem, d = mem.shape
    blk = pl.BlockSpec((1, n_mem, d), lambda b: (b, 0, 0))
    out = jax.ShapeDtypeStruct((batch, n_mem, d), BF16)
    return pl.pallas_call(
        _mem_kv_kernel,
        out_shape=(out, out),
        grid=(batch,),
        in_specs=[blk, _resident((1, d)), _resident((d, 2 * d))],
        out_specs=(blk, blk),
        compiler_params=_params("parallel"),
        name="mem_kv",
    )(mem, g, w_kv)


def _mix_out_kernel(a_ref, r_ref, x_ref, wout_ref, gpost_ref, gpre_ref, wq_ref, mk_ref, mv_ref,
                    wo_ref, gpost2_ref, o_ref):
    half = a_ref.shape[-1]
    y = _dot(a_ref[0], wout_ref[:half, :]) + _dot(r_ref[0], wout_ref[half:, :])
    x1 = x_ref[0] + _rms(y, gpost_ref[...])
    h = _rms(x1, gpre_ref[...]).astype(BF16)
    d = x1.shape[-1]
    dh = d // XATT_HEADS
    q = (_dot(h, wq_ref[...]) * (dh ** -0.5)).astype(BF16)
    heads = []
    for hd in range(XATT_HEADS):
        c = slice(hd * dh, (hd + 1) * dh)
        s = _dot_nt(q[:, c], mk_ref[0, :, c])
        p = jnp.exp(s - jnp.max(s, axis=-1, keepdims=True))
        l = jnp.sum(p, axis=-1, keepdims=True)
        heads.append((_dot(p.astype(BF16), mv_ref[0, :, c]) / l).astype(BF16))
    y2 = _dot(jnp.concatenate(heads, axis=1), wo_ref[...])
    o_ref[0] = x1 + _rms(y2, gpost2_ref[...])


def _mix_out(a, r, x, w_out, g_post, g_pre, w_q, mem_k, mem_v, w_o, g_post2, tm=512):
    batch, seq, d = x.shape
    half = a.shape[-1]
    n_mem = mem_k.shape[1]
    tok = lambda w: pl.BlockSpec((1, tm, w), lambda b, t: (b, t, 0))
    memblk = pl.BlockSpec((1, n_mem, d), lambda b, t: (b, 0, 0))
    vec = _resident((1, d))
    mat = _resident((d, d))
    return pl.pallas_call(
        _mix_out_kernel,
        out_shape=jax.ShapeDtypeStruct((batch, seq, d), F32),
        grid=(batch, seq // tm),
        in_specs=[tok(half), tok(half), tok(d), mat, vec, vec, mat, memblk, memblk, mat, vec],
        out_specs=tok(d),
        compiler_params=_params("parallel", "parallel"),
        name="mix_out",
    )(a, r, x, w_out, g_post, g_pre, w_q, mem_k, mem_v, w_o, g_post2)


def _ffn_kernel(x_ref, gpre_ref, wgu_ref, wdown_ref, gpost_ref, o_ref, *, chunks):
    x = x_ref[...]
    h = _rms(x, gpre_ref[...]).astype(BF16)
    hidden = wdown_ref.shape[0]
    y = None
    for lo, hi in chunks:
        g = _dot(h, wgu_ref[:, lo:hi])
        u = _dot(h, wgu_ref[:, hidden + lo:hidden + hi])
        act = (g * (1.0 / (1.0 + jnp.exp(-g))) * u).astype(BF16)
        part = _dot(act, wdown_ref[lo:hi, :])
        y = part if y is None else y + part
    o_ref[...] = x + _rms(y, gpost_ref[...])


def _ffn(x2d, g_pre, w_gu, w_down, g_post, tm=512):
    tokens, d = x2d.shape
    hidden = w_down.shape[0]
    mxu = 256
    cut = (hidden // 2) // mxu * mxu
    chunks = ((0, cut), (cut, hidden))
    row = pl.BlockSpec((tm, d), lambda t: (t, 0))
    vec = _resident((1, d))
    return pl.pallas_call(
        functools.partial(_ffn_kernel, chunks=chunks),
        out_shape=jax.ShapeDtypeStruct((tokens, d), F32),
        grid=(tokens // tm,),
        in_specs=[row, vec, _resident((d, 2 * hidden)), _resident((hidden, d)), vec],
        out_specs=row,
        compiler_params=_params("parallel"),
        name="ffn",
    )(x2d, g_pre, w_gu, w_down, g_post)


def kernel(x, mem, pre_mix_g, post_mix_g, w_in, attn_gn_g, ret_gn_g, w_out,
           pre_mem_g, post_mem_g, mem_norm_g, w_q_mem, w_kv_mem, w_o_mem,
           pre_ffn_g, post_ffn_g, w_gate_up, w_down):
    batch, seq, d = x.shape
    depth = w_in.shape[0]
    for l in range(depth):
        bf = lambda w: w[l].astype(BF16)
        vec = lambda g: g[l][None, :]
        streams = _in_proj(x.reshape(batch * seq, d), vec(pre_mix_g), bf(w_in), seq)
        aq, ak, av, rq, rk, rv, rg = (s.reshape(batch, seq, -1) for s in streams)
        a = _attention(aq, ak, av, vec(attn_gn_g))
        r = _retention(rq, rk, rv, rg, vec(ret_gn_g))
        mem_k, mem_v = _mem_kv(mem, vec(mem_norm_g), bf(w_kv_mem))
        x = _mix_out(a, r, x, bf(w_out), vec(post_mix_g), vec(pre_mem_g), bf(w_q_mem),
                     mem_k, mem_v, bf(w_o_mem), vec(post_mem_g))
        x = _ffn(x.reshape(batch * seq, d), vec(pre_ffn_g), bf(w_gate_up), bf(w_down),
                 vec(post_ffn_g)).reshape(batch, seq, d)
    return x
```

```python
import functools

import jax
import jax.numpy as jnp
from jax import lax
from jax.experimental import pallas as pl
from jax.experimental.pallas import tpu as pltpu

F32 = jnp.float32
BF16 = jnp.bfloat16

RMS_EPS = 1e-6
GN_EPS = 1e-5
ATT_HEADS = 8
ATT_HEAD_DIM = 64
ATT_WIDTH = ATT_HEADS * ATT_HEAD_DIM
ROT_DIM = ATT_HEAD_DIM // 4
ROPE_THETA = 500000.0
DILATED_PATTERNS = ((128, 1), (512, 4), (2048, 16))
RET_HEADS = 4
RET_HEAD_DIM = 128
RET_WIDTH = RET_HEADS * RET_HEAD_DIM
RET_CHUNK = 128
RET_ROPE_THETA = 10000.0
XATT_HEADS = 4

LANES = 128
BAND = 128
MASKED = -1e30
VMEM_LIMIT = 56 * 1024 * 1024


def _resident(shape):
    zeros = (0,) * len(shape)
    return pl.BlockSpec(shape, lambda *_: zeros, pipeline_mode=pl.Buffered(1))


def _params(*sem):
    return pltpu.CompilerParams(dimension_semantics=sem, vmem_limit_bytes=VMEM_LIMIT)


def _rms(x, g):
    return x * lax.rsqrt(jnp.mean(x * x, axis=-1, keepdims=True) + RMS_EPS) * g


def _dot(a, b):
    return jnp.dot(a, b, preferred_element_type=F32)


def _dot_nt(a, b):
    return lax.dot_general(a, b, (((1,), (1,)), ((), ())), preferred_element_type=F32)


def _dot_tn(a, b):
    return lax.dot_general(a, b, (((0,), (0,)), ((), ())), preferred_element_type=F32)


def _in_proj_kernel(x_ref, g_ref, w_ref, ca_ref, sa1_ref, sa2_ref, cr_ref, sr_ref, *refs):
    att_refs = refs[:3 * len(DILATED_PATTERNS)]
    rq_ref, rk_ref, rv_ref, rg_ref, scr_ref = refs[len(att_refs):]
    tm = x_ref.shape[1]
    h = _rms(x_ref[0], g_ref[...]).astype(BF16)

    def proj(j, width):
        return _dot(h, w_ref[:, j:j + width])

    def rope_attn(t):
        outs = []
        for c in range(0, ATT_WIDTH, LANES):
            tg = t[:, c:c + LANES]
            outs.append(tg * ca_ref[...]
                        + pltpu.roll(tg, LANES - ROT_DIM // 2, 1) * sa1_ref[...]
                        + pltpu.roll(tg, ROT_DIM // 2, 1) * sa2_ref[...])
        return jnp.concatenate(outs, axis=1)

    def rope_ret(t):
        outs = []
        for c in range(0, RET_WIDTH, LANES):
            tg = t[:, c:c + LANES]
            outs.append(tg * cr_ref[...] + pltpu.roll(tg, RET_HEAD_DIM // 2, 1) * sr_ref[...])
        return jnp.concatenate(outs, axis=1)

    def emit(stream, val):
        for p, (_, r) in enumerate(DILATED_PATTERNS):
            out_ref = att_refs[3 * p + stream]
            if r == 1:
                out_ref[0, 0] = val.astype(BF16)
                continue
            for gi, c0 in enumerate(range(0, ATT_WIDTH, LANES)):
                if p == 1:
                    scr_ref[stream, gi] = val[:, c0:c0 + LANES]
                for c in range(r):
                    rows = scr_ref[stream, gi, pl.ds(c, tm // r, stride=r), :]
                    out_ref[0, c, :, c0:c0 + LANES] = rows.astype(BF16)

    o = 0
    emit(0, rope_attn(proj(o, ATT_WIDTH)) * (ATT_HEAD_DIM ** -0.5))
    o += ATT_WIDTH
    emit(1, rope_attn(proj(o, ATT_WIDTH)))
    o += ATT_WIDTH
    emit(2, proj(o, ATT_WIDTH))
    o += ATT_WIDTH
    rq_ref[0] = rope_ret(proj(o, RET_WIDTH)).astype(BF16)
    o += RET_WIDTH
    rk_ref[0] = (rope_ret(proj(o, RET_WIDTH)) * (RET_HEAD_DIM ** -0.5)).astype(BF16)
    o += RET_WIDTH
    rv_ref[0] = proj(o, RET_WIDTH).astype(BF16)
    o += RET_WIDTH
    rg_ref[0] = proj(o, RET_WIDTH).astype(BF16)


def _rope_tables(seq):
    pos = jnp.arange(seq, dtype=F32)[:, None]
    half = ROT_DIM // 2
    inv = ROPE_THETA ** (-(jnp.arange(0, ROT_DIM, 2, dtype=F32) / ROT_DIM))
    ang = pos * inv[None, :]
    cos, sin = jnp.cos(ang), jnp.sin(ang)
    one = jnp.ones((seq, ATT_HEAD_DIM - ROT_DIM), F32)
    zero = jnp.zeros((seq, ATT_HEAD_DIM - ROT_DIM), F32)
    zh = jnp.zeros((seq, half), F32)
    ca = jnp.concatenate([cos, cos, one], axis=1)
    sa1 = jnp.concatenate([-sin, zh, zero], axis=1)
    sa2 = jnp.concatenate([zh, sin, zero], axis=1)
    reps = LANES // ATT_HEAD_DIM
    ca, sa1, sa2 = (jnp.tile(t, (1, reps)) for t in (ca, sa1, sa2))
    inv_r = RET_ROPE_THETA ** (-(jnp.arange(0, RET_HEAD_DIM, 2, dtype=F32) / RET_HEAD_DIM))
    ang_r = pos * inv_r[None, :]
    cr = jnp.concatenate([jnp.cos(ang_r), jnp.cos(ang_r)], axis=1)
    sr = jnp.concatenate([-jnp.sin(ang_r), jnp.sin(ang_r)], axis=1)
    return ca, sa1, sa2, cr, sr


def _in_proj(x, g, w_in, tm=512):
    batch, seq, d = x.shape
    width = w_in.shape[1]
    tok = lambda w: pl.BlockSpec((1, tm, w), lambda b, t: (b, t, 0))
    tab = pl.BlockSpec((tm, LANES), lambda b, t: (t, 0))
    att_shapes, att_specs = [], []
    for _, r in DILATED_PATTERNS:
        assert tm % (r * 16) == 0
        att_shapes += [jax.ShapeDtypeStruct((batch, r, seq // r, ATT_WIDTH), BF16)] * 3
        att_specs += [pl.BlockSpec((1, r, tm // r, ATT_WIDTH), lambda b, t: (b, 0, t, 0))] * 3
    ret_shape = jax.ShapeDtypeStruct((batch, seq, RET_WIDTH), BF16)
    outs = pl.pallas_call(
        _in_proj_kernel,
        out_shape=tuple(att_shapes) + (ret_shape,) * 4,
        grid=(batch, seq // tm),
        in_specs=[tok(d), _resident((1, d)), _resident((d, width))] + [tab] * 5,
        out_specs=tuple(att_specs) + (tok(RET_WIDTH),) * 4,
        scratch_shapes=[pltpu.VMEM((3, ATT_WIDTH // LANES, tm, LANES), F32)],
        compiler_params=_params("parallel", "parallel"),
        name="in_proj",
    )(x, g, w_in, *_rope_tables(seq))
    return outs[:len(att_shapes)], outs[len(att_shapes):]


def _attn_kernel(*refs, nc, nq, seq_tiles, has_prev, is_last):
    refs = list(refs)
    q_ref, km_ref, kh_ref, vm_ref, vh_ref = refs[:5]
    del refs[:5]
    if has_prev:
        po_ref, ps_ref = refs[:2]
        del refs[:2]
    if is_last:
        g_ref = refs.pop(0)
    o_ref = refs.pop(0)
    if not is_last:
        s_ref = refs.pop(0)
    kbuf, vbuf = refs[:2]
    del refs[:2]
    if nc > 1:
        obuf, sbuf = refs

    half = ATT_HEAD_DIM
    for cls in range(nc):
        kbuf[cls, :BAND] = kh_ref[0, cls]
        kbuf[cls, BAND:] = km_ref[0, cls]
        vbuf[cls, :BAND] = vh_ref[0, cls]
        vbuf[cls, BAND:] = vm_ref[0, cls]

    key = lax.broadcasted_iota(jnp.int32, (2 * BAND, 2 * BAND), 0)
    qry = lax.broadcasted_iota(jnp.int32, (2 * BAND, 2 * BAND), 1) & (BAND - 1)
    mask_band = (key >= qry) & (key <= qry + BAND)
    if seq_tiles == 1:
        mask_first = (key >= BAND) & (key <= qry + BAND)
    else:
        floor = jnp.where(pl.program_id(1) == 0, BAND, 0)
        mask_first = (key >= jnp.maximum(qry, floor)) & (key <= qry + BAND)
    lane = lax.broadcasted_iota(jnp.int32, (BAND, LANES), 1)
    lo = lane < half
    row8 = lax.broadcasted_iota(jnp.int32, (8, BAND), 0)

    def block(cls, qstart, mask):
        qs = pl.ds(qstart, BAND)
        ks = pl.ds(qstart, 2 * BAND)
        if has_prev:
            prev_lse = ps_ref[0, cls, qs, :].T
        stats = jnp.zeros((8, BAND), F32)
        for gi, c0 in enumerate(range(0, ATT_WIDTH, LANES)):
            cols = slice(c0, c0 + LANES)
            qp = q_ref[0, cls, qs, cols]
            zero = jnp.zeros_like(qp)
            q2 = jnp.concatenate([jnp.where(lo, qp, zero), jnp.where(lo, zero, qp)], axis=0)
            s = _dot_nt(kbuf[cls, ks, cols], q2)
            s = jnp.where(mask, s, MASKED)
            m = jnp.max(s, axis=0, keepdims=True)
            p = jnp.exp(s - m)
            l = jnp.sum(p, axis=0, keepdims=True)
            ot = _dot_tn(vbuf[cls, ks, cols], p.astype(BF16))
            inv = 1.0 / l
            lse = m + jnp.log(l)
            if has_prev:
                prev_o = po_ref[0, cls, qs, cols].astype(F32).T
            parts = []
            for hh in range(LANES // half):
                head = 2 * gi + hh
                qcols = slice(hh * BAND, (hh + 1) * BAND)
                rows = slice(hh * half, (hh + 1) * half)
                o = ot[rows, qcols] * inv[:, qcols]
                lse_h = lse[:, qcols]
                if has_prev:
                    lp = prev_lse[head:head + 1, :]
                    top = jnp.maximum(lp, lse_h)
                    wp = jnp.exp(lp - top)
                    wn = jnp.exp(lse_h - top)
                    den = wp + wn
                    o = (wp * prev_o[rows, :] + wn * o) / den
                    lse_h = top + jnp.log(den)
                if is_last:
                    ms = jnp.mean(o * o, axis=0, keepdims=True)
                    o = o * lax.rsqrt(ms + RMS_EPS)
                parts.append(o)
                stats = jnp.where(row8 == head, lse_h, stats)
            o = jnp.concatenate(parts, axis=0).T
            if is_last:
                o = o * g_ref[:, cols]
            if nc > 1:
                obuf[gi, pl.ds(qstart * nc + cls, BAND, stride=nc), :] = o
            else:
                o_ref[0, 0, qs, cols] = o.astype(o_ref.dtype)
        if not is_last:
            st = jnp.concatenate([stats, jnp.zeros((BAND - 8, BAND), F32)], axis=0).T
            if nc > 1:
                sbuf[pl.ds(qstart * nc + cls, BAND, stride=nc), :] = st
            else:
                s_ref[0, 0, qs, :] = st

    if nc > 1:
        for cls in range(nc):
            for n in range(nq):
                block(cls, n * BAND, mask_first if n == 0 else mask_band)
        for gi, c0 in enumerate(range(0, ATT_WIDTH, LANES)):
            o_ref[0, 0, :, c0:c0 + LANES] = obuf[gi].astype(o_ref.dtype)
        if not is_last:
            s_ref[0, 0] = sbuf[...]
    else:
        block(0, 0, mask_first)

        def body(n, carry):
            block(0, pl.multiple_of(n * BAND, BAND), mask_band)
            return carry

        lax.fori_loop(1, nq, body, 0)


def _attention(qkv, gn_g):
    w = ATT_WIDTH
    order = sorted(range(len(DILATED_PATTERNS)), key=lambda p: -DILATED_PATTERNS[p][1])
    prev = None
    for step, p in enumerate(order):
        window, r = DILATED_PATTERNS[p]
        q, k, v = qkv[p]
        batch, _, length, _ = q.shape
        is_last = step == len(order) - 1
        r_next = 1 if is_last else DILATED_PATTERNS[order[step + 1]][1]
        nc = r // r_next
        assert window // r == BAND and r == nc * r_next
        nq = 2 if nc > 1 else 8
        seq_tiles = length // (nq * BAND)
        assert length % (nq * BAND) == 0
        rows = nq * BAND

        if nc > 1:
            shape5 = lambda t: t.reshape(batch, nc, r_next, length, t.shape[-1])
            main = lambda wd: pl.BlockSpec((1, nc, None, rows, wd), lambda b, t: (b, 0, t // seq_tiles, t % seq_tiles, 0))
            halo = lambda wd: pl.BlockSpec(
                (1, nc, None, BAND, wd),
                lambda b, t: (b, 0, t // seq_tiles, jnp.maximum(t % seq_tiles * nq - 1, 0), 0))
            grid = (batch, r_next * seq_tiles)
            out_blk = lambda wd: pl.BlockSpec((1, 1, nc * rows, wd), lambda b, t: (b, t // seq_tiles, t % seq_tiles, 0))
        else:
            shape5 = lambda t: t
            main = lambda wd: pl.BlockSpec((1, 1, rows, wd), lambda b, t: (b, 0, t, 0))
            halo = lambda wd: pl.BlockSpec((1, 1, BAND, wd), lambda b, t: (b, 0, jnp.maximum(t * nq - 1, 0), 0))
            grid = (batch, seq_tiles)
            out_blk = lambda wd: pl.BlockSpec((1, 1, rows, wd), lambda b, t: (b, 0, t, 0))
        if nc > 1 and seq_tiles > 1:
            assert r_next == 1

        args = [shape5(q), shape5(k), shape5(k), shape5(v), shape5(v)]
        in_specs = [main(w), main(w), halo(w), main(w), halo(w)]
        if prev is not None:
            args += [shape5(prev[0]), shape5(prev[1])]
            in_specs += [main(w), main(LANES)]
        if is_last:
            args.append(gn_g)
            in_specs.append(_resident((1, w)))
        o_shape = jax.ShapeDtypeStruct((batch, r_next, length * nc, w), BF16)
        s_shape = jax.ShapeDtypeStruct((batch, r_next, length * nc, LANES), F32)
        scratch = [pltpu.VMEM((nc, rows + BAND, w), BF16)] * 2
        if nc > 1:
            scratch += [pltpu.VMEM((w // LANES, nc * rows, LANES), F32), pltpu.VMEM((nc * rows, LANES), F32)]
        res = pl.pallas_call(
            functools.partial(_attn_kernel, nc=nc, nq=nq, seq_tiles=seq_tiles,
                              has_prev=prev is not None, is_last=is_last),
            out_shape=o_shape if is_last else (o_shape, s_shape),
            grid=grid,
            in_specs=in_specs,
            out_specs=out_blk(w) if is_last else (out_blk(w), out_blk(LANES)),
            scratch_shapes=scratch,
            compiler_params=_params("parallel", "arbitrary"),
            name=f"attn_r{r}",
        )(*args)
        if is_last:
            return res.reshape(batch, length * nc, w)
        prev = res


def _retention_kernel(q_ref, k_ref, v_ref, gate_ref, dmask_ref, xi_ref, zeta_ref, cdec_ref, g_ref,
                      o_ref, state_ref):
    @pl.when(pl.program_id(1) == 0)
    def _():
        state_ref[...] = jnp.zeros_like(state_ref)

    for h in range(RET_HEADS):
        c = slice(h * RET_HEAD_DIM, (h + 1) * RET_HEAD_DIM)
        q = q_ref[0, :, c]
        k = k_ref[0, :, c]
        v = v_ref[0, :, c]
        state = state_ref[h]
        inner = _dot_nt(q, k) * dmask_ref[h]
        o = _dot(inner.astype(BF16), v) + _dot(q, state.astype(BF16)) * xi_ref[h]
        kz = (k.astype(F32) * zeta_ref[h]).astype(BF16)
        state_ref[h] = state * cdec_ref[h] + _dot_tn(kz, v)
        mu = jnp.mean(o, axis=-1, keepdims=True)
        var = jnp.mean(jnp.square(o - mu), axis=-1, keepdims=True)
        y = (o - mu) * lax.rsqrt(var + GN_EPS) * g_ref[:, c]
        gate = gate_ref[0, :, c].astype(F32)
        o_ref[0, :, c] = (gate * (1.0 / (1.0 + jnp.exp(-gate))) * y).astype(o_ref.dtype)


def _retention_tables():
    c = RET_CHUNK
    log_g = jnp.log(1.0 - 2.0 ** (-5.0 - jnp.arange(RET_HEADS, dtype=F32)))
    n = jnp.arange(c, dtype=F32)
    rel = n[:, None] - n[None, :]
    dmask = jnp.where(rel >= 0, jnp.exp(log_g[:, None, None] * jnp.maximum(rel, 0.0)), 0.0)
    xi = jnp.exp(log_g[:, None] * (n + 1.0))
    zeta = jnp.exp(log_g[:, None] * (c - 1.0 - n))
    cdec = jnp.exp(log_g * c)
    wide = lambda t: jnp.broadcast_to(t[:, :, None], (RET_HEADS, c, RET_HEAD_DIM))
    cdec = jnp.broadcast_to(cdec[:, None, None], (RET_HEADS, RET_HEAD_DIM, RET_HEAD_DIM))
    return dmask, wide(xi), wide(zeta), cdec


def _retention(rq, rk, rv, rg, gn_g):
    batch, seq, width = rq.shape
    c = RET_CHUNK
    blk = pl.BlockSpec((1, c, width), lambda b, n: (b, n, 0))
    tab = _resident((RET_HEADS, c, RET_HEAD_DIM))
    return pl.pallas_call(
        _retention_kernel,
        out_shape=jax.ShapeDtypeStruct((batch, seq, width), BF16),
        grid=(batch, seq // c),
        in_specs=[blk] * 4 + [tab] * 4 + [_resident((1, width))],
        out_specs=blk,
        scratch_shapes=[pltpu.VMEM((RET_HEADS, RET_HEAD_DIM, RET_HEAD_DIM), F32)],
        compiler_params=_params("parallel", "arbitrary"),
        name="retention",
    )(rq, rk, rv, rg, *_retention_tables(), gn_g)


def _mem_kv_kernel(mem_ref, g_ref, w_ref, k_ref, v_ref):
    d = mem_ref.shape[-1]
    kv = _dot(_rms(mem_ref[0], g_ref[...]).astype(BF16), w_ref[...])
    k_ref[0] = kv[:, :d].astype(BF16)
    v_ref[0] = kv[:, d:].astype(BF16)


def _mem_kv(mem, g, w_kv):
    batch, n_mem, d = mem.shape
    blk = pl.BlockSpec((1, n_mem, d), lambda b: (b, 0, 0))
    out = jax.ShapeDtypeStruct((batch, n_mem, d), BF16)
    return pl.pallas_call(
        _mem_kv_kernel,
        out_shape=(out, out),
        grid=(batch,),
        in_specs=[blk, _resident((1, d)), _resident((d, 2 * d))],
        out_specs=(blk, blk),
        compiler_params=_params("parallel"),
        name="mem_kv",
    )(mem, g, w_kv)


def _mix_out_kernel(a_ref, r_ref, x_ref, wout_ref, gpost_ref, gpre_ref, wq_ref, mk_ref, mv_ref,
                    wo_ref, gpost2_ref, o_ref):
    half = a_ref.shape[-1]
    y = _dot(a_ref[0], wout_ref[:half, :]) + _dot(r_ref[0], wout_ref[half:, :])
    x1 = x_ref[0] + _rms(y, gpost_ref[...])
    h = _rms(x1, gpre_ref[...]).astype(BF16)
    d = x1.shape[-1]
    dh = d // XATT_HEADS
    q = (_dot(h, wq_ref[...]) * (dh ** -0.5)).astype(BF16)
    heads = []
    for hd in range(XATT_HEADS):
        c = slice(hd * dh, (hd + 1) * dh)
        s = _dot_nt(q[:, c], mk_ref[0, :, c])
        p = jnp.exp(s - jnp.max(s, axis=-1, keepdims=True))
        l = jnp.sum(p, axis=-1, keepdims=True)
        heads.append((_dot(p.astype(BF16), mv_ref[0, :, c]) / l).astype(BF16))
    y2 = _dot(jnp.concatenate(heads, axis=1), wo_ref[...])
    o_ref[0] = x1 + _rms(y2, gpost2_ref[...])


def _mix_out(a, r, x, w_out, g_post, g_pre, w_q, mem_k, mem_v, w_o, g_post2, tm=512):
    batch, seq, d = x.shape
    half = a.shape[-1]
    n_mem = mem_k.shape[1]
    tok = lambda w: pl.BlockSpec((1, tm, w), lambda b, t: (b, t, 0))
    memblk = pl.BlockSpec((1, n_mem, d), lambda b, t: (b, 0, 0))
    vec = _resident((1, d))
    mat = _resident((d, d))
    return pl.pallas_call(
        _mix_out_kernel,
        out_shape=jax.ShapeDtypeStruct((batch, seq, d), F32),
        grid=(batch, seq // tm),
        in_specs=[tok(half), tok(half), tok(d), mat, vec, vec, mat, memblk, memblk, mat, vec],
        out_specs=tok(d),
        compiler_params=_params("parallel", "parallel"),
        name="mix_out",
    )(a, r, x, w_out, g_post, g_pre, w_q, mem_k, mem_v, w_o, g_post2)


def _ffn_kernel(x_ref, gpre_ref, wgu_ref, wdown_ref, gpost_ref, o_ref, *, chunks):
    x = x_ref[...]
    h = _rms(x, gpre_ref[...]).astype(BF16)
    hidden = wdown_ref.shape[0]
    y = None
    for lo, hi in chunks:
        g = _dot(h, wgu_ref[:, lo:hi])
        u = _dot(h, wgu_ref[:, hidden + lo:hidden + hi])
        act = (g * (1.0 / (1.0 + jnp.exp(-g))) * u).astype(BF16)
        part = _dot(act, wdown_ref[lo:hi, :])
        y = part if y is None else y + part
    o_ref[...] = x + _rms(y, gpost_ref[...])


def _ffn(x2d, g_pre, w_gu, w_down, g_post, tm=512):
    tokens, d = x2d.shape
    hidden = w_down.shape[0]
    mxu = 256
    cut = (hidden // 2) // mxu * mxu
    chunks = ((0, cut), (cut, hidden))
    row = pl.BlockSpec((tm, d), lambda t: (t, 0))
    vec = _resident((1, d))
    return pl.pallas_call(
        functools.partial(_ffn_kernel, chunks=chunks),
        out_shape=jax.ShapeDtypeStruct((tokens, d), F32),
        grid=(tokens // tm,),
        in_specs=[row, vec, _resident((d, 2 * hidden)), _resident((hidden, d)), vec],
        out_specs=row,
        compiler_params=_params("parallel"),
        name="ffn",
    )(x2d, g_pre, w_gu, w_down, g_post)


def kernel(x, mem, pre_mix_g, post_mix_g, w_in, attn_gn_g, ret_gn_g, w_out,
           pre_mem_g, post_mem_g, mem_norm_g, w_q_mem, w_kv_mem, w_o_mem,
           pre_ffn_g, post_ffn_g, w_gate_up, w_down):
    batch, seq, d = x.shape
    depth = w_in.shape[0]
    for l in range(depth):
        bf = lambda w: w[l].astype(BF16)
        vec = lambda g: g[l][None, :]
        att, (rq, rk, rv, rg) = _in_proj(x, vec(pre_mix_g), bf(w_in))
        qkv = [att[3 * p:3 * p + 3] for p in range(len(DILATED_PATTERNS))]
        a = _attention(qkv, vec(attn_gn_g))
        r = _retention(rq, rk, rv, rg, vec(ret_gn_g))
        mem_k, mem_v = _mem_kv(mem, vec(mem_norm_g), bf(w_kv_mem))
        x = _mix_out(a, r, x, bf(w_out), vec(post_mix_g), vec(pre_mem_g), bf(w_q_mem),
                     mem_k, mem_v, bf(w_o_mem), vec(post_mem_g))
        x = _ffn(x.reshape(batch * seq, d), vec(pre_ffn_g), bf(w_gate_up), bf(w_down),
                 vec(post_ffn_g)).reshape(batch, seq, d)
    return x
```

```python
import functools

import jax
import jax.numpy as jnp
from jax import lax
from jax.experimental import pallas as pl
from jax.experimental.pallas import tpu as pltpu

F32 = jnp.float32
BF16 = jnp.bfloat16

RMS_EPS = 1e-6
GN_EPS = 1e-5
ATT_HEADS = 8
ATT_HEAD_DIM = 64
ATT_WIDTH = ATT_HEADS * ATT_HEAD_DIM
ROT_DIM = ATT_HEAD_DIM // 4
ROPE_THETA = 500000.0
DILATED_PATTERNS = ((128, 1), (512, 4), (2048, 16))
RET_HEADS = 4
RET_HEAD_DIM = 128
RET_WIDTH = RET_HEADS * RET_HEAD_DIM
RET_CHUNK = 128
RET_ROPE_THETA = 10000.0
XATT_HEADS = 4

LANES = 128
BAND = 128
MASKED = -1e30
VMEM_LIMIT = 56 * 1024 * 1024


def _resident(shape):
    zeros = (0,) * len(shape)
    return pl.BlockSpec(shape, lambda *_: zeros, pipeline_mode=pl.Buffered(1))


def _params(*sem):
    return pltpu.CompilerParams(dimension_semantics=sem, vmem_limit_bytes=VMEM_LIMIT)


def _rms(x, g):
    return x * lax.rsqrt(jnp.mean(x * x, axis=-1, keepdims=True) + RMS_EPS) * g


def _dot(a, b):
    return jnp.dot(a, b, preferred_element_type=F32)


def _dot_nt(a, b):
    return lax.dot_general(a, b, (((1,), (1,)), ((), ())), preferred_element_type=F32)


def _dot_tn(a, b):
    return lax.dot_general(a, b, (((0,), (0,)), ((), ())), preferred_element_type=F32)


def _in_proj_kernel(x_ref, g_ref, w_ref, ca_ref, sa1_ref, sa2_ref, cr_ref, sr_ref, *refs):
    att_refs = refs[:3 * len(DILATED_PATTERNS)]
    rq_ref, rk_ref, rv_ref, rg_ref, scr_ref = refs[len(att_refs):]
    tm = x_ref.shape[1]
    h = _rms(x_ref[0], g_ref[...]).astype(BF16)

    def proj(j, width):
        return _dot(h, w_ref[:, j:j + width])

    def rope_attn(t):
        outs = []
        for c in range(0, ATT_WIDTH, LANES):
            tg = t[:, c:c + LANES]
            outs.append(tg * ca_ref[...]
                        + pltpu.roll(tg, LANES - ROT_DIM // 2, 1) * sa1_ref[...]
                        + pltpu.roll(tg, ROT_DIM // 2, 1) * sa2_ref[...])
        return jnp.concatenate(outs, axis=1)

    def rope_ret(t):
        outs = []
        for c in range(0, RET_WIDTH, LANES):
            tg = t[:, c:c + LANES]
            outs.append(tg * cr_ref[...] + pltpu.roll(tg, RET_HEAD_DIM // 2, 1) * sr_ref[...])
        return jnp.concatenate(outs, axis=1)

    def emit(stream, val):
        for p, (_, r) in enumerate(DILATED_PATTERNS):
            out_ref = att_refs[3 * p + stream]
            if r == 1:
                out_ref[0, 0] = val.astype(BF16)
                continue
            for gi, c0 in enumerate(range(0, ATT_WIDTH, LANES)):
                if p == 1:
                    scr_ref[stream, gi] = val[:, c0:c0 + LANES]
                for c in range(r):
                    rows = scr_ref[stream, gi, pl.ds(c, tm // r, stride=r), :]
                    out_ref[0, c, :, c0:c0 + LANES] = rows.astype(BF16)

    o = 0
    emit(0, rope_attn(proj(o, ATT_WIDTH)) * (ATT_HEAD_DIM ** -0.5))
    o += ATT_WIDTH
    emit(1, rope_attn(proj(o, ATT_WIDTH)))
    o += ATT_WIDTH
    emit(2, proj(o, ATT_WIDTH))
    o += ATT_WIDTH
    rq_ref[0] = rope_ret(proj(o, RET_WIDTH)).astype(BF16)
    o += RET_WIDTH
    rk_ref[0] = (rope_ret(proj(o, RET_WIDTH)) * (RET_HEAD_DIM ** -0.5)).astype(BF16)
    o += RET_WIDTH
    rv_ref[0] = proj(o, RET_WIDTH).astype(BF16)
    o += RET_WIDTH
    rg_ref[0] = proj(o, RET_WIDTH).astype(BF16)


def _rope_tables(seq):
    pos = jnp.arange(seq, dtype=F32)[:, None]
    half = ROT_DIM // 2
    inv = ROPE_THETA ** (-(jnp.arange(0, ROT_DIM, 2, dtype=F32) / ROT_DIM))
    ang = pos * inv[None, :]
    cos, sin = jnp.cos(ang), jnp.sin(ang)
    one = jnp.ones((seq, ATT_HEAD_DIM - ROT_DIM), F32)
    zero = jnp.zeros((seq, ATT_HEAD_DIM - ROT_DIM), F32)
    zh = jnp.zeros((seq, half), F32)
    ca = jnp.concatenate([cos, cos, one], axis=1)
    sa1 = jnp.concatenate([-sin, zh, zero], axis=1)
    sa2 = jnp.concatenate([zh, sin, zero], axis=1)
    reps = LANES // ATT_HEAD_DIM
    ca, sa1, sa2 = (jnp.tile(t, (1, reps)) for t in (ca, sa1, sa2))
    inv_r = RET_ROPE_THETA ** (-(jnp.arange(0, RET_HEAD_DIM, 2, dtype=F32) / RET_HEAD_DIM))
    ang_r = pos * inv_r[None, :]
    cr = jnp.concatenate([jnp.cos(ang_r), jnp.cos(ang_r)], axis=1)
    sr = jnp.concatenate([-jnp.sin(ang_r), jnp.sin(ang_r)], axis=1)
    return ca, sa1, sa2, cr, sr


def _in_proj(x, g, w_in, tm=512):
    batch, seq, d = x.shape
    width = w_in.shape[1]
    tok = lambda w: pl.BlockSpec((1, tm, w), lambda b, t: (b, t, 0))
    tab = pl.BlockSpec((tm, LANES), lambda b, t: (t, 0))
    att_shapes, att_specs = [], []
    for _, r in DILATED_PATTERNS:
        assert tm % (r * 16) == 0
        att_shapes += [jax.ShapeDtypeStruct((batch, r, seq // r, ATT_WIDTH), BF16)] * 3
        att_specs += [pl.BlockSpec((1, r, tm // r, ATT_WIDTH), lambda b, t: (b, 0, t, 0))] * 3
    ret_shape = jax.ShapeDtypeStruct((batch, seq, RET_WIDTH), BF16)
    outs = pl.pallas_call(
        _in_proj_kernel,
        out_shape=tuple(att_shapes) + (ret_shape,) * 4,
        grid=(batch, seq // tm),
        in_specs=[tok(d), _resident((1, d)), _resident((d, width))] + [tab] * 5,
        out_specs=tuple(att_specs) + (tok(RET_WIDTH),) * 4,
        scratch_shapes=[pltpu.VMEM((3, ATT_WIDTH // LANES, tm, LANES), F32)],
        compiler_params=_params("parallel", "parallel"),
        name="in_proj",
    )(x, g, w_in, *_rope_tables(seq))
    return outs[:len(att_shapes)], outs[len(att_shapes):]


def _attn_kernel(*refs, nc, nq, seq_tiles, has_prev, is_last):
    refs = list(refs)
    q_ref, km_ref, kh_ref, vm_ref, vh_ref = refs[:5]
    del refs[:5]
    if has_prev:
        po_ref, ps_ref = refs[:2]
        del refs[:2]
    if is_last:
        g_ref = refs.pop(0)
    o_ref = refs.pop(0)
    if not is_last:
        s_ref = refs.pop(0)
    kbuf, vbuf, sc_buf, p_buf, bias = refs[:5]
    del refs[:5]
    if nc > 1:
        obuf, sbuf = refs

    half = ATT_HEAD_DIM
    groups = ATT_WIDTH // LANES
    chunk = 32
    for cls in range(nc):
        kbuf[cls, :BAND] = kh_ref[0, cls]
        kbuf[cls, BAND:] = km_ref[0, cls]
        vbuf[cls, :BAND] = vh_ref[0, cls]
        vbuf[cls, BAND:] = vm_ref[0, cls]

    key = lax.broadcasted_iota(jnp.int32, (2 * BAND, 2 * BAND), 0)
    qry = lax.broadcasted_iota(jnp.int32, (2 * BAND, 2 * BAND), 1) & (BAND - 1)
    if seq_tiles == 1:
        floor = BAND
    else:
        floor = jnp.where(pl.program_id(1) == 0, BAND, 0)
    upper = key <= qry + BAND
    bias[0] = jnp.where((key >= jnp.maximum(qry, floor)) & upper, 0.0, MASKED)
    bias[1] = jnp.where((key >= qry) & upper, 0.0, MASKED)

    lane = lax.broadcasted_iota(jnp.int32, (BAND, LANES), 1)
    lo = lane < half
    row8 = lax.broadcasted_iota(jnp.int32, (8, BAND), 0)

    def scores(cls, qstart, slot):
        for gi, c0 in enumerate(range(0, ATT_WIDTH, LANES)):
            cols = slice(c0, c0 + LANES)
            qp = q_ref[0, cls, pl.ds(qstart, BAND), cols]
            zero = jnp.zeros_like(qp)
            q2 = jnp.concatenate([jnp.where(lo, qp, zero), jnp.where(lo, zero, qp)], axis=0)
            sc_buf[slot, gi] = _dot_nt(kbuf[cls, pl.ds(qstart, 2 * BAND), cols], q2)

    def softmax(slot, which):
        out = []
        for gi in range(groups):
            top = jnp.full((8, 2 * BAND), MASKED, F32)
            for c in range(0, 2 * BAND, chunk):
                sc = sc_buf[slot, gi, c:c + chunk, :] + bias[which, c:c + chunk, :]
                sc_buf[slot, gi, c:c + chunk, :] = sc
                for r8 in range(0, chunk, 8):
                    top = jnp.maximum(top, sc[r8:r8 + 8, :])
            m = jnp.max(top, axis=0, keepdims=True)
            tot = jnp.zeros((8, 2 * BAND), F32)
            for c in range(0, 2 * BAND, chunk):
                e = jnp.exp(sc_buf[slot, gi, c:c + chunk, :] - m)
                p_buf[gi, c:c + chunk, :] = e.astype(BF16)
                for r8 in range(0, chunk, 8):
                    tot = tot + e[r8:r8 + 8, :]
            out.append((m, jnp.sum(tot, axis=0, keepdims=True)))
        return out

    def finish(cls, qstart, ml):
        qs = pl.ds(qstart, BAND)
        ks = pl.ds(qstart, 2 * BAND)
        if has_prev:
            prev_lse = ps_ref[0, cls, qs, :].T
        stats = jnp.zeros((8, BAND), F32)
        for gi, c0 in enumerate(range(0, ATT_WIDTH, LANES)):
            cols = slice(c0, c0 + LANES)
            m, l = ml[gi]
            ot = _dot_tn(vbuf[cls, ks, cols], p_buf[gi])
            inv = 1.0 / l
            lse = m + jnp.log(l)
            if has_prev:
                prev_o = po_ref[0, cls, qs, cols].astype(F32).T
            parts = []
            for hh in range(LANES // half):
                head = 2 * gi + hh
                qcols = slice(hh * BAND, (hh + 1) * BAND)
                rows = slice(hh * half, (hh + 1) * half)
                w_new = inv[:, qcols]
                lse_h = lse[:, qcols]
                if has_prev:
                    lp = prev_lse[head:head + 1, :]
                    top = jnp.maximum(lp, lse_h)
                    wp = jnp.exp(lp - top)
                    wn = jnp.exp(lse_h - top)
                    den = wp + wn
                    o = (wp / den) * prev_o[rows, :] + (wn * w_new / den) * ot[rows, qcols]
                    lse_h = top + jnp.log(den)
                else:
                    o = ot[rows, qcols] * w_new
                if is_last:
                    ms = jnp.mean(o * o, axis=0, keepdims=True)
                    o = o * lax.rsqrt(ms + RMS_EPS)
                parts.append(o)
                stats = jnp.where(row8 == head, lse_h, stats)
            o = jnp.concatenate(parts, axis=0).T
            if is_last:
                o = o * g_ref[:, cols]
            if nc > 1:
                obuf[gi, pl.ds(qstart * nc + cls, BAND, stride=nc), :] = o
            else:
                o_ref[0, 0, qs, cols] = o.astype(o_ref.dtype)
        if not is_last:
            st = jnp.concatenate([stats, jnp.zeros((BAND - 8, BAND), F32)], axis=0).T
            if nc > 1:
                sbuf[pl.ds(qstart * nc + cls, BAND, stride=nc), :] = st
            else:
                s_ref[0, 0, qs, :] = st

    blocks = [(cls, n * BAND) for cls in range(nc) for n in range(nq)]
    scores(*blocks[0], 0)
    for i, (cls, qstart) in enumerate(blocks):
        if i + 1 < len(blocks):
            scores(*blocks[i + 1], (i + 1) % 2)
        finish(cls, qstart, softmax(i % 2, 0 if qstart == 0 else 1))
    if nc > 1:
        for gi, c0 in enumerate(range(0, ATT_WIDTH, LANES)):
            o_ref[0, 0, :, c0:c0 + LANES] = obuf[gi].astype(o_ref.dtype)
        if not is_last:
            s_ref[0, 0] = sbuf[...]


def _attention(qkv, gn_g):
    w = ATT_WIDTH
    order = sorted(range(len(DILATED_PATTERNS)), key=lambda p: -DILATED_PATTERNS[p][1])
    prev = None
    for step, p in enumerate(order):
        window, r = DILATED_PATTERNS[p]
        q, k, v = qkv[p]
        batch, _, length, _ = q.shape
        is_last = step == len(order) - 1
        r_next = 1 if is_last else DILATED_PATTERNS[order[step + 1]][1]
        nc = r // r_next
        assert window // r == BAND and r == nc * r_next
        nq = 2 if nc > 1 else 8
        seq_tiles = length // (nq * BAND)
        assert length % (nq * BAND) == 0
        rows = nq * BAND

        if nc > 1:
            shape5 = lambda t: t.reshape(batch, nc, r_next, length, t.shape[-1])
            main = lambda wd: pl.BlockSpec((1, nc, None, rows, wd), lambda b, t: (b, 0, t // seq_tiles, t % seq_tiles, 0))
            halo = lambda wd: pl.BlockSpec(
                (1, nc, None, BAND, wd),
                lambda b, t: (b, 0, t // seq_tiles, jnp.maximum(t % seq_tiles * nq - 1, 0), 0))
            grid = (batch, r_next * seq_tiles)
            out_blk = lambda wd: pl.BlockSpec((1, 1, nc * rows, wd), lambda b, t: (b, t // seq_tiles, t % seq_tiles, 0))
        else:
            shape5 = lambda t: t
            main = lambda wd: pl.BlockSpec((1, 1, rows, wd), lambda b, t: (b, 0, t, 0))
            halo = lambda wd: pl.BlockSpec((1, 1, BAND, wd), lambda b, t: (b, 0, jnp.maximum(t * nq - 1, 0), 0))
            grid = (batch, seq_tiles)
            out_blk = lambda wd: pl.BlockSpec((1, 1, rows, wd), lambda b, t: (b, 0, t, 0))
        if nc > 1 and seq_tiles > 1:
            assert r_next == 1

        args = [shape5(q), shape5(k), shape5(k), shape5(v), shape5(v)]
        in_specs = [main(w), main(w), halo(w), main(w), halo(w)]
        if prev is not None:
            args += [shape5(prev[0]), shape5(prev[1])]
            in_specs += [main(w), main(LANES)]
        if is_last:
            args.append(gn_g)
            in_specs.append(_resident((1, w)))
        o_shape = jax.ShapeDtypeStruct((batch, r_next, length * nc, w), BF16)
        s_shape = jax.ShapeDtypeStruct((batch, r_next, length * nc, LANES), F32)
        groups = w // LANES
        scratch = [pltpu.VMEM((nc, rows + BAND, w), BF16)] * 2 + [
            pltpu.VMEM((2, groups, 2 * BAND, 2 * BAND), F32),
            pltpu.VMEM((groups, 2 * BAND, 2 * BAND), BF16),
            pltpu.VMEM((2, 2 * BAND, 2 * BAND), F32)]
        if nc > 1:
            scratch += [pltpu.VMEM((w // LANES, nc * rows, LANES), F32), pltpu.VMEM((nc * rows, LANES), F32)]
        res = pl.pallas_call(
            functools.partial(_attn_kernel, nc=nc, nq=nq, seq_tiles=seq_tiles,
                              has_prev=prev is not None, is_last=is_last),
            out_shape=o_shape if is_last else (o_shape, s_shape),
            grid=grid,
            in_specs=in_specs,
            out_specs=out_blk(w) if is_last else (out_blk(w), out_blk(LANES)),
            scratch_shapes=scratch,
            compiler_params=_params("parallel", "arbitrary"),
            name=f"attn_r{r}",
        )(*args)
        if is_last:
            return res.reshape(batch, length * nc, w)
        prev = res


def _retention_kernel(q_ref, k_ref, v_ref, gate_ref, dmask_ref, xi_ref, zeta_ref, cdec_ref, g_ref,
                      o_ref, state_ref):
    @pl.when(pl.program_id(1) == 0)
    def _():
        state_ref[...] = jnp.zeros_like(state_ref)

    for h in range(RET_HEADS):
        c = slice(h * RET_HEAD_DIM, (h + 1) * RET_HEAD_DIM)
        q = q_ref[0, :, c]
        k = k_ref[0, :, c]
        v = v_ref[0, :, c]
        state = state_ref[h]
        inner = _dot_nt(q, k) * dmask_ref[h]
        o = _dot(inner.astype(BF16), v) + _dot(q, state.astype(BF16)) * xi_ref[h]
        kz = (k.astype(F32) * zeta_ref[h]).astype(BF16)
        state_ref[h] = state * cdec_ref[h] + _dot_tn(kz, v)
        mu = jnp.mean(o, axis=-1, keepdims=True)
        var = jnp.mean(jnp.square(o - mu), axis=-1, keepdims=True)
        y = (o - mu) * lax.rsqrt(var + GN_EPS) * g_ref[:, c]
        gate = gate_ref[0, :, c].astype(F32)
        o_ref[0, :, c] = (gate * (1.0 / (1.0 + jnp.exp(-gate))) * y).astype(o_ref.dtype)


def _retention_tables():
    c = RET_CHUNK
    log_g = jnp.log(1.0 - 2.0 ** (-5.0 - jnp.arange(RET_HEADS, dtype=F32)))
    n = jnp.arange(c, dtype=F32)
    rel = n[:, None] - n[None, :]
    dmask = jnp.where(rel >= 0, jnp.exp(log_g[:, None, None] * jnp.maximum(rel, 0.0)), 0.0)
    xi = jnp.exp(log_g[:, None] * (n + 1.0))
    zeta = jnp.exp(log_g[:, None] * (c - 1.0 - n))
    cdec = jnp.exp(log_g * c)
    wide = lambda t: jnp.broadcast_to(t[:, :, None], (RET_HEADS, c, RET_HEAD_DIM))
    cdec = jnp.broadcast_to(cdec[:, None, None], (RET_HEADS, RET_HEAD_DIM, RET_HEAD_DIM))
    return dmask, wide(xi), wide(zeta), cdec


def _retention(rq, rk, rv, rg, gn_g):
    batch, seq, width = rq.shape
    c = RET_CHUNK
    blk = pl.BlockSpec((1, c, width), lambda b, n: (b, n, 0))
    tab = _resident((RET_HEADS, c, RET_HEAD_DIM))
    return pl.pallas_call(
        _retention_kernel,
        out_shape=jax.ShapeDtypeStruct((batch, seq, width), BF16),
        grid=(batch, seq // c),
        in_specs=[blk] * 4 + [tab] * 4 + [_resident((1, width))],
        out_specs=blk,
        scratch_shapes=[pltpu.VMEM((RET_HEADS, RET_HEAD_DIM, RET_HEAD_DIM), F32)],
        compiler_params=_params("parallel", "arbitrary"),
        name="retention",
    )(rq, rk, rv, rg, *_retention_tables(), gn_g)


def _mem_kv_kernel(mem_ref, g_ref, w_ref, k_ref, v_ref):
    d = mem_ref.shape[-1]
    kv = _dot(_rms(mem_ref[0], g_ref[...]).astype(BF16), w_ref[...])
    k_ref[0] = kv[:, :d].astype(BF16)
    v_ref[0] = kv[:, d:].astype(BF16)


def _mem_kv(mem, g, w_kv):
    batch, n_mem, d = mem.shape
    blk = pl.BlockSpec((1, n_mem, d), lambda b: (b, 0, 0))
    out = jax.ShapeDtypeStruct((batch, n_mem, d), BF16)
    return pl.pallas_call(
        _mem_kv_kernel,
        out_shape=(out, out),
        grid=(batch,),
        in_specs=[blk, _resident((1, d)), _resident((d, 2 * d))],
        out_specs=(blk, blk),
        compiler_params=_params("parallel"),
        name="mem_kv",
    )(mem, g, w_kv)


def _mix_out_kernel(a_ref, r_ref, x_ref, wout_ref, gpost_ref, gpre_ref, wq_ref, mk_ref, mv_ref,
                    wo_ref, gpost2_ref, o_ref):
    half = a_ref.shape[-1]
    y = _dot(a_ref[0], wout_ref[:half, :]) + _dot(r_ref[0], wout_ref[half:, :])
    x1 = x_ref[0] + _rms(y, gpost_ref[...])
    h = _rms(x1, gpre_ref[...]).astype(BF16)
    d = x1.shape[-1]
    dh = d // XATT_HEADS
    q = (_dot(h, wq_ref[...]) * (dh ** -0.5)).astype(BF16)
    heads = []
    for hd in range(XATT_HEADS):
        c = slice(hd * dh, (hd + 1) * dh)
        s = _dot_nt(q[:, c], mk_ref[0, :, c])
        p = jnp.exp(s - jnp.max(s, axis=-1, keepdims=True))
        l = jnp.sum(p, axis=-1, keepdims=True)
        heads.append((_dot(p.astype(BF16), mv_ref[0, :, c]) / l).astype(BF16))
    y2 = _dot(jnp.concatenate(heads, axis=1), wo_ref[...])
    o_ref[0] = x1 + _rms(y2, gpost2_ref[...])


def _mix_out(a, r, x, w_out, g_post, g_pre, w_q, mem_k, mem_v, w_o, g_post2, tm=512):
    batch, seq, d = x.shape
    half = a.shape[-1]
    n_mem = mem_k.shape[1]
    tok = lambda w: pl.BlockSpec((1, tm, w), lambda b, t: (b, t, 0))
    memblk = pl.BlockSpec((1, n_mem, d), lambda b, t: (b, 0, 0))
    vec = _resident((1, d))
    mat = _resident((d, d))
    return pl.pallas_call(
        _mix_out_kernel,
        out_shape=jax.ShapeDtypeStruct((batch, seq, d), F32),
        grid=(batch, seq // tm),
        in_specs=[tok(half), tok(half), tok(d), mat, vec, vec, mat, memblk, memblk, mat, vec],
        out_specs=tok(d),
        compiler_params=_params("parallel", "parallel"),
        name="mix_out",
    )(a, r, x, w_out, g_post, g_pre, w_q, mem_k, mem_v, w_o, g_post2)


def _ffn_kernel(x_ref, gpre_ref, wgu_ref, wdown_ref, gpost_ref, o_ref, *, chunks):
    x = x_ref[...]
    h = _rms(x, gpre_ref[...]).astype(BF16)
    hidden = wdown_ref.shape[0]
    y = None
    for lo, hi in chunks:
        g = _dot(h, wgu_ref[:, lo:hi])
        u = _dot(h, wgu_ref[:, hidden + lo:hidden + hi])
        act = (g * (1.0 / (1.0 + jnp.exp(-g))) * u).astype(BF16)
        part = _dot(act, wdown_ref[lo:hi, :])
        y = part if y is None else y + part
    o_ref[...] = x + _rms(y, gpost_ref[...])


def _ffn(x2d, g_pre, w_gu, w_down, g_post, tm=512):
    tokens, d = x2d.shape
    hidden = w_down.shape[0]
    mxu = 256
    cut = (hidden // 2) // mxu * mxu
    chunks = ((0, cut), (cut, hidden))
    row = pl.BlockSpec((tm, d), lambda t: (t, 0))
    vec = _resident((1, d))
    return pl.pallas_call(
        functools.partial(_ffn_kernel, chunks=chunks),
        out_shape=jax.ShapeDtypeStruct((tokens, d), F32),
        grid=(tokens // tm,),
        in_specs=[row, vec, _resident((d, 2 * hidden)), _resident((hidden, d)), vec],
        out_specs=row,
        compiler_params=_params("parallel"),
        name="ffn",
    )(x2d, g_pre, w_gu, w_down, g_post)


def kernel(x, mem, pre_mix_g, post_mix_g, w_in, attn_gn_g, ret_gn_g, w_out,
           pre_mem_g, post_mem_g, mem_norm_g, w_q_mem, w_kv_mem, w_o_mem,
           pre_ffn_g, post_ffn_g, w_gate_up, w_down):
    batch, seq, d = x.shape
    depth = w_in.shape[0]
    for l in range(depth):
        bf = lambda w: w[l].astype(BF16)
        vec = lambda g: g[l][None, :]
        att, (rq, rk, rv, rg) = _in_proj(x, vec(pre_mix_g), bf(w_in))
        qkv = [att[3 * p:3 * p + 3] for p in range(len(DILATED_PATTERNS))]
        a = _attention(qkv, vec(attn_gn_g))
        r = _retention(rq, rk, rv, rg, vec(ret_gn_g))
        mem_k, mem_v = _mem_kv(mem, vec(mem_norm_g), bf(w_kv_mem))
        x = _mix_out(a, r, x, bf(w_out), vec(post_mix_g), vec(pre_mem_g), bf(w_q_mem),
                     mem_k, mem_v, bf(w_o_mem), vec(post_mem_g))
        x = _ffn(x.reshape(batch * seq, d), vec(pre_ffn_g), bf(w_gate_up), bf(w_down),
                 vec(post_ffn_g)).reshape(batch, seq, d)
    return x
```

```python
import functools

import jax
import jax.numpy as jnp
from jax import lax
from jax.experimental import pallas as pl
from jax.experimental.pallas import tpu as pltpu

F32 = jnp.float32
BF16 = jnp.bfloat16

RMS_EPS = 1e-6
GN_EPS = 1e-5
ATT_HEADS = 8
ATT_HEAD_DIM = 64
ATT_WIDTH = ATT_HEADS * ATT_HEAD_DIM
ROT_DIM = ATT_HEAD_DIM // 4
ROPE_THETA = 500000.0
DILATED_PATTERNS = ((128, 1), (512, 4), (2048, 16))
RET_HEADS = 4
RET_HEAD_DIM = 128
RET_WIDTH = RET_HEADS * RET_HEAD_DIM
RET_CHUNK = 128
RET_ROPE_THETA = 10000.0
XATT_HEADS = 4

LANES = 128
BAND = 128
MASKED = -1e30
VMEM_LIMIT = 56 * 1024 * 1024


def _resident(shape):
    zeros = (0,) * len(shape)
    return pl.BlockSpec(shape, lambda *_: zeros, pipeline_mode=pl.Buffered(1))


def _params(*sem):
    return pltpu.CompilerParams(dimension_semantics=sem, vmem_limit_bytes=VMEM_LIMIT)


def _rms(x, g):
    return x * lax.rsqrt(jnp.mean(x * x, axis=-1, keepdims=True) + RMS_EPS) * g


def _dot(a, b):
    return jnp.dot(a, b, preferred_element_type=F32)


def _dot_nt(a, b):
    return lax.dot_general(a, b, (((1,), (1,)), ((), ())), preferred_element_type=F32)


def _dot_tn(a, b):
    return lax.dot_general(a, b, (((0,), (0,)), ((), ())), preferred_element_type=F32)


def _in_proj_kernel(x_ref, g_ref, w_ref, ca_ref, sa1_ref, sa2_ref, cr_ref, sr_ref, *refs):
    att_refs = refs[:3 * len(DILATED_PATTERNS)]
    ret_refs = refs[len(att_refs):len(att_refs) + 4]
    h_ref, acc_ref, scr_ref, scr2_ref = refs[len(att_refs) + 4:]
    tm = x_ref.shape[1]
    width = ATT_WIDTH
    n_att = 3
    (_, r_one), (_, r_mid), (_, r_top) = DILATED_PATTERNS
    assert r_one == 1 and r_top % r_mid == 0
    rows_per_step = 64
    h_ref[...] = _rms(x_ref[0], g_ref[...]).astype(BF16)

    def project(i, slot):
        acc_ref[slot] = _dot(h_ref[...], w_ref[:, i * width:(i + 1) * width])

    def rope_attn(t, rows):
        return (t * ca_ref[rows, :]
                + pltpu.roll(t, LANES - ROT_DIM // 2, 1) * sa1_ref[rows, :]
                + pltpu.roll(t, ROT_DIM // 2, 1) * sa2_ref[rows, :])

    def rope_ret(t, rows):
        return t * cr_ref[rows, :] + pltpu.roll(t, RET_HEAD_DIM // 2, 1) * sr_ref[rows, :]

    def post(i, slot):
        for gi, c0 in enumerate(range(0, width, LANES)):
            cols = slice(c0, c0 + LANES)
            for r0 in range(0, tm, rows_per_step):
                rows = slice(r0, r0 + rows_per_step)
                t = acc_ref[slot, rows, cols]
                if i == 0:
                    t = rope_attn(t, rows) * (ATT_HEAD_DIM ** -0.5)
                elif i == 1:
                    t = rope_attn(t, rows)
                elif i == n_att:
                    t = rope_ret(t, rows)
                elif i == n_att + 1:
                    t = rope_ret(t, rows) * (RET_HEAD_DIM ** -0.5)
                if i >= n_att:
                    ret_refs[i - n_att][0, rows, cols] = t.astype(BF16)
                else:
                    scr_ref[i, gi, rows, :] = t
                    att_refs[i][0, 0, rows, cols] = t.astype(BF16)
            if i < n_att:
                for c4 in range(r_mid):
                    picked = scr_ref[i, gi, pl.ds(c4, tm // r_mid, stride=r_mid), :]
                    att_refs[3 + i][0, c4, :, cols] = picked.astype(BF16)
                    scr2_ref[i, gi, c4] = picked
                for c4 in range(r_mid):
                    for m in range(r_top // r_mid):
                        picked = scr2_ref[i, gi, c4, pl.ds(m, tm // r_top, stride=r_top // r_mid), :]
                        att_refs[6 + i][0, m * r_mid + c4, :, cols] = picked.astype(BF16)

    n_streams = n_att + 4
    project(0, 0)
    for i in range(n_streams):
        if i + 1 < n_streams:
            project(i + 1, (i + 1) % 2)
        post(i, i % 2)


def _rope_tables(seq):
    pos = jnp.arange(seq, dtype=F32)[:, None]
    half = ROT_DIM // 2
    inv = ROPE_THETA ** (-(jnp.arange(0, ROT_DIM, 2, dtype=F32) / ROT_DIM))
    ang = pos * inv[None, :]
    cos, sin = jnp.cos(ang), jnp.sin(ang)
    one = jnp.ones((seq, ATT_HEAD_DIM - ROT_DIM), F32)
    zero = jnp.zeros((seq, ATT_HEAD_DIM - ROT_DIM), F32)
    zh = jnp.zeros((seq, half), F32)
    ca = jnp.concatenate([cos, cos, one], axis=1)
    sa1 = jnp.concatenate([-sin, zh, zero], axis=1)
    sa2 = jnp.concatenate([zh, sin, zero], axis=1)
    reps = LANES // ATT_HEAD_DIM
    ca, sa1, sa2 = (jnp.tile(t, (1, reps)) for t in (ca, sa1, sa2))
    inv_r = RET_ROPE_THETA ** (-(jnp.arange(0, RET_HEAD_DIM, 2, dtype=F32) / RET_HEAD_DIM))
    ang_r = pos * inv_r[None, :]
    cr = jnp.concatenate([jnp.cos(ang_r), jnp.cos(ang_r)], axis=1)
    sr = jnp.concatenate([-jnp.sin(ang_r), jnp.sin(ang_r)], axis=1)
    return ca, sa1, sa2, cr, sr


def _in_proj(x, g, w_in, tm=512):
    batch, seq, d = x.shape
    width = w_in.shape[1]
    tok = lambda w: pl.BlockSpec((1, tm, w), lambda b, t: (b, t, 0))
    tab = pl.BlockSpec((tm, LANES), lambda b, t: (t, 0))
    att_shapes, att_specs = [], []
    for _, r in DILATED_PATTERNS:
        assert tm % (r * 16) == 0
        att_shapes += [jax.ShapeDtypeStruct((batch, r, seq // r, ATT_WIDTH), BF16)] * 3
        att_specs += [pl.BlockSpec((1, r, tm // r, ATT_WIDTH), lambda b, t: (b, 0, t, 0))] * 3
    ret_shape = jax.ShapeDtypeStruct((batch, seq, RET_WIDTH), BF16)
    outs = pl.pallas_call(
        _in_proj_kernel,
        out_shape=tuple(att_shapes) + (ret_shape,) * 4,
        grid=(batch, seq // tm),
        in_specs=[tok(d), _resident((1, d)), _resident((d, width))] + [tab] * 5,
        out_specs=tuple(att_specs) + (tok(RET_WIDTH),) * 4,
        scratch_shapes=[pltpu.VMEM((tm, d), BF16),
                        pltpu.VMEM((2, tm, ATT_WIDTH), F32),
                        pltpu.VMEM((3, ATT_WIDTH // LANES, tm, LANES), F32),
                        pltpu.VMEM((3, ATT_WIDTH // LANES, DILATED_PATTERNS[1][1],
                                    tm // DILATED_PATTERNS[1][1], LANES), F32)],
        compiler_params=_params("parallel", "parallel"),
        name="in_proj",
    )(x, g, w_in, *_rope_tables(seq))
    return outs[:len(att_shapes)], outs[len(att_shapes):]


def _attn_kernel(*refs, nc, nq, seq_tiles, has_prev, is_last):
    refs = list(refs)
    q_ref, km_ref, kh_ref, vm_ref, vh_ref = refs[:5]
    del refs[:5]
    if has_prev:
        po_ref, ps_ref = refs[:2]
        del refs[:2]
    if is_last:
        g_ref = refs.pop(0)
    o_ref = refs.pop(0)
    if not is_last:
        s_ref = refs.pop(0)
    kbuf, vbuf, sc_buf, p_buf, bias = refs[:5]
    del refs[:5]
    if nc > 1:
        obuf, sbuf = refs

    half = ATT_HEAD_DIM
    groups = ATT_WIDTH // LANES
    chunk = 32
    for cls in range(nc):
        kbuf[cls, :BAND] = kh_ref[0, cls]
        kbuf[cls, BAND:] = km_ref[0, cls]
        vbuf[cls, :BAND] = vh_ref[0, cls]
        vbuf[cls, BAND:] = vm_ref[0, cls]

    key = lax.broadcasted_iota(jnp.int32, (2 * BAND, 2 * BAND), 0)
    qry = lax.broadcasted_iota(jnp.int32, (2 * BAND, 2 * BAND), 1) & (BAND - 1)
    if seq_tiles == 1:
        floor = BAND
    else:
        floor = jnp.where(pl.program_id(1) == 0, BAND, 0)
    upper = key <= qry + BAND
    bias[0] = jnp.where((key >= jnp.maximum(qry, floor)) & upper, 0.0, MASKED)
    bias[1] = jnp.where((key >= qry) & upper, 0.0, MASKED)

    lane = lax.broadcasted_iota(jnp.int32, (BAND, LANES), 1)
    lo = lane < half
    row8 = lax.broadcasted_iota(jnp.int32, (8, BAND), 0)

    def scores(cls, qstart, slot):
        for gi, c0 in enumerate(range(0, ATT_WIDTH, LANES)):
            cols = slice(c0, c0 + LANES)
            qp = q_ref[0, cls, pl.ds(qstart, BAND), cols]
            zero = jnp.zeros_like(qp)
            q2 = jnp.concatenate([jnp.where(lo, qp, zero), jnp.where(lo, zero, qp)], axis=0)
            sc_buf[slot, gi] = _dot_nt(kbuf[cls, pl.ds(qstart, 2 * BAND), cols], q2)

    def softmax(slot, which):
        out = []
        for gi in range(groups):
            top = jnp.full((8, 2 * BAND), MASKED, F32)
            for c in range(0, 2 * BAND, chunk):
                sc = sc_buf[slot, gi, c:c + chunk, :] + bias[which, c:c + chunk, :]
                sc_buf[slot, gi, c:c + chunk, :] = sc
                for r8 in range(0, chunk, 8):
                    top = jnp.maximum(top, sc[r8:r8 + 8, :])
            m = jnp.max(top, axis=0, keepdims=True)
            tot = jnp.zeros((8, 2 * BAND), F32)
            for c in range(0, 2 * BAND, chunk):
                e = jnp.exp(sc_buf[slot, gi, c:c + chunk, :] - m)
                p_buf[gi, c:c + chunk, :] = e.astype(BF16)
                for r8 in range(0, chunk, 8):
                    tot = tot + e[r8:r8 + 8, :]
            out.append((m, jnp.sum(tot, axis=0, keepdims=True)))
        return out

    def finish(cls, qstart, ml):
        qs = pl.ds(qstart, BAND)
        ks = pl.ds(qstart, 2 * BAND)
        if has_prev:
            prev_lse = ps_ref[0, cls, qs, :].T
        stats = jnp.zeros((8, BAND), F32)
        for gi, c0 in enumerate(range(0, ATT_WIDTH, LANES)):
            cols = slice(c0, c0 + LANES)
            m, l = ml[gi]
            ot = _dot_tn(vbuf[cls, ks, cols], p_buf[gi])
            inv = 1.0 / l
            lse = m + jnp.log(l)
            if has_prev:
                prev_o = po_ref[0, cls, qs, cols].astype(F32).T
            parts = []
            for hh in range(LANES // half):
                head = 2 * gi + hh
                qcols = slice(hh * BAND, (hh + 1) * BAND)
                rows = slice(hh * half, (hh + 1) * half)
                w_new = inv[:, qcols]
                lse_h = lse[:, qcols]
                if has_prev:
                    lp = prev_lse[head:head + 1, :]
                    top = jnp.maximum(lp, lse_h)
                    wp = jnp.exp(lp - top)
                    wn = jnp.exp(lse_h - top)
                    den = wp + wn
                    o = (wp / den) * prev_o[rows, :] + (wn * w_new / den) * ot[rows, qcols]
                    lse_h = top + jnp.log(den)
                else:
                    o = ot[rows, qcols] * w_new
                if is_last:
                    ms = jnp.mean(o * o, axis=0, keepdims=True)
                    o = o * lax.rsqrt(ms + RMS_EPS)
                parts.append(o)
                stats = jnp.where(row8 == head, lse_h, stats)
            o = jnp.concatenate(parts, axis=0).T
            if is_last:
                o = o * g_ref[:, cols]
            if nc > 1:
                obuf[gi, pl.ds(qstart * nc + cls, BAND, stride=nc), :] = o
            else:
                o_ref[0, 0, qs, cols] = o.astype(o_ref.dtype)
        if not is_last:
            st = jnp.concatenate([stats, jnp.zeros((BAND - 8, BAND), F32)], axis=0).T
            if nc > 1:
                sbuf[pl.ds(qstart * nc + cls, BAND, stride=nc), :] = st
            else:
                s_ref[0, 0, qs, :] = st

    blocks = [(cls, n * BAND) for cls in range(nc) for n in range(nq)]
    scores(*blocks[0], 0)
    for i, (cls, qstart) in enumerate(blocks):
        if i + 1 < len(blocks):
            scores(*blocks[i + 1], (i + 1) % 2)
        finish(cls, qstart, softmax(i % 2, 0 if qstart == 0 else 1))
    if nc > 1:
        for gi, c0 in enumerate(range(0, ATT_WIDTH, LANES)):
            o_ref[0, 0, :, c0:c0 + LANES] = obuf[gi].astype(o_ref.dtype)
        if not is_last:
            s_ref[0, 0] = sbuf[...]


def _attention(qkv, gn_g):
    w = ATT_WIDTH
    order = sorted(range(len(DILATED_PATTERNS)), key=lambda p: -DILATED_PATTERNS[p][1])
    prev = None
    for step, p in enumerate(order):
        window, r = DILATED_PATTERNS[p]
        q, k, v = qkv[p]
        batch, _, length, _ = q.shape
        is_last = step == len(order) - 1
        r_next = 1 if is_last else DILATED_PATTERNS[order[step + 1]][1]
        nc = r // r_next
        assert window // r == BAND and r == nc * r_next
        nq = 2 if nc > 1 else 8
        seq_tiles = length // (nq * BAND)
        assert length % (nq * BAND) == 0
        rows = nq * BAND

        if nc > 1:
            shape5 = lambda t: t.reshape(batch, nc, r_next, length, t.shape[-1])
            main = lambda wd: pl.BlockSpec((1, nc, None, rows, wd), lambda b, t: (b, 0, t // seq_tiles, t % seq_tiles, 0))
            halo = lambda wd: pl.BlockSpec(
                (1, nc, None, BAND, wd),
                lambda b, t: (b, 0, t // seq_tiles, jnp.maximum(t % seq_tiles * nq - 1, 0), 0))
            grid = (batch, r_next * seq_tiles)
            out_blk = lambda wd: pl.BlockSpec((1, 1, nc * rows, wd), lambda b, t: (b, t // seq_tiles, t % seq_tiles, 0))
        else:
            shape5 = lambda t: t
            main = lambda wd: pl.BlockSpec((1, 1, rows, wd), lambda b, t: (b, 0, t, 0))
            halo = lambda wd: pl.BlockSpec((1, 1, BAND, wd), lambda b, t: (b, 0, jnp.maximum(t * nq - 1, 0), 0))
            grid = (batch, seq_tiles)
            out_blk = lambda wd: pl.BlockSpec((1, 1, rows, wd), lambda b, t: (b, 0, t, 0))
        if nc > 1 and seq_tiles > 1:
            assert r_next == 1

        args = [shape5(q), shape5(k), shape5(k), shape5(v), shape5(v)]
        in_specs = [main(w), main(w), halo(w), main(w), halo(w)]
        if prev is not None:
            args += [shape5(prev[0]), shape5(prev[1])]
            in_specs += [main(w), main(LANES)]
        if is_last:
            args.append(gn_g)
            in_specs.append(_resident((1, w)))
        o_shape = jax.ShapeDtypeStruct((batch, r_next, length * nc, w), BF16)
        s_shape = jax.ShapeDtypeStruct((batch, r_next, length * nc, LANES), F32)
        groups = w // LANES
        scratch = [pltpu.VMEM((nc, rows + BAND, w), BF16)] * 2 + [
            pltpu.VMEM((2, groups, 2 * BAND, 2 * BAND), F32),
            pltpu.VMEM((groups, 2 * BAND, 2 * BAND), BF16),
            pltpu.VMEM((2, 2 * BAND, 2 * BAND), F32)]
        if nc > 1:
            scratch += [pltpu.VMEM((w // LANES, nc * rows, LANES), F32), pltpu.VMEM((nc * rows, LANES), F32)]
        res = pl.pallas_call(
            functools.partial(_attn_kernel, nc=nc, nq=nq, seq_tiles=seq_tiles,
                              has_prev=prev is not None, is_last=is_last),
            out_shape=o_shape if is_last else (o_shape, s_shape),
            grid=grid,
            in_specs=in_specs,
            out_specs=out_blk(w) if is_last else (out_blk(w), out_blk(LANES)),
            scratch_shapes=scratch,
            compiler_params=_params("parallel", "arbitrary"),
            name=f"attn_r{r}",
        )(*args)
        if is_last:
            return res.reshape(batch, length * nc, w)
        prev = res


def _retention_kernel(q_ref, k_ref, v_ref, gate_ref, dmask_ref, xi_ref, zeta_ref, cdec_ref, g_ref,
                      o_ref, state_ref, raw_ref, upd_ref, pre_ref):
    @pl.when(pl.program_id(1) == 0)
    def _():
        state_ref[...] = jnp.zeros_like(state_ref)

    chunk = RET_CHUNK

    def where(t, h):
        return slice(t, t + chunk), slice(h * RET_HEAD_DIM, (h + 1) * RET_HEAD_DIM)

    def scores(t, h, slot):
        rows, c = where(t, h)
        k = k_ref[0, rows, c]
        raw_ref[slot] = _dot_nt(q_ref[0, rows, c], k)
        kz = (k.astype(F32) * zeta_ref[h]).astype(BF16)
        upd_ref[slot] = _dot_tn(kz, v_ref[0, rows, c])

    def mix(t, h, slot):
        rows, c = where(t, h)
        inner = (raw_ref[slot] * dmask_ref[h]).astype(BF16)
        state = state_ref[h]
        pre_ref[slot] = (_dot(inner, v_ref[0, rows, c])
                         + _dot(q_ref[0, rows, c], state.astype(BF16)) * xi_ref[h])
        state_ref[h] = state * cdec_ref[h] + upd_ref[slot]

    def finish(t, h, slot):
        rows, c = where(t, h)
        o = pre_ref[slot]
        mu = jnp.mean(o, axis=-1, keepdims=True)
        var = jnp.mean(jnp.square(o - mu), axis=-1, keepdims=True)
        y = (o - mu) * lax.rsqrt(var + GN_EPS) * g_ref[:, c]
        gate = gate_ref[0, rows, c].astype(F32)
        o_ref[0, rows, c] = (gate * (1.0 / (1.0 + jnp.exp(-gate))) * y).astype(o_ref.dtype)

    units = [(t, h) for t in range(0, q_ref.shape[1], chunk) for h in range(RET_HEADS)]
    scores(*units[0], 0)
    for i, unit in enumerate(units):
        if i + 1 < len(units):
            scores(*units[i + 1], (i + 1) % 2)
        mix(*unit, i % 2)
        if i > 0:
            finish(*units[i - 1], (i - 1) % 2)
    finish(*units[-1], (len(units) - 1) % 2)


def _retention_tables():
    c = RET_CHUNK
    log_g = jnp.log(1.0 - 2.0 ** (-5.0 - jnp.arange(RET_HEADS, dtype=F32)))
    n = jnp.arange(c, dtype=F32)
    rel = n[:, None] - n[None, :]
    dmask = jnp.where(rel >= 0, jnp.exp(log_g[:, None, None] * jnp.maximum(rel, 0.0)), 0.0)
    xi = jnp.exp(log_g[:, None] * (n + 1.0))
    zeta = jnp.exp(log_g[:, None] * (c - 1.0 - n))
    cdec = jnp.exp(log_g * c)
    wide = lambda t: jnp.broadcast_to(t[:, :, None], (RET_HEADS, c, RET_HEAD_DIM))
    cdec = jnp.broadcast_to(cdec[:, None, None], (RET_HEADS, RET_HEAD_DIM, RET_HEAD_DIM))
    return dmask, wide(xi), wide(zeta), cdec


def _retention(rq, rk, rv, rg, gn_g, chunks_per_step=4):
    batch, seq, width = rq.shape
    c = RET_CHUNK
    rows = c * chunks_per_step
    blk = pl.BlockSpec((1, rows, width), lambda b, n: (b, n, 0))
    tab = _resident((RET_HEADS, c, RET_HEAD_DIM))
    return pl.pallas_call(
        _retention_kernel,
        out_shape=jax.ShapeDtypeStruct((batch, seq, width), BF16),
        grid=(batch, seq // rows),
        in_specs=[blk] * 4 + [tab] * 4 + [_resident((1, width))],
        out_specs=blk,
        scratch_shapes=[pltpu.VMEM((RET_HEADS, RET_HEAD_DIM, RET_HEAD_DIM), F32),
                        pltpu.VMEM((2, c, c), F32),
                        pltpu.VMEM((2, RET_HEAD_DIM, RET_HEAD_DIM), F32),
                        pltpu.VMEM((2, c, RET_HEAD_DIM), F32)],
        compiler_params=_params("parallel", "arbitrary"),
        name="retention",
    )(rq, rk, rv, rg, *_retention_tables(), gn_g)


def _mem_kv_kernel(mem_ref, g_ref, w_ref, k_ref, v_ref):
    d = mem_ref.shape[-1]
    kv = _dot(_rms(mem_ref[0], g_ref[...]).astype(BF16), w_ref[...])
    k_ref[0] = kv[:, :d].astype(BF16)
    v_ref[0] = kv[:, d:].astype(BF16)


def _mem_kv(mem, g, w_kv):
    batch, n_mem, d = mem.shape
    blk = pl.BlockSpec((1, n_mem, d), lambda b: (b, 0, 0))
    out = jax.ShapeDtypeStruct((batch, n_mem, d), BF16)
    return pl.pallas_call(
        _mem_kv_kernel,
        out_shape=(out, out),
        grid=(batch,),
        in_specs=[blk, _resident((1, d)), _resident((d, 2 * d))],
        out_specs=(blk, blk),
        compiler_params=_params("parallel"),
        name="mem_kv",
    )(mem, g, w_kv)


def _mix_out_kernel(a_ref, r_ref, x_ref, wout_ref, gpost_ref, gpre_ref, wq_ref, mk_ref, mv_ref,
                    wo_ref, gpost2_ref, o_ref, x1_ref, h_ref, q_ref, att_ref, *, sub):
    half = a_ref.shape[-1]
    d = x_ref.shape[-1]
    dh = d // XATT_HEADS

    def out_proj(rows):
        y = _dot(a_ref[0, rows, :], wout_ref[:half, :]) + _dot(r_ref[0, rows, :], wout_ref[half:, :])
        x1 = x_ref[0, rows, :] + _rms(y, gpost_ref[...])
        x1_ref[rows, :] = x1
        h_ref[rows, :] = _rms(x1, gpre_ref[...]).astype(BF16)

    def q_proj(rows):
        q_ref[rows, :] = (_dot(h_ref[rows, :], wq_ref[...]) * (dh ** -0.5)).astype(BF16)

    def cross_attn(rows):
        for hd in range(XATT_HEADS):
            c = slice(hd * dh, (hd + 1) * dh)
            s = _dot_nt(q_ref[rows, c], mk_ref[0, :, c])
            p = jnp.exp(s - jnp.max(s, axis=-1, keepdims=True))
            l = jnp.sum(p, axis=-1, keepdims=True)
            att_ref[rows, c] = (_dot(p.astype(BF16), mv_ref[0, :, c]) / l).astype(BF16)

    def o_proj(rows):
        y2 = _dot(att_ref[rows, :], wo_ref[...])
        o_ref[0, rows, :] = x1_ref[rows, :] + _rms(y2, gpost2_ref[...])

    tiles = [slice(r0, r0 + sub) for r0 in range(0, x_ref.shape[1], sub)]
    for stage in (out_proj, q_proj, cross_attn, o_proj):
        for rows in tiles:
            stage(rows)


def _mix_out(a, r, x, w_out, g_post, g_pre, w_q, mem_k, mem_v, w_o, g_post2, tm=1024, sub=256):
    batch, seq, d = x.shape
    half = a.shape[-1]
    n_mem = mem_k.shape[1]
    tok = lambda w: pl.BlockSpec((1, tm, w), lambda b, t: (b, t, 0))
    memblk = pl.BlockSpec((1, n_mem, d), lambda b, t: (b, 0, 0))
    vec = _resident((1, d))
    mat = _resident((d, d))
    return pl.pallas_call(
        functools.partial(_mix_out_kernel, sub=sub),
        out_shape=jax.ShapeDtypeStruct((batch, seq, d), F32),
        grid=(batch, seq // tm),
        in_specs=[tok(half), tok(half), tok(d), mat, vec, vec, mat, memblk, memblk, mat, vec],
        out_specs=tok(d),
        scratch_shapes=[pltpu.VMEM((tm, d), F32),
                        pltpu.VMEM((tm, d), BF16),
                        pltpu.VMEM((tm, d), BF16),
                        pltpu.VMEM((tm, d), BF16)],
        compiler_params=_params("parallel", "parallel"),
        name="mix_out",
    )(a, r, x, w_out, g_post, g_pre, w_q, mem_k, mem_v, w_o, g_post2)


def _ffn_kernel(x_ref, gpre_ref, wgu_ref, wdown_ref, gpost_ref, o_ref, *, chunks):
    x = x_ref[...]
    h = _rms(x, gpre_ref[...]).astype(BF16)
    hidden = wdown_ref.shape[0]
    y = None
    for lo, hi in chunks:
        g = _dot(h, wgu_ref[:, lo:hi])
        u = _dot(h, wgu_ref[:, hidden + lo:hidden + hi])
        act = (g * (1.0 / (1.0 + jnp.exp(-g))) * u).astype(BF16)
        part = _dot(act, wdown_ref[lo:hi, :])
        y = part if y is None else y + part
    o_ref[...] = x + _rms(y, gpost_ref[...])


def _ffn(x2d, g_pre, w_gu, w_down, g_post, tm=512):
    tokens, d = x2d.shape
    hidden = w_down.shape[0]
    mxu = 256
    cut = (hidden // 2) // mxu * mxu
    chunks = ((0, cut), (cut, hidden))
    row = pl.BlockSpec((tm, d), lambda t: (t, 0))
    vec = _resident((1, d))
    return pl.pallas_call(
        functools.partial(_ffn_kernel, chunks=chunks),
        out_shape=jax.ShapeDtypeStruct((tokens, d), F32),
        grid=(tokens // tm,),
        in_specs=[row, vec, _resident((d, 2 * hidden)), _resident((hidden, d)), vec],
        out_specs=row,
        compiler_params=_params("parallel"),
        name="ffn",
    )(x2d, g_pre, w_gu, w_down, g_post)


def kernel(x, mem, pre_mix_g, post_mix_g, w_in, attn_gn_g, ret_gn_g, w_out,
           pre_mem_g, post_mem_g, mem_norm_g, w_q_mem, w_kv_mem, w_o_mem,
           pre_ffn_g, post_ffn_g, w_gate_up, w_down):
    batch, seq, d = x.shape
    depth = w_in.shape[0]
    for l in range(depth):
        bf = lambda w: w[l].astype(BF16)
        vec = lambda g: g[l][None, :]
        att, (rq, rk, rv, rg) = _in_proj(x, vec(pre_mix_g), bf(w_in))
        qkv = [att[3 * p:3 * p + 3] for p in range(len(DILATED_PATTERNS))]
        a = _attention(qkv, vec(attn_gn_g))
        r = _retention(rq, rk, rv, rg, vec(ret_gn_g))
        mem_k, mem_v = _mem_kv(mem, vec(mem_norm_g), bf(w_kv_mem))
        x = _mix_out(a, r, x, bf(w_out), vec(post_mix_g), vec(pre_mem_g), bf(w_q_mem),
                     mem_k, mem_v, bf(w_o_mem), vec(post_mem_g))
        x = _ffn(x.reshape(batch * seq, d), vec(pre_ffn_g), bf(w_gate_up), bf(w_down),
                 vec(post_ffn_g)).reshape(batch, seq, d)
    return x
```

```python
import functools

import jax
import jax.numpy as jnp
from jax import lax
from jax.experimental import pallas as pl
from jax.experimental.pallas import tpu as pltpu

F32 = jnp.float32
BF16 = jnp.bfloat16

RMS_EPS = 1e-6
GN_EPS = 1e-5
ATT_HEADS = 8
ATT_HEAD_DIM = 64
ATT_WIDTH = ATT_HEADS * ATT_HEAD_DIM
ROT_DIM = ATT_HEAD_DIM // 4
ROPE_THETA = 500000.0
DILATED_PATTERNS = ((128, 1), (512, 4), (2048, 16))
RET_HEADS = 4
RET_HEAD_DIM = 128
RET_WIDTH = RET_HEADS * RET_HEAD_DIM
RET_CHUNK = 128
RET_ROPE_THETA = 10000.0
XATT_HEADS = 4

LANES = 128
BAND = 128
MASKED = -1e30
LOG2_E = 1.4426950408889634
VMEM_LIMIT = 56 * 1024 * 1024


def _resident(shape):
    zeros = (0,) * len(shape)
    return pl.BlockSpec(shape, lambda *_: zeros, pipeline_mode=pl.Buffered(1))


def _params(*sem):
    return pltpu.CompilerParams(dimension_semantics=sem, vmem_limit_bytes=VMEM_LIMIT)


def _rms(x, g):
    return x * lax.rsqrt(jnp.mean(x * x, axis=-1, keepdims=True) + RMS_EPS) * g


def _dot(a, b):
    return jnp.dot(a, b, preferred_element_type=F32)


def _dot_nt(a, b):
    return lax.dot_general(a, b, (((1,), (1,)), ((), ())), preferred_element_type=F32)


def _dot_tn(a, b):
    return lax.dot_general(a, b, (((0,), (0,)), ((), ())), preferred_element_type=F32)


def _in_proj_kernel(x_ref, g_ref, w_ref, ca_ref, sa1_ref, sa2_ref, cr_ref, sr_ref, *refs):
    att_refs = refs[:3 * len(DILATED_PATTERNS)]
    ret_refs = refs[len(att_refs):len(att_refs) + 4]
    h_ref, acc_ref, scr_ref, scr2_ref = refs[len(att_refs) + 4:]
    tm = x_ref.shape[1]
    width = ATT_WIDTH
    n_att = 3
    (_, r_one), (_, r_mid), (_, r_top) = DILATED_PATTERNS
    assert r_one == 1 and r_top % r_mid == 0
    rows_per_step = 64
    h_ref[...] = _rms(x_ref[0], g_ref[...]).astype(BF16)

    def project(i, slot):
        acc_ref[slot] = _dot(h_ref[...], w_ref[:, i * width:(i + 1) * width])

    def rope_attn(t, rows):
        return (t * ca_ref[rows, :]
                + pltpu.roll(t, LANES - ROT_DIM // 2, 1) * sa1_ref[rows, :]
                + pltpu.roll(t, ROT_DIM // 2, 1) * sa2_ref[rows, :])

    def rope_ret(t, rows):
        return t * cr_ref[rows, :] + pltpu.roll(t, RET_HEAD_DIM // 2, 1) * sr_ref[rows, :]

    def post(i, slot):
        for gi, c0 in enumerate(range(0, width, LANES)):
            cols = slice(c0, c0 + LANES)
            for r0 in range(0, tm, rows_per_step):
                rows = slice(r0, r0 + rows_per_step)
                t = acc_ref[slot, rows, cols]
                if i == 0:
                    t = rope_attn(t, rows) * (ATT_HEAD_DIM ** -0.5 * LOG2_E)
                elif i == 1:
                    t = rope_attn(t, rows)
                elif i == n_att:
                    t = rope_ret(t, rows)
                elif i == n_att + 1:
                    t = rope_ret(t, rows) * (RET_HEAD_DIM ** -0.5)
                if i >= n_att:
                    ret_refs[i - n_att][0, rows, cols] = t.astype(BF16)
                else:
                    scr_ref[i, gi, rows, :] = t
                    att_refs[i][0, 0, rows, cols] = t.astype(BF16)
            if i < n_att:
                for c4 in range(r_mid):
                    picked = scr_ref[i, gi, pl.ds(c4, tm // r_mid, stride=r_mid), :]
                    att_refs[3 + i][0, c4, :, cols] = picked.astype(BF16)
                    scr2_ref[i, gi, c4] = picked
                for c4 in range(r_mid):
                    for m in range(r_top // r_mid):
                        picked = scr2_ref[i, gi, c4, pl.ds(m, tm // r_top, stride=r_top // r_mid), :]
                        att_refs[6 + i][0, m * r_mid + c4, :, cols] = picked.astype(BF16)

    n_streams = n_att + 4
    project(0, 0)
    for i in range(n_streams):
        if i + 1 < n_streams:
            project(i + 1, (i + 1) % 2)
        post(i, i % 2)


def _rope_tables(seq):
    pos = jnp.arange(seq, dtype=F32)[:, None]
    half = ROT_DIM // 2
    inv = ROPE_THETA ** (-(jnp.arange(0, ROT_DIM, 2, dtype=F32) / ROT_DIM))
    ang = pos * inv[None, :]
    cos, sin = jnp.cos(ang), jnp.sin(ang)
    one = jnp.ones((seq, ATT_HEAD_DIM - ROT_DIM), F32)
    zero = jnp.zeros((seq, ATT_HEAD_DIM - ROT_DIM), F32)
    zh = jnp.zeros((seq, half), F32)
    ca = jnp.concatenate([cos, cos, one], axis=1)
    sa1 = jnp.concatenate([-sin, zh, zero], axis=1)
    sa2 = jnp.concatenate([zh, sin, zero], axis=1)
    reps = LANES // ATT_HEAD_DIM
    ca, sa1, sa2 = (jnp.tile(t, (1, reps)) for t in (ca, sa1, sa2))
    inv_r = RET_ROPE_THETA ** (-(jnp.arange(0, RET_HEAD_DIM, 2, dtype=F32) / RET_HEAD_DIM))
    ang_r = pos * inv_r[None, :]
    cr = jnp.concatenate([jnp.cos(ang_r), jnp.cos(ang_r)], axis=1)
    sr = jnp.concatenate([-jnp.sin(ang_r), jnp.sin(ang_r)], axis=1)
    return ca, sa1, sa2, cr, sr


def _in_proj(x, g, w_in, tm=512):
    batch, seq, d = x.shape
    width = w_in.shape[1]
    tok = lambda w: pl.BlockSpec((1, tm, w), lambda b, t: (b, t, 0))
    tab = pl.BlockSpec((tm, LANES), lambda b, t: (t, 0))
    att_shapes, att_specs = [], []
    for _, r in DILATED_PATTERNS:
        assert tm % (r * 16) == 0
        att_shapes += [jax.ShapeDtypeStruct((batch, r, seq // r, ATT_WIDTH), BF16)] * 3
        att_specs += [pl.BlockSpec((1, r, tm // r, ATT_WIDTH), lambda b, t: (b, 0, t, 0))] * 3
    ret_shape = jax.ShapeDtypeStruct((batch, seq, RET_WIDTH), BF16)
    outs = pl.pallas_call(
        _in_proj_kernel,
        out_shape=tuple(att_shapes) + (ret_shape,) * 4,
        grid=(batch, seq // tm),
        in_specs=[tok(d), _resident((1, d)), _resident((d, width))] + [tab] * 5,
        out_specs=tuple(att_specs) + (tok(RET_WIDTH),) * 4,
        scratch_shapes=[pltpu.VMEM((tm, d), BF16),
                        pltpu.VMEM((2, tm, ATT_WIDTH), F32),
                        pltpu.VMEM((3, ATT_WIDTH // LANES, tm, LANES), F32),
                        pltpu.VMEM((3, ATT_WIDTH // LANES, DILATED_PATTERNS[1][1],
                                    tm // DILATED_PATTERNS[1][1], LANES), F32)],
        compiler_params=_params("parallel", "parallel"),
        name="in_proj",
    )(x, g, w_in, *_rope_tables(seq))
    return outs[:len(att_shapes)], outs[len(att_shapes):]


def _attn_kernel(*refs, nc, nq, seq_tiles, has_prev, is_last):
    refs = list(refs)
    q_ref, km_ref, kh_ref, vm_ref, vh_ref = refs[:5]
    del refs[:5]
    if has_prev:
        po_ref, ps_ref = refs[:2]
        del refs[:2]
    if is_last:
        g_ref = refs.pop(0)
    o_ref = refs.pop(0)
    if not is_last:
        s_ref = refs.pop(0)
    kbuf, vbuf, sc_buf, p_buf, bias = refs[:5]
    del refs[:5]
    if nc > 1:
        obuf, sbuf = refs

    half = ATT_HEAD_DIM
    groups = ATT_WIDTH // LANES
    chunk = 32
    for cls in range(nc):
        kbuf[cls, :BAND] = kh_ref[0, cls]
        kbuf[cls, BAND:] = km_ref[0, cls]
        vbuf[cls, :BAND] = vh_ref[0, cls]
        vbuf[cls, BAND:] = vm_ref[0, cls]

    key = lax.broadcasted_iota(jnp.int32, (2 * BAND, 2 * BAND), 0)
    qry = lax.broadcasted_iota(jnp.int32, (2 * BAND, 2 * BAND), 1) & (BAND - 1)
    if seq_tiles == 1:
        floor = BAND
    else:
        floor = jnp.where(pl.program_id(1) == 0, BAND, 0)
    upper = key <= qry + BAND
    bias[0] = jnp.where((key >= jnp.maximum(qry, floor)) & upper, 0.0, MASKED)
    bias[1] = jnp.where((key >= qry) & upper, 0.0, MASKED)

    lane = lax.broadcasted_iota(jnp.int32, (BAND, LANES), 1)
    lo = lane < half
    row8 = lax.broadcasted_iota(jnp.int32, (8, BAND), 0)

    def scores(cls, qstart, slot, which):
        tops = []
        for gi, c0 in enumerate(range(0, ATT_WIDTH, LANES)):
            cols = slice(c0, c0 + LANES)
            qp = q_ref[0, cls, pl.ds(qstart, BAND), cols]
            zero = jnp.zeros_like(qp)
            q2 = jnp.concatenate([jnp.where(lo, qp, zero), jnp.where(lo, zero, qp)], axis=0)
            s = _dot_nt(kbuf[cls, pl.ds(qstart, 2 * BAND), cols], q2) + bias[which]
            sc_buf[slot, gi] = s
            top = s[0:8, :]
            for r8 in range(8, 2 * BAND, 8):
                top = jnp.maximum(top, s[r8:r8 + 8, :])
            tops.append(jnp.max(top, axis=0, keepdims=True))
        return tops

    def softmax(slot, tops):
        for gi in range(groups):
            for c in range(0, 2 * BAND, chunk):
                e = jnp.exp2(sc_buf[slot, gi, c:c + chunk, :] - tops[gi])
                p_buf[gi, c:c + chunk, :] = e.astype(BF16)

    ones_rows = jnp.ones((16, 2 * BAND), BF16)

    def finish(cls, qstart, tops):
        qs = pl.ds(qstart, BAND)
        ks = pl.ds(qstart, 2 * BAND)
        if has_prev:
            prev_lse = ps_ref[0, cls, qs, :].T
        stats = jnp.zeros((8, BAND), F32)
        for gi, c0 in enumerate(range(0, ATT_WIDTH, LANES)):
            cols = slice(c0, c0 + LANES)
            m = tops[gi]
            lhs = jnp.concatenate([vbuf[cls, ks, cols].T, ones_rows], axis=0)
            ot = _dot(lhs, p_buf[gi])
            l = ot[LANES:LANES + 1, :]
            inv = 1.0 / l
            lse = m + jnp.log2(l)
            if has_prev:
                prev_o = po_ref[0, cls, qs, cols].astype(F32).T
            parts = []
            for hh in range(LANES // half):
                head = 2 * gi + hh
                qcols = slice(hh * BAND, (hh + 1) * BAND)
                rows = slice(hh * half, (hh + 1) * half)
                w_new = inv[:, qcols]
                lse_h = lse[:, qcols]
                if has_prev:
                    lp = prev_lse[head:head + 1, :]
                    top = jnp.maximum(lp, lse_h)
                    wp = jnp.exp2(lp - top)
                    wn = jnp.exp2(lse_h - top)
                    den = wp + wn
                    o = (wp / den) * prev_o[rows, :] + (wn * w_new / den) * ot[rows, qcols]
                    lse_h = top + jnp.log2(den)
                else:
                    o = ot[rows, qcols] * w_new
                if is_last:
                    ms = jnp.mean(o * o, axis=0, keepdims=True)
                    o = o * lax.rsqrt(ms + RMS_EPS)
                parts.append(o)
                stats = jnp.where(row8 == head, lse_h, stats)
            o = jnp.concatenate(parts, axis=0).T
            if is_last:
                o = o * g_ref[:, cols]
            if nc > 1:
                obuf[gi, pl.ds(qstart * nc + cls, BAND, stride=nc), :] = o
            else:
                o_ref[0, 0, qs, cols] = o.astype(o_ref.dtype)
        if not is_last:
            st = jnp.concatenate([stats, jnp.zeros((BAND - 8, BAND), F32)], axis=0).T
            if nc > 1:
                sbuf[pl.ds(qstart * nc + cls, BAND, stride=nc), :] = st
            else:
                s_ref[0, 0, qs, :] = st

    blocks = [(cls, n * BAND) for cls in range(nc) for n in range(nq)]
    which = lambda blk: 0 if blk[1] == 0 else 1
    tops = scores(*blocks[0], 0, which(blocks[0]))
    for i, (cls, qstart) in enumerate(blocks):
        if i + 1 < len(blocks):
            tops_next = scores(*blocks[i + 1], (i + 1) % 2, which(blocks[i + 1]))
        softmax(i % 2, tops)
        finish(cls, qstart, tops)
        tops = tops_next
    if nc > 1:
        for gi, c0 in enumerate(range(0, ATT_WIDTH, LANES)):
            o_ref[0, 0, :, c0:c0 + LANES] = obuf[gi].astype(o_ref.dtype)
        if not is_last:
            s_ref[0, 0] = sbuf[...]


def _attention(qkv, gn_g):
    w = ATT_WIDTH
    order = sorted(range(len(DILATED_PATTERNS)), key=lambda p: -DILATED_PATTERNS[p][1])
    prev = None
    for step, p in enumerate(order):
        window, r = DILATED_PATTERNS[p]
        q, k, v = qkv[p]
        batch, _, length, _ = q.shape
        is_last = step == len(order) - 1
        r_next = 1 if is_last else DILATED_PATTERNS[order[step + 1]][1]
        nc = r // r_next
        assert window // r == BAND and r == nc * r_next
        nq = 2 if nc > 1 else 8
        seq_tiles = length // (nq * BAND)
        assert length % (nq * BAND) == 0
        rows = nq * BAND

        if nc > 1:
            shape5 = lambda t: t.reshape(batch, nc, r_next, length, t.shape[-1])
            main = lambda wd: pl.BlockSpec((1, nc, None, rows, wd), lambda b, t: (b, 0, t // seq_tiles, t % seq_tiles, 0))
            halo = lambda wd: pl.BlockSpec(
                (1, nc, None, BAND, wd),
                lambda b, t: (b, 0, t // seq_tiles, jnp.maximum(t % seq_tiles * nq - 1, 0), 0))
            grid = (batch, r_next * seq_tiles)
            out_blk = lambda wd: pl.BlockSpec((1, 1, nc * rows, wd), lambda b, t: (b, t // seq_tiles, t % seq_tiles, 0))
        else:
            shape5 = lambda t: t
            main = lambda wd: pl.BlockSpec((1, 1, rows, wd), lambda b, t: (b, 0, t, 0))
            halo = lambda wd: pl.BlockSpec((1, 1, BAND, wd), lambda b, t: (b, 0, jnp.maximum(t * nq - 1, 0), 0))
            grid = (batch, seq_tiles)
            out_blk = lambda wd: pl.BlockSpec((1, 1, rows, wd), lambda b, t: (b, 0, t, 0))
        if nc > 1 and seq_tiles > 1:
            assert r_next == 1

        args = [shape5(q), shape5(k), shape5(k), shape5(v), shape5(v)]
        in_specs = [main(w), main(w), halo(w), main(w), halo(w)]
        if prev is not None:
            args += [shape5(prev[0]), shape5(prev[1])]
            in_specs += [main(w), main(LANES)]
        if is_last:
            args.append(gn_g)
            in_specs.append(_resident((1, w)))
        o_shape = jax.ShapeDtypeStruct((batch, r_next, length * nc, w), BF16)
        s_shape = jax.ShapeDtypeStruct((batch, r_next, length * nc, LANES), F32)
        groups = w // LANES
        scratch = [pltpu.VMEM((nc, rows + BAND, w), BF16)] * 2 + [
            pltpu.VMEM((2, groups, 2 * BAND, 2 * BAND), F32),
            pltpu.VMEM((groups, 2 * BAND, 2 * BAND), BF16),
            pltpu.VMEM((2, 2 * BAND, 2 * BAND), F32)]
        if nc > 1:
            scratch += [pltpu.VMEM((w // LANES, nc * rows, LANES), F32), pltpu.VMEM((nc * rows, LANES), F32)]
        res = pl.pallas_call(
            functools.partial(_attn_kernel, nc=nc, nq=nq, seq_tiles=seq_tiles,
                              has_prev=prev is not None, is_last=is_last),
            out_shape=o_shape if is_last else (o_shape, s_shape),
            grid=grid,
            in_specs=in_specs,
            out_specs=out_blk(w) if is_last else (out_blk(w), out_blk(LANES)),
            scratch_shapes=scratch,
            compiler_params=_params("parallel", "arbitrary"),
            name=f"attn_r{r}",
        )(*args)
        if is_last:
            return res.reshape(batch, length * nc, w)
        prev = res


def _retention_kernel(q_ref, k_ref, v_ref, gate_ref, dmask_ref, xi_ref, zeta_ref, cdec_ref, g_ref,
                      o_ref, state_ref, raw_ref, upd_ref, pre_ref):
    @pl.when(pl.program_id(1) == 0)
    def _():
        state_ref[...] = jnp.zeros_like(state_ref)

    chunk = RET_CHUNK

    def where(t, h):
        return slice(t, t + chunk), slice(h * RET_HEAD_DIM, (h + 1) * RET_HEAD_DIM)

    def scores(t, h, slot):
        rows, c = where(t, h)
        k = k_ref[0, rows, c]
        raw_ref[slot] = _dot_nt(q_ref[0, rows, c], k)
        kz = (k.astype(F32) * zeta_ref[h]).astype(BF16)
        upd_ref[slot] = _dot_tn(kz, v_ref[0, rows, c])

    def mix(t, h, slot):
        rows, c = where(t, h)
        inner = (raw_ref[slot] * dmask_ref[h]).astype(BF16)
        state = state_ref[h]
        pre_ref[slot] = (_dot(inner, v_ref[0, rows, c])
                         + _dot(q_ref[0, rows, c], state.astype(BF16)) * xi_ref[h])
        state_ref[h] = state * cdec_ref[h] + upd_ref[slot]

    def finish(t, h, slot):
        rows, c = where(t, h)
        o = pre_ref[slot]
        mu = jnp.mean(o, axis=-1, keepdims=True)
        var = jnp.mean(jnp.square(o - mu), axis=-1, keepdims=True)
        y = (o - mu) * lax.rsqrt(var + GN_EPS) * g_ref[:, c]
        gate = gate_ref[0, rows, c].astype(F32)
        o_ref[0, rows, c] = (gate * (1.0 / (1.0 + jnp.exp(-gate))) * y).astype(o_ref.dtype)

    units = [(t, h) for t in range(0, q_ref.shape[1], chunk) for h in range(RET_HEADS)]
    scores(*units[0], 0)
    for i, unit in enumerate(units):
        if i + 1 < len(units):
            scores(*units[i + 1], (i + 1) % 2)
        mix(*unit, i % 2)
        if i > 0:
            finish(*units[i - 1], (i - 1) % 2)
    finish(*units[-1], (len(units) - 1) % 2)


def _retention_tables():
    c = RET_CHUNK
    log_g = jnp.log(1.0 - 2.0 ** (-5.0 - jnp.arange(RET_HEADS, dtype=F32)))
    n = jnp.arange(c, dtype=F32)
    rel = n[:, None] - n[None, :]
    dmask = jnp.where(rel >= 0, jnp.exp(log_g[:, None, None] * jnp.maximum(rel, 0.0)), 0.0)
    xi = jnp.exp(log_g[:, None] * (n + 1.0))
    zeta = jnp.exp(log_g[:, None] * (c - 1.0 - n))
    cdec = jnp.exp(log_g * c)
    wide = lambda t: jnp.broadcast_to(t[:, :, None], (RET_HEADS, c, RET_HEAD_DIM))
    cdec = jnp.broadcast_to(cdec[:, None, None], (RET_HEADS, RET_HEAD_DIM, RET_HEAD_DIM))
    return dmask, wide(xi), wide(zeta), cdec


def _retention(rq, rk, rv, rg, gn_g, chunks_per_step=4):
    batch, seq, width = rq.shape
    c = RET_CHUNK
    rows = c * chunks_per_step
    blk = pl.BlockSpec((1, rows, width), lambda b, n: (b, n, 0))
    tab = _resident((RET_HEADS, c, RET_HEAD_DIM))
    return pl.pallas_call(
        _retention_kernel,
        out_shape=jax.ShapeDtypeStruct((batch, seq, width), BF16),
        grid=(batch, seq // rows),
        in_specs=[blk] * 4 + [tab] * 4 + [_resident((1, width))],
        out_specs=blk,
        scratch_shapes=[pltpu.VMEM((RET_HEADS, RET_HEAD_DIM, RET_HEAD_DIM), F32),
                        pltpu.VMEM((2, c, c), F32),
                        pltpu.VMEM((2, RET_HEAD_DIM, RET_HEAD_DIM), F32),
                        pltpu.VMEM((2, c, RET_HEAD_DIM), F32)],
        compiler_params=_params("parallel", "arbitrary"),
        name="retention",
    )(rq, rk, rv, rg, *_retention_tables(), gn_g)


def _mem_kv_kernel(mem_ref, g_ref, w_ref, k_ref, v_ref):
    d = mem_ref.shape[-1]
    kv = _dot(_rms(mem_ref[0], g_ref[...]).astype(BF16), w_ref[...])
    k_ref[0] = kv[:, :d].astype(BF16)
    v_ref[0] = kv[:, d:].astype(BF16)


def _mem_kv(mem, g, w_kv):
    batch, n_mem, d = mem.shape
    blk = pl.BlockSpec((1, n_mem, d), lambda b: (b, 0, 0))
    out = jax.ShapeDtypeStruct((batch, n_mem, d), BF16)
    return pl.pallas_call(
        _mem_kv_kernel,
        out_shape=(out, out),
        grid=(batch,),
        in_specs=[blk, _resident((1, d)), _resident((d, 2 * d))],
        out_specs=(blk, blk),
        compiler_params=_params("parallel"),
        name="mem_kv",
    )(mem, g, w_kv)


def _mix_out_kernel(a_ref, r_ref, x_ref, wout_ref, gpost_ref, gpre_ref, wq_ref, mk_ref, mv_ref,
                    wo_ref, gpost2_ref, o_ref, x1_ref, h_ref, q_ref, att_ref, *, sub):
    half = a_ref.shape[-1]
    d = x_ref.shape[-1]
    dh = d // XATT_HEADS

    def out_proj(rows):
        y = _dot(a_ref[0, rows, :], wout_ref[:half, :]) + _dot(r_ref[0, rows, :], wout_ref[half:, :])
        x1 = x_ref[0, rows, :] + _rms(y, gpost_ref[...])
        x1_ref[rows, :] = x1
        h_ref[rows, :] = _rms(x1, gpre_ref[...]).astype(BF16)

    def q_proj(rows):
        q_ref[rows, :] = (_dot(h_ref[rows, :], wq_ref[...]) * (dh ** -0.5)).astype(BF16)

    def cross_attn(rows):
        for hd in range(XATT_HEADS):
            c = slice(hd * dh, (hd + 1) * dh)
            s = _dot_nt(q_ref[rows, c], mk_ref[0, :, c])
            p = jnp.exp(s - jnp.max(s, axis=-1, keepdims=True))
            l = jnp.sum(p, axis=-1, keepdims=True)
            att_ref[rows, c] = (_dot(p.astype(BF16), mv_ref[0, :, c]) / l).astype(BF16)

    def o_proj(rows):
        y2 = _dot(att_ref[rows, :], wo_ref[...])
        o_ref[0, rows, :] = x1_ref[rows, :] + _rms(y2, gpost2_ref[...])

    tiles = [slice(r0, r0 + sub) for r0 in range(0, x_ref.shape[1], sub)]
    for stage in (out_proj, q_proj, cross_attn, o_proj):
        for rows in tiles:
            stage(rows)


def _mix_out(a, r, x, w_out, g_post, g_pre, w_q, mem_k, mem_v, w_o, g_post2, tm=1024, sub=256):
    batch, seq, d = x.shape
    half = a.shape[-1]
    n_mem = mem_k.shape[1]
    tok = lambda w: pl.BlockSpec((1, tm, w), lambda b, t: (b, t, 0))
    memblk = pl.BlockSpec((1, n_mem, d), lambda b, t: (b, 0, 0))
    vec = _resident((1, d))
    mat = _resident((d, d))
    return pl.pallas_call(
        functools.partial(_mix_out_kernel, sub=sub),
        out_shape=jax.ShapeDtypeStruct((batch, seq, d), F32),
        grid=(batch, seq // tm),
        in_specs=[tok(half), tok(half), tok(d), mat, vec, vec, mat, memblk, memblk, mat, vec],
        out_specs=tok(d),
        scratch_shapes=[pltpu.VMEM((tm, d), F32),
                        pltpu.VMEM((tm, d), BF16),
                        pltpu.VMEM((tm, d), BF16),
                        pltpu.VMEM((tm, d), BF16)],
        compiler_params=_params("parallel", "parallel"),
        name="mix_out",
    )(a, r, x, w_out, g_post, g_pre, w_q, mem_k, mem_v, w_o, g_post2)


def _ffn_kernel(x_ref, gpre_ref, wgu_ref, wdown_ref, gpost_ref, o_ref, *, chunks):
    x = x_ref[...]
    h = _rms(x, gpre_ref[...]).astype(BF16)
    hidden = wdown_ref.shape[0]
    y = None
    for lo, hi in chunks:
        g = _dot(h, wgu_ref[:, lo:hi])
        u = _dot(h, wgu_ref[:, hidden + lo:hidden + hi])
        act = (g * (1.0 / (1.0 + jnp.exp(-g))) * u).astype(BF16)
        part = _dot(act, wdown_ref[lo:hi, :])
        y = part if y is None else y + part
    o_ref[...] = x + _rms(y, gpost_ref[...])


def _ffn(x2d, g_pre, w_gu, w_down, g_post, tm=512):
    tokens, d = x2d.shape
    hidden = w_down.shape[0]
    mxu = 256
    cut = (hidden // 2) // mxu * mxu
    chunks = ((0, cut), (cut, hidden))
    row = pl.BlockSpec((tm, d), lambda t: (t, 0))
    vec = _resident((1, d))
    return pl.pallas_call(
        functools.partial(_ffn_kernel, chunks=chunks),
        out_shape=jax.ShapeDtypeStruct((tokens, d), F32),
        grid=(tokens // tm,),
        in_specs=[row, vec, _resident((d, 2 * hidden)), _resident((hidden, d)), vec],
        out_specs=row,
        compiler_params=_params("parallel"),
        name="ffn",
    )(x2d, g_pre, w_gu, w_down, g_post)


def kernel(x, mem, pre_mix_g, post_mix_g, w_in, attn_gn_g, ret_gn_g, w_out,
           pre_mem_g, post_mem_g, mem_norm_g, w_q_mem, w_kv_mem, w_o_mem,
           pre_ffn_g, post_ffn_g, w_gate_up, w_down):
    batch, seq, d = x.shape
    depth = w_in.shape[0]
    for l in range(depth):
        bf = lambda w: w[l].astype(BF16)
        vec = lambda g: g[l][None, :]
        att, (rq, rk, rv, rg) = _in_proj(x, vec(pre_mix_g), bf(w_in))
        qkv = [att[3 * p:3 * p + 3] for p in range(len(DILATED_PATTERNS))]
        a = _attention(qkv, vec(attn_gn_g))
        r = _retention(rq, rk, rv, rg, vec(ret_gn_g))
        mem_k, mem_v = _mem_kv(mem, vec(mem_norm_g), bf(w_kv_mem))
        x = _mix_out(a, r, x, bf(w_out), vec(post_mix_g), vec(pre_mem_g), bf(w_q_mem),
                     mem_k, mem_v, bf(w_o_mem), vec(post_mem_g))
        x = _ffn(x.reshape(batch * seq, d), vec(pre_ffn_g), bf(w_gate_up), bf(w_down),
                 vec(post_ffn_g)).reshape(batch, seq, d)
    return x
```

```python
import functools

import jax
import jax.numpy as jnp
from jax import lax
from jax.experimental import pallas as pl
from jax.experimental.pallas import tpu as pltpu

F32 = jnp.float32
BF16 = jnp.bfloat16

RMS_EPS = 1e-6
GN_EPS = 1e-5
ATT_HEADS = 8
ATT_HEAD_DIM = 64
ATT_WIDTH = ATT_HEADS * ATT_HEAD_DIM
ROT_DIM = ATT_HEAD_DIM // 4
ROPE_THETA = 500000.0
DILATED_PATTERNS = ((128, 1), (512, 4), (2048, 16))
RET_HEADS = 4
RET_HEAD_DIM = 128
RET_WIDTH = RET_HEADS * RET_HEAD_DIM
RET_CHUNK = 128
RET_ROPE_THETA = 10000.0
XATT_HEADS = 4

LANES = 128
BAND = 128
MASKED = -1e30
LOG2_E = 1.4426950408889634
VMEM_LIMIT = 56 * 1024 * 1024


def _resident(shape):
    zeros = (0,) * len(shape)
    return pl.BlockSpec(shape, lambda *_: zeros, pipeline_mode=pl.Buffered(1))


def _params(*sem):
    return pltpu.CompilerParams(dimension_semantics=sem, vmem_limit_bytes=VMEM_LIMIT)


def _rms(x, g):
    return x * lax.rsqrt(jnp.mean(x * x, axis=-1, keepdims=True) + RMS_EPS) * g


def _dot(a, b):
    return jnp.dot(a, b, preferred_element_type=F32)


def _dot_nt(a, b):
    return lax.dot_general(a, b, (((1,), (1,)), ((), ())), preferred_element_type=F32)


def _dot_tn(a, b):
    return lax.dot_general(a, b, (((0,), (0,)), ((), ())), preferred_element_type=F32)


def _in_proj_kernel(x_ref, g_ref, w_ref, ca_ref, sa1_ref, sa2_ref, cr_ref, sr_ref, *refs):
    att_refs = refs[:len(DILATED_PATTERNS)]
    ret_ref, h_ref, acc_ref, scr_ref, scr2_ref = refs[len(att_refs):]
    tm = x_ref.shape[1]
    width = ATT_WIDTH
    n_att = 3
    (_, r_one), (_, r_mid), (_, r_top) = DILATED_PATTERNS
    assert r_one == 1 and r_top % r_mid == 0
    rows_per_step = 64
    h_ref[...] = _rms(x_ref[0], g_ref[...]).astype(BF16)

    def project(i, slot):
        acc_ref[slot] = _dot(h_ref[...], w_ref[:, i * width:(i + 1) * width])

    def rope_attn(t, rows):
        return (t * ca_ref[rows, :]
                + pltpu.roll(t, LANES - ROT_DIM // 2, 1) * sa1_ref[rows, :]
                + pltpu.roll(t, ROT_DIM // 2, 1) * sa2_ref[rows, :])

    def rope_ret(t, rows):
        return t * cr_ref[rows, :] + pltpu.roll(t, RET_HEAD_DIM // 2, 1) * sr_ref[rows, :]

    def post(i, slot):
        for gi, c0 in enumerate(range(0, width, LANES)):
            cols = slice(c0, c0 + LANES)
            for r0 in range(0, tm, rows_per_step):
                rows = slice(r0, r0 + rows_per_step)
                t = acc_ref[slot, rows, cols]
                if i == 0:
                    t = rope_attn(t, rows) * (ATT_HEAD_DIM ** -0.5 * LOG2_E)
                elif i == 1:
                    t = rope_attn(t, rows)
                elif i == n_att:
                    t = rope_ret(t, rows)
                elif i == n_att + 1:
                    t = rope_ret(t, rows) * (RET_HEAD_DIM ** -0.5)
                if i >= n_att:
                    ocols = slice((i - n_att) * width + c0, (i - n_att) * width + c0 + LANES)
                    ret_ref[0, rows, ocols] = t.astype(BF16)
                else:
                    ocols = slice(i * width + c0, i * width + c0 + LANES)
                    scr_ref[i, gi, rows, :] = t
                    att_refs[0][0, 0, rows, ocols] = t.astype(BF16)
            if i < n_att:
                for c4 in range(r_mid):
                    picked = scr_ref[i, gi, pl.ds(c4, tm // r_mid, stride=r_mid), :]
                    att_refs[1][0, c4, :, ocols] = picked.astype(BF16)
                    scr2_ref[i, gi, c4] = picked
                for c4 in range(r_mid):
                    for m in range(r_top // r_mid):
                        picked = scr2_ref[i, gi, c4, pl.ds(m, tm // r_top, stride=r_top // r_mid), :]
                        att_refs[2][0, m * r_mid + c4, :, ocols] = picked.astype(BF16)

    n_streams = n_att + 4
    project(0, 0)
    for i in range(n_streams):
        if i + 1 < n_streams:
            project(i + 1, (i + 1) % 2)
        post(i, i % 2)


def _rope_tables(seq):
    pos = jnp.arange(seq, dtype=F32)[:, None]
    half = ROT_DIM // 2
    inv = ROPE_THETA ** (-(jnp.arange(0, ROT_DIM, 2, dtype=F32) / ROT_DIM))
    ang = pos * inv[None, :]
    cos, sin = jnp.cos(ang), jnp.sin(ang)
    one = jnp.ones((seq, ATT_HEAD_DIM - ROT_DIM), F32)
    zero = jnp.zeros((seq, ATT_HEAD_DIM - ROT_DIM), F32)
    zh = jnp.zeros((seq, half), F32)
    ca = jnp.concatenate([cos, cos, one], axis=1)
    sa1 = jnp.concatenate([-sin, zh, zero], axis=1)
    sa2 = jnp.concatenate([zh, sin, zero], axis=1)
    reps = LANES // ATT_HEAD_DIM
    ca, sa1, sa2 = (jnp.tile(t, (1, reps)) for t in (ca, sa1, sa2))
    inv_r = RET_ROPE_THETA ** (-(jnp.arange(0, RET_HEAD_DIM, 2, dtype=F32) / RET_HEAD_DIM))
    ang_r = pos * inv_r[None, :]
    cr = jnp.concatenate([jnp.cos(ang_r), jnp.cos(ang_r)], axis=1)
    sr = jnp.concatenate([-jnp.sin(ang_r), jnp.sin(ang_r)], axis=1)
    return ca, sa1, sa2, cr, sr


def _in_proj(x, g, w_in, tm=512):
    batch, seq, d = x.shape
    width = w_in.shape[1]
    tok = lambda w: pl.BlockSpec((1, tm, w), lambda b, t: (b, t, 0))
    tab = pl.BlockSpec((tm, LANES), lambda b, t: (t, 0))
    att_shapes, att_specs = [], []
    for _, r in DILATED_PATTERNS:
        assert tm % (r * 16) == 0
        att_shapes.append(jax.ShapeDtypeStruct((batch, r, seq // r, 3 * ATT_WIDTH), BF16))
        att_specs.append(pl.BlockSpec((1, r, tm // r, 3 * ATT_WIDTH), lambda b, t: (b, 0, t, 0)))
    ret_shape = jax.ShapeDtypeStruct((batch, seq, 4 * RET_WIDTH), BF16)
    outs = pl.pallas_call(
        _in_proj_kernel,
        out_shape=tuple(att_shapes) + (ret_shape,),
        grid=(batch, seq // tm),
        in_specs=[tok(d), _resident((1, d)), _resident((d, width))] + [tab] * 5,
        out_specs=tuple(att_specs) + (tok(4 * RET_WIDTH),),
        scratch_shapes=[pltpu.VMEM((tm, d), BF16),
                        pltpu.VMEM((2, tm, ATT_WIDTH), F32),
                        pltpu.VMEM((3, ATT_WIDTH // LANES, tm, LANES), F32),
                        pltpu.VMEM((3, ATT_WIDTH // LANES, DILATED_PATTERNS[1][1],
                                    tm // DILATED_PATTERNS[1][1], LANES), F32)],
        compiler_params=_params("parallel", "parallel"),
        name="in_proj",
    )(x, g, w_in, *_rope_tables(seq))
    return outs[:len(att_shapes)], outs[len(att_shapes)]


def _attn_kernel(*refs, nc, nq, seq_tiles, has_prev, is_last):
    refs = list(refs)
    main_ref = refs.pop(0)
    if seq_tiles > 1:
        halo_ref = refs.pop(0)
    if has_prev:
        po_ref, ps_ref = refs[:2]
        del refs[:2]
    if is_last:
        g_ref = refs.pop(0)
    o_ref = refs.pop(0)
    if not is_last:
        s_ref = refs.pop(0)
    kbuf, vbuf, sc_buf, p_buf, bias = refs[:5]
    del refs[:5]
    if nc > 1:
        obuf, sbuf = refs

    half = ATT_HEAD_DIM
    groups = ATT_WIDTH // LANES
    chunk = 32
    kcols = slice(ATT_WIDTH, 2 * ATT_WIDTH)
    vcols = slice(2 * ATT_WIDTH, 3 * ATT_WIDTH)
    for cls in range(nc):
        if seq_tiles > 1:
            kbuf[cls, :BAND] = halo_ref[0, cls, :, kcols]
            vbuf[cls, :BAND] = halo_ref[0, cls, :, vcols]
        else:
            kbuf[cls, :BAND] = jnp.zeros((BAND, ATT_WIDTH), BF16)
            vbuf[cls, :BAND] = jnp.zeros((BAND, ATT_WIDTH), BF16)
        kbuf[cls, BAND:] = main_ref[0, cls, :, kcols]
        vbuf[cls, BAND:] = main_ref[0, cls, :, vcols]

    key = lax.broadcasted_iota(jnp.int32, (2 * BAND, 2 * BAND), 0)
    qry = lax.broadcasted_iota(jnp.int32, (2 * BAND, 2 * BAND), 1) & (BAND - 1)
    if seq_tiles == 1:
        floor = BAND
    else:
        floor = jnp.where(pl.program_id(1) == 0, BAND, 0)
    upper = key <= qry + BAND
    bias[0] = jnp.where((key >= jnp.maximum(qry, floor)) & upper, 0.0, MASKED)
    bias[1] = jnp.where((key >= qry) & upper, 0.0, MASKED)

    lane = lax.broadcasted_iota(jnp.int32, (BAND, LANES), 1)
    lo = lane < half
    row8 = lax.broadcasted_iota(jnp.int32, (8, BAND), 0)

    def scores(cls, qstart, slot, which):
        tops = []
        for gi, c0 in enumerate(range(0, ATT_WIDTH, LANES)):
            cols = slice(c0, c0 + LANES)
            qp = main_ref[0, cls, pl.ds(qstart, BAND), cols]
            zero = jnp.zeros_like(qp)
            q2 = jnp.concatenate([jnp.where(lo, qp, zero), jnp.where(lo, zero, qp)], axis=0)
            s = _dot_nt(kbuf[cls, pl.ds(qstart, 2 * BAND), cols], q2) + bias[which]
            sc_buf[slot, gi] = s
            top = s[0:8, :]
            for r8 in range(8, 2 * BAND, 8):
                top = jnp.maximum(top, s[r8:r8 + 8, :])
            tops.append(jnp.max(top, axis=0, keepdims=True))
        return tops

    def softmax(slot, tops):
        for gi in range(groups):
            for c in range(0, 2 * BAND, chunk):
                e = jnp.exp2(sc_buf[slot, gi, c:c + chunk, :] - tops[gi])
                p_buf[gi, c:c + chunk, :] = e.astype(BF16)

    ones_rows = jnp.ones((16, 2 * BAND), BF16)

    def finish(cls, qstart, tops):
        qs = pl.ds(qstart, BAND)
        ks = pl.ds(qstart, 2 * BAND)
        if has_prev:
            prev_lse = ps_ref[0, cls, qs, :].T
        stats = jnp.zeros((8, BAND), F32)
        for gi, c0 in enumerate(range(0, ATT_WIDTH, LANES)):
            cols = slice(c0, c0 + LANES)
            m = tops[gi]
            lhs = jnp.concatenate([vbuf[cls, ks, cols].T, ones_rows], axis=0)
            ot = _dot(lhs, p_buf[gi])
            l = ot[LANES:LANES + 1, :]
            inv = 1.0 / l
            lse = m + jnp.log2(l)
            if has_prev:
                prev_o = po_ref[0, cls, qs, cols].astype(F32).T
            parts = []
            for hh in range(LANES // half):
                head = 2 * gi + hh
                qcols = slice(hh * BAND, (hh + 1) * BAND)
                rows = slice(hh * half, (hh + 1) * half)
                w_new = inv[:, qcols]
                lse_h = lse[:, qcols]
                if has_prev:
                    lp = prev_lse[head:head + 1, :]
                    top = jnp.maximum(lp, lse_h)
                    wp = jnp.exp2(lp - top)
                    wn = jnp.exp2(lse_h - top)
                    den = wp + wn
                    o = (wp / den) * prev_o[rows, :] + (wn * w_new / den) * ot[rows, qcols]
                    lse_h = top + jnp.log2(den)
                else:
                    o = ot[rows, qcols] * w_new
                if is_last:
                    ms = jnp.mean(o * o, axis=0, keepdims=True)
                    o = o * lax.rsqrt(ms + RMS_EPS)
                parts.append(o)
                stats = jnp.where(row8 == head, lse_h, stats)
            o = jnp.concatenate(parts, axis=0).T
            if is_last:
                o = o * g_ref[:, cols]
            if nc > 1:
                obuf[gi, pl.ds(qstart * nc + cls, BAND, stride=nc), :] = o
            else:
                o_ref[0, 0, qs, cols] = o.astype(o_ref.dtype)
        if not is_last:
            st = jnp.concatenate([stats, jnp.zeros((BAND - 8, BAND), F32)], axis=0).T
            if nc > 1:
                sbuf[pl.ds(qstart * nc + cls, BAND, stride=nc), :] = st
            else:
                s_ref[0, 0, qs, :] = st

    blocks = [(cls, n * BAND) for cls in range(nc) for n in range(nq)]
    which = lambda blk: 0 if blk[1] == 0 else 1
    tops = scores(*blocks[0], 0, which(blocks[0]))
    for i, (cls, qstart) in enumerate(blocks):
        if i + 1 < len(blocks):
            tops_next = scores(*blocks[i + 1], (i + 1) % 2, which(blocks[i + 1]))
        softmax(i % 2, tops)
        finish(cls, qstart, tops)
        tops = tops_next
    if nc > 1:
        for gi, c0 in enumerate(range(0, ATT_WIDTH, LANES)):
            o_ref[0, 0, :, c0:c0 + LANES] = obuf[gi].astype(o_ref.dtype)
        if not is_last:
            s_ref[0, 0] = sbuf[...]


def _attention(qkv, gn_g):
    w = ATT_WIDTH
    order = sorted(range(len(DILATED_PATTERNS)), key=lambda p: -DILATED_PATTERNS[p][1])
    prev = None
    for step, p in enumerate(order):
        window, r = DILATED_PATTERNS[p]
        batch, _, length, _ = qkv[p].shape
        is_last = step == len(order) - 1
        r_next = 1 if is_last else DILATED_PATTERNS[order[step + 1]][1]
        nc = r // r_next
        assert window // r == BAND and r == nc * r_next
        nq = 2 if nc > 1 else 8
        seq_tiles = length // (nq * BAND)
        assert length % (nq * BAND) == 0
        rows = nq * BAND

        if nc > 1:
            shape5 = lambda t: t.reshape(batch, nc, r_next, length, t.shape[-1])
            main = lambda wd: pl.BlockSpec((1, nc, None, rows, wd), lambda b, t: (b, 0, t // seq_tiles, t % seq_tiles, 0))
            halo = lambda wd: pl.BlockSpec(
                (1, nc, None, BAND, wd),
                lambda b, t: (b, 0, t // seq_tiles, jnp.maximum(t % seq_tiles * nq - 1, 0), 0))
            grid = (batch, r_next * seq_tiles)
            out_blk = lambda wd: pl.BlockSpec((1, 1, nc * rows, wd), lambda b, t: (b, t // seq_tiles, t % seq_tiles, 0))
        else:
            shape5 = lambda t: t
            main = lambda wd: pl.BlockSpec((1, 1, rows, wd), lambda b, t: (b, 0, t, 0))
            halo = lambda wd: pl.BlockSpec((1, 1, BAND, wd), lambda b, t: (b, 0, jnp.maximum(t * nq - 1, 0), 0))
            grid = (batch, seq_tiles)
            out_blk = lambda wd: pl.BlockSpec((1, 1, rows, wd), lambda b, t: (b, 0, t, 0))
        if nc > 1 and seq_tiles > 1:
            assert r_next == 1

        args = [shape5(qkv[p])]
        in_specs = [main(3 * w)]
        if seq_tiles > 1:
            args.append(shape5(qkv[p]))
            in_specs.append(halo(3 * w))
        if prev is not None:
            args += [shape5(prev[0]), shape5(prev[1])]
            in_specs += [main(w), main(LANES)]
        if is_last:
            args.append(gn_g)
            in_specs.append(_resident((1, w)))
        o_shape = jax.ShapeDtypeStruct((batch, r_next, length * nc, w), BF16)
        s_shape = jax.ShapeDtypeStruct((batch, r_next, length * nc, LANES), F32)
        groups = w // LANES
        scratch = [pltpu.VMEM((nc, rows + BAND, w), BF16)] * 2 + [
            pltpu.VMEM((2, groups, 2 * BAND, 2 * BAND), F32),
            pltpu.VMEM((groups, 2 * BAND, 2 * BAND), BF16),
            pltpu.VMEM((2, 2 * BAND, 2 * BAND), F32)]
        if nc > 1:
            scratch += [pltpu.VMEM((w // LANES, nc * rows, LANES), F32), pltpu.VMEM((nc * rows, LANES), F32)]
        res = pl.pallas_call(
            functools.partial(_attn_kernel, nc=nc, nq=nq, seq_tiles=seq_tiles,
                              has_prev=prev is not None, is_last=is_last),
            out_shape=o_shape if is_last else (o_shape, s_shape),
            grid=grid,
            in_specs=in_specs,
            out_specs=out_blk(w) if is_last else (out_blk(w), out_blk(LANES)),
            scratch_shapes=scratch,
            compiler_params=_params("parallel", "arbitrary"),
            name=f"attn_r{r}",
        )(*args)
        if is_last:
            return res.reshape(batch, length * nc, w)
        prev = res


def _retention_kernel(in_ref, dmask_ref, xi_ref, zeta_ref, cdec_ref, g_ref,
                      o_ref, state_ref, raw_ref, upd_ref, pre_ref):
    @pl.when(pl.program_id(1) == 0)
    def _():
        state_ref[...] = jnp.zeros_like(state_ref)

    chunk = RET_CHUNK

    def where(t, h):
        return slice(t, t + chunk), slice(h * RET_HEAD_DIM, (h + 1) * RET_HEAD_DIM)

    def stream(j, t, h):
        c0 = j * RET_WIDTH + h * RET_HEAD_DIM
        return in_ref[0, t:t + chunk, c0:c0 + RET_HEAD_DIM]

    def scores(t, h, slot):
        k = stream(1, t, h)
        raw_ref[slot] = _dot_nt(stream(0, t, h), k)
        kz = (k.astype(F32) * zeta_ref[h]).astype(BF16)
        upd_ref[slot] = _dot_tn(kz, stream(2, t, h))

    def mix(t, h, slot):
        inner = (raw_ref[slot] * dmask_ref[h]).astype(BF16)
        state = state_ref[h]
        pre_ref[slot] = (_dot(inner, stream(2, t, h))
                         + _dot(stream(0, t, h), state.astype(BF16)) * xi_ref[h])
        state_ref[h] = state * cdec_ref[h] + upd_ref[slot]

    def finish(t, h, slot):
        rows, c = where(t, h)
        o = pre_ref[slot]
        mu = jnp.mean(o, axis=-1, keepdims=True)
        var = jnp.mean(jnp.square(o - mu), axis=-1, keepdims=True)
        y = (o - mu) * lax.rsqrt(var + GN_EPS) * g_ref[:, c]
        gate = stream(3, t, h).astype(F32)
        o_ref[0, rows, c] = (gate * (1.0 / (1.0 + jnp.exp(-gate))) * y).astype(o_ref.dtype)

    units = [(t, h) for t in range(0, in_ref.shape[1], chunk) for h in range(RET_HEADS)]
    scores(*units[0], 0)
    for i, unit in enumerate(units):
        if i + 1 < len(units):
            scores(*units[i + 1], (i + 1) % 2)
        mix(*unit, i % 2)
        if i > 0:
            finish(*units[i - 1], (i - 1) % 2)
    finish(*units[-1], (len(units) - 1) % 2)


def _retention_tables():
    c = RET_CHUNK
    log_g = jnp.log(1.0 - 2.0 ** (-5.0 - jnp.arange(RET_HEADS, dtype=F32)))
    n = jnp.arange(c, dtype=F32)
    rel = n[:, None] - n[None, :]
    dmask = jnp.where(rel >= 0, jnp.exp(log_g[:, None, None] * jnp.maximum(rel, 0.0)), 0.0)
    xi = jnp.exp(log_g[:, None] * (n + 1.0))
    zeta = jnp.exp(log_g[:, None] * (c - 1.0 - n))
    cdec = jnp.exp(log_g * c)
    wide = lambda t: jnp.broadcast_to(t[:, :, None], (RET_HEADS, c, RET_HEAD_DIM))
    cdec = jnp.broadcast_to(cdec[:, None, None], (RET_HEADS, RET_HEAD_DIM, RET_HEAD_DIM))
    return dmask, wide(xi), wide(zeta), cdec


def _retention(qkvg, gn_g, chunks_per_step=8):
    batch, seq, _ = qkvg.shape
    width = RET_WIDTH
    c = RET_CHUNK
    rows = c * chunks_per_step
    blk = lambda wd: pl.BlockSpec((1, rows, wd), lambda b, n: (b, n, 0))
    tab = _resident((RET_HEADS, c, RET_HEAD_DIM))
    return pl.pallas_call(
        _retention_kernel,
        out_shape=jax.ShapeDtypeStruct((batch, seq, width), BF16),
        grid=(batch, seq // rows),
        in_specs=[blk(4 * width)] + [tab] * 4 + [_resident((1, width))],
        out_specs=blk(width),
        scratch_shapes=[pltpu.VMEM((RET_HEADS, RET_HEAD_DIM, RET_HEAD_DIM), F32),
                        pltpu.VMEM((2, c, c), F32),
                        pltpu.VMEM((2, RET_HEAD_DIM, RET_HEAD_DIM), F32),
                        pltpu.VMEM((2, c, RET_HEAD_DIM), F32)],
        compiler_params=_params("parallel", "arbitrary"),
        name="retention",
    )(qkvg, *_retention_tables(), gn_g)


def _mem_kv_kernel(mem_ref, g_ref, w_ref, k_ref, v_ref):
    d = mem_ref.shape[-1]
    kv = _dot(_rms(mem_ref[0], g_ref[...]).astype(BF16), w_ref[...])
    k_ref[0] = kv[:, :d].astype(BF16)
    v_ref[0] = kv[:, d:].astype(BF16)


def _mem_kv(mem, g, w_kv):
    batch, n_mem, d = mem.shape
    blk = pl.BlockSpec((1, n_mem, d), lambda b: (b, 0, 0))
    out = jax.ShapeDtypeStruct((batch, n_mem, d), BF16)
    return pl.pallas_call(
        _mem_kv_kernel,
        out_shape=(out, out),
        grid=(batch,),
        in_specs=[blk, _resident((1, d)), _resident((d, 2 * d))],
        out_specs=(blk, blk),
        compiler_params=_params("parallel"),
        name="mem_kv",
    )(mem, g, w_kv)


def _mix_out_kernel(a_ref, r_ref, x_ref, wout_ref, gpost_ref, gpre_ref, wq_ref, mk_ref, mv_ref,
                    wo_ref, gpost2_ref, o_ref, x1_ref, h_ref, q_ref, att_ref, *, sub):
    half = a_ref.shape[-1]
    d = x_ref.shape[-1]
    dh = d // XATT_HEADS

    def out_proj(rows):
        y = _dot(a_ref[0, rows, :], wout_ref[:half, :]) + _dot(r_ref[0, rows, :], wout_ref[half:, :])
        x1 = x_ref[0, rows, :] + _rms(y, gpost_ref[...])
        x1_ref[rows, :] = x1
        h_ref[rows, :] = _rms(x1, gpre_ref[...]).astype(BF16)

    def q_proj(rows):
        q_ref[rows, :] = (_dot(h_ref[rows, :], wq_ref[...]) * (dh ** -0.5)).astype(BF16)

    def cross_attn(rows):
        for hd in range(XATT_HEADS):
            c = slice(hd * dh, (hd + 1) * dh)
            s = _dot_nt(q_ref[rows, c], mk_ref[0, :, c])
            p = jnp.exp(s - jnp.max(s, axis=-1, keepdims=True))
            l = jnp.sum(p, axis=-1, keepdims=True)
            att_ref[rows, c] = (_dot(p.astype(BF16), mv_ref[0, :, c]) / l).astype(BF16)

    def o_proj(rows):
        y2 = _dot(att_ref[rows, :], wo_ref[...])
        o_ref[0, rows, :] = x1_ref[rows, :] + _rms(y2, gpost2_ref[...])

    tiles = [slice(r0, r0 + sub) for r0 in range(0, x_ref.shape[1], sub)]
    for stage in (out_proj, q_proj, cross_attn, o_proj):
        for rows in tiles:
            stage(rows)


def _mix_out(a, r, x, w_out, g_post, g_pre, w_q, mem_k, mem_v, w_o, g_post2, tm=1024, sub=256):
    batch, seq, d = x.shape
    half = a.shape[-1]
    n_mem = mem_k.shape[1]
    tok = lambda w: pl.BlockSpec((1, tm, w), lambda b, t: (b, t, 0))
    memblk = pl.BlockSpec((1, n_mem, d), lambda b, t: (b, 0, 0))
    vec = _resident((1, d))
    mat = _resident((d, d))
    return pl.pallas_call(
        functools.partial(_mix_out_kernel, sub=sub),
        out_shape=jax.ShapeDtypeStruct((batch, seq, d), F32),
        grid=(batch, seq // tm),
        in_specs=[tok(half), tok(half), tok(d), mat, vec, vec, mat, memblk, memblk, mat, vec],
        out_specs=tok(d),
        scratch_shapes=[pltpu.VMEM((tm, d), F32),
                        pltpu.VMEM((tm, d), BF16),
                        pltpu.VMEM((tm, d), BF16),
                        pltpu.VMEM((tm, d), BF16)],
        compiler_params=_params("parallel", "parallel"),
        name="mix_out",
    )(a, r, x, w_out, g_post, g_pre, w_q, mem_k, mem_v, w_o, g_post2)


def _ffn_kernel(x_ref, gpre_ref, wgu_ref, wdown_ref, gpost_ref, o_ref, *, chunks):
    x = x_ref[...]
    h = _rms(x, gpre_ref[...]).astype(BF16)
    hidden = wdown_ref.shape[0]
    y = None
    for lo, hi in chunks:
        g = _dot(h, wgu_ref[:, lo:hi])
        u = _dot(h, wgu_ref[:, hidden + lo:hidden + hi])
        act = (g * (1.0 / (1.0 + jnp.exp(-g))) * u).astype(BF16)
        part = _dot(act, wdown_ref[lo:hi, :])
        y = part if y is None else y + part
    o_ref[...] = x + _rms(y, gpost_ref[...])


def _ffn(x2d, g_pre, w_gu, w_down, g_post, tm=512):
    tokens, d = x2d.shape
    hidden = w_down.shape[0]
    mxu = 256
    cut = (hidden // 2) // mxu * mxu
    chunks = ((0, cut), (cut, hidden))
    row = pl.BlockSpec((tm, d), lambda t: (t, 0))
    vec = _resident((1, d))
    return pl.pallas_call(
        functools.partial(_ffn_kernel, chunks=chunks),
        out_shape=jax.ShapeDtypeStruct((tokens, d), F32),
        grid=(tokens // tm,),
        in_specs=[row, vec, _resident((d, 2 * hidden)), _resident((hidden, d)), vec],
        out_specs=row,
        compiler_params=_params("parallel"),
        name="ffn",
    )(x2d, g_pre, w_gu, w_down, g_post)


def kernel(x, mem, pre_mix_g, post_mix_g, w_in, attn_gn_g, ret_gn_g, w_out,
           pre_mem_g, post_mem_g, mem_norm_g, w_q_mem, w_kv_mem, w_o_mem,
           pre_ffn_g, post_ffn_g, w_gate_up, w_down):
    batch, seq, d = x.shape
    depth = w_in.shape[0]
    for l in range(depth):
        bf = lambda w: w[l].astype(BF16)
        vec = lambda g: g[l][None, :]
        qkv, ret_in = _in_proj(x, vec(pre_mix_g), bf(w_in))
        a = _attention(qkv, vec(attn_gn_g))
        r = _retention(ret_in, vec(ret_gn_g))
        mem_k, mem_v = _mem_kv(mem, vec(mem_norm_g), bf(w_kv_mem))
        x = _mix_out(a, r, x, bf(w_out), vec(post_mix_g), vec(pre_mem_g), bf(w_q_mem),
                     mem_k, mem_v, bf(w_o_mem), vec(post_mem_g))
        x = _ffn(x.reshape(batch * seq, d), vec(pre_ffn_g), bf(w_gate_up), bf(w_down),
                 vec(post_ffn_g)).reshape(batch, seq, d)
    return x
```

```python
import functools

import jax
import jax.numpy as jnp
from jax import lax
from jax.experimental import pallas as pl
from jax.experimental.pallas import tpu as pltpu

F32 = jnp.float32
BF16 = jnp.bfloat16

RMS_EPS = 1e-6
GN_EPS = 1e-5
ATT_HEADS = 8
ATT_HEAD_DIM = 64
ATT_WIDTH = ATT_HEADS * ATT_HEAD_DIM
ROT_DIM = ATT_HEAD_DIM // 4
ROPE_THETA = 500000.0
DILATED_PATTERNS = ((128, 1), (512, 4), (2048, 16))
RET_HEADS = 4
RET_HEAD_DIM = 128
RET_WIDTH = RET_HEADS * RET_HEAD_DIM
RET_CHUNK = 128
RET_ROPE_THETA = 10000.0
XATT_HEADS = 4

LANES = 128
BAND = 128
MASKED = -1e30
LOG2_E = 1.4426950408889634
VMEM_LIMIT = 56 * 1024 * 1024


def _resident(shape):
    zeros = (0,) * len(shape)
    return pl.BlockSpec(shape, lambda *_: zeros, pipeline_mode=pl.Buffered(1))


def _params(*sem):
    return pltpu.CompilerParams(dimension_semantics=sem, vmem_limit_bytes=VMEM_LIMIT)


def _rms(x, g):
    return x * lax.rsqrt(jnp.mean(x * x, axis=-1, keepdims=True) + RMS_EPS) * g


def _dot(a, b):
    return jnp.dot(a, b, preferred_element_type=F32)


def _dot_nt(a, b):
    return lax.dot_general(a, b, (((1,), (1,)), ((), ())), preferred_element_type=F32)


def _dot_tn(a, b):
    return lax.dot_general(a, b, (((0,), (0,)), ((), ())), preferred_element_type=F32)


def _in_proj_kernel(x_ref, g_ref, w_ref, ca_ref, sa1_ref, sa2_ref, cr_ref, sr_ref, *refs):
    att_refs = refs[:len(DILATED_PATTERNS)]
    ret_ref, h_ref, acc_ref, scr_ref, scr2_ref = refs[len(att_refs):]
    tm = x_ref.shape[1]
    width = ATT_WIDTH
    n_att = 3
    (_, r_one), (_, r_mid), (_, r_top) = DILATED_PATTERNS
    assert r_one == 1 and r_top % r_mid == 0
    rows_per_step = 64
    h_ref[...] = _rms(x_ref[0], g_ref[...]).astype(BF16)

    def project(i, slot):
        acc_ref[slot] = _dot(h_ref[...], w_ref[:, i * width:(i + 1) * width])

    def rope_attn(t, rows):
        return (t * ca_ref[rows, :]
                + pltpu.roll(t, LANES - ROT_DIM // 2, 1) * sa1_ref[rows, :]
                + pltpu.roll(t, ROT_DIM // 2, 1) * sa2_ref[rows, :])

    def rope_ret(t, rows):
        return t * cr_ref[rows, :] + pltpu.roll(t, RET_HEAD_DIM // 2, 1) * sr_ref[rows, :]

    def post(i, slot):
        for gi, c0 in enumerate(range(0, width, LANES)):
            cols = slice(c0, c0 + LANES)
            for r0 in range(0, tm, rows_per_step):
                rows = slice(r0, r0 + rows_per_step)
                t = acc_ref[slot, rows, cols]
                if i == 0:
                    t = rope_attn(t, rows) * (ATT_HEAD_DIM ** -0.5 * LOG2_E)
                elif i == 1:
                    t = rope_attn(t, rows)
                elif i == n_att:
                    t = rope_ret(t, rows)
                elif i == n_att + 1:
                    t = rope_ret(t, rows) * (RET_HEAD_DIM ** -0.5)
                if i >= n_att:
                    ocols = slice((i - n_att) * width + c0, (i - n_att) * width + c0 + LANES)
                    ret_ref[0, rows, ocols] = t.astype(BF16)
                else:
                    ocols = slice(i * width + c0, i * width + c0 + LANES)
                    scr_ref[i, gi, rows, :] = t
                    att_refs[0][0, 0, rows, ocols] = t.astype(BF16)
            if i < n_att:
                for c4 in range(r_mid):
                    picked = scr_ref[i, gi, pl.ds(c4, tm // r_mid, stride=r_mid), :]
                    att_refs[1][0, c4, :, ocols] = picked.astype(BF16)
                    scr2_ref[i, gi, c4] = picked
                for c4 in range(r_mid):
                    for m in range(r_top // r_mid):
                        picked = scr2_ref[i, gi, c4, pl.ds(m, tm // r_top, stride=r_top // r_mid), :]
                        att_refs[2][0, m * r_mid + c4, :, ocols] = picked.astype(BF16)

    n_streams = n_att + 4
    project(0, 0)
    for i in range(n_streams):
        if i + 1 < n_streams:
            project(i + 1, (i + 1) % 2)
        post(i, i % 2)


def _rope_tables(seq):
    pos = jnp.arange(seq, dtype=F32)[:, None]
    half = ROT_DIM // 2
    inv = ROPE_THETA ** (-(jnp.arange(0, ROT_DIM, 2, dtype=F32) / ROT_DIM))
    ang = pos * inv[None, :]
    cos, sin = jnp.cos(ang), jnp.sin(ang)
    one = jnp.ones((seq, ATT_HEAD_DIM - ROT_DIM), F32)
    zero = jnp.zeros((seq, ATT_HEAD_DIM - ROT_DIM), F32)
    zh = jnp.zeros((seq, half), F32)
    ca = jnp.concatenate([cos, cos, one], axis=1)
    sa1 = jnp.concatenate([-sin, zh, zero], axis=1)
    sa2 = jnp.concatenate([zh, sin, zero], axis=1)
    reps = LANES // ATT_HEAD_DIM
    ca, sa1, sa2 = (jnp.tile(t, (1, reps)) for t in (ca, sa1, sa2))
    inv_r = RET_ROPE_THETA ** (-(jnp.arange(0, RET_HEAD_DIM, 2, dtype=F32) / RET_HEAD_DIM))
    ang_r = pos * inv_r[None, :]
    cr = jnp.concatenate([jnp.cos(ang_r), jnp.cos(ang_r)], axis=1)
    sr = jnp.concatenate([-jnp.sin(ang_r), jnp.sin(ang_r)], axis=1)
    return ca, sa1, sa2, cr, sr


def _in_proj(x, g, w_in, tm=512):
    batch, seq, d = x.shape
    width = w_in.shape[1]
    tok = lambda w: pl.BlockSpec((1, tm, w), lambda b, t: (b, t, 0))
    tab = pl.BlockSpec((tm, LANES), lambda b, t: (t, 0))
    att_shapes, att_specs = [], []
    for _, r in DILATED_PATTERNS:
        assert tm % (r * 16) == 0
        att_shapes.append(jax.ShapeDtypeStruct((batch, r, seq // r, 3 * ATT_WIDTH), BF16))
        att_specs.append(pl.BlockSpec((1, r, tm // r, 3 * ATT_WIDTH), lambda b, t: (b, 0, t, 0)))
    ret_shape = jax.ShapeDtypeStruct((batch, seq, 4 * RET_WIDTH), BF16)
    outs = pl.pallas_call(
        _in_proj_kernel,
        out_shape=tuple(att_shapes) + (ret_shape,),
        grid=(batch, seq // tm),
        in_specs=[tok(d), _resident((1, d)), _resident((d, width))] + [tab] * 5,
        out_specs=tuple(att_specs) + (tok(4 * RET_WIDTH),),
        scratch_shapes=[pltpu.VMEM((tm, d), BF16),
                        pltpu.VMEM((2, tm, ATT_WIDTH), F32),
                        pltpu.VMEM((3, ATT_WIDTH // LANES, tm, LANES), F32),
                        pltpu.VMEM((3, ATT_WIDTH // LANES, DILATED_PATTERNS[1][1],
                                    tm // DILATED_PATTERNS[1][1], LANES), F32)],
        compiler_params=_params("parallel", "parallel"),
        name="in_proj",
    )(x, g, w_in, *_rope_tables(seq))
    return outs[:len(att_shapes)], outs[len(att_shapes)]


def _attn_kernel(*refs, nc, nq, seq_tiles, has_prev, is_last):
    refs = list(refs)
    main_ref = refs.pop(0)
    if seq_tiles > 1:
        halo_ref = refs.pop(0)
    if has_prev:
        po_ref, ps_ref = refs[:2]
        del refs[:2]
    if is_last:
        g_ref = refs.pop(0)
    o_ref = refs.pop(0)
    if not is_last:
        s_ref = refs.pop(0)
    kbuf, vbuf, sc_buf, p_buf, bias = refs[:5]
    del refs[:5]
    if nc > 1:
        obuf, sbuf = refs

    half = ATT_HEAD_DIM
    groups = ATT_WIDTH // LANES
    chunk = 32
    kcols = slice(ATT_WIDTH, 2 * ATT_WIDTH)
    vcols = slice(2 * ATT_WIDTH, 3 * ATT_WIDTH)
    for cls in range(nc):
        if seq_tiles > 1:
            kbuf[cls, :BAND] = halo_ref[0, cls, :, kcols]
            vbuf[cls, :BAND] = halo_ref[0, cls, :, vcols]
        else:
            kbuf[cls, :BAND] = jnp.zeros((BAND, ATT_WIDTH), BF16)
            vbuf[cls, :BAND] = jnp.zeros((BAND, ATT_WIDTH), BF16)
        kbuf[cls, BAND:] = main_ref[0, cls, :, kcols]
        vbuf[cls, BAND:] = main_ref[0, cls, :, vcols]

    key = lax.broadcasted_iota(jnp.int32, (2 * BAND, 2 * BAND), 0)
    qry = lax.broadcasted_iota(jnp.int32, (2 * BAND, 2 * BAND), 1) & (BAND - 1)
    if seq_tiles == 1:
        floor = BAND
    else:
        floor = jnp.where(pl.program_id(1) == 0, BAND, 0)
    upper = key <= qry + BAND
    bias[0] = jnp.where((key >= jnp.maximum(qry, floor)) & upper, 0.0, MASKED)
    bias[1] = jnp.where((key >= qry) & upper, 0.0, MASKED)

    lane = lax.broadcasted_iota(jnp.int32, (BAND, LANES), 1)
    lo = lane < half
    row8 = lax.broadcasted_iota(jnp.int32, (8, BAND), 0)

    def scores(cls, qstart, slot, which):
        for gi, c0 in enumerate(range(0, ATT_WIDTH, LANES)):
            cols = slice(c0, c0 + LANES)
            qp = main_ref[0, cls, pl.ds(qstart, BAND), cols]
            zero = jnp.zeros_like(qp)
            q2 = jnp.concatenate([jnp.where(lo, qp, zero), jnp.where(lo, zero, qp)], axis=0)
            sc_buf[slot, gi] = _dot_nt(kbuf[cls, pl.ds(qstart, 2 * BAND), cols], q2) + bias[which]

    def softmax(slot):
        tops = []
        for gi in range(groups):
            top = sc_buf[slot, gi, 0:8, :]
            for c in range(0, 2 * BAND, chunk):
                sc = sc_buf[slot, gi, c:c + chunk, :]
                for r8 in range(0, chunk, 8):
                    top = jnp.maximum(top, sc[r8:r8 + 8, :])
            m = jnp.max(top, axis=0, keepdims=True)
            for c in range(0, 2 * BAND, chunk):
                e = jnp.exp2(sc_buf[slot, gi, c:c + chunk, :] - m)
                p_buf[gi, c:c + chunk, :] = e.astype(BF16)
            tops.append(m)
        return tops

    ones_rows = jnp.ones((16, 2 * BAND), BF16)

    def finish(cls, qstart, tops):
        qs = pl.ds(qstart, BAND)
        ks = pl.ds(qstart, 2 * BAND)
        if has_prev:
            prev_lse = ps_ref[0, cls, qs, :].T
        stats = jnp.zeros((8, BAND), F32)
        for gi, c0 in enumerate(range(0, ATT_WIDTH, LANES)):
            cols = slice(c0, c0 + LANES)
            m = tops[gi]
            lhs = jnp.concatenate([vbuf[cls, ks, cols].T, ones_rows], axis=0)
            ot = _dot(lhs, p_buf[gi])
            l = ot[LANES:LANES + 1, :]
            inv = 1.0 / l
            lse = m + jnp.log2(l)
            if has_prev:
                prev_o = po_ref[0, cls, qs, cols].astype(F32).T
            parts = []
            for hh in range(LANES // half):
                head = 2 * gi + hh
                qcols = slice(hh * BAND, (hh + 1) * BAND)
                rows = slice(hh * half, (hh + 1) * half)
                w_new = inv[:, qcols]
                lse_h = lse[:, qcols]
                if has_prev:
                    lp = prev_lse[head:head + 1, :]
                    top = jnp.maximum(lp, lse_h)
                    wp = jnp.exp2(lp - top)
                    wn = jnp.exp2(lse_h - top)
                    den = wp + wn
                    o = (wp / den) * prev_o[rows, :] + (wn * w_new / den) * ot[rows, qcols]
                    lse_h = top + jnp.log2(den)
                else:
                    o = ot[rows, qcols] * w_new
                if is_last:
                    ms = jnp.mean(o * o, axis=0, keepdims=True)
                    o = o * lax.rsqrt(ms + RMS_EPS)
                parts.append(o)
                stats = jnp.where(row8 == head, lse_h, stats)
            o = jnp.concatenate(parts, axis=0).T
            if is_last:
                o = o * g_ref[:, cols]
            if nc > 1:
                obuf[gi, pl.ds(qstart * nc + cls, BAND, stride=nc), :] = o
            else:
                o_ref[0, 0, qs, cols] = o.astype(o_ref.dtype)
        if not is_last:
            st = jnp.concatenate([stats, jnp.zeros((BAND - 8, BAND), F32)], axis=0).T
            if nc > 1:
                sbuf[pl.ds(qstart * nc + cls, BAND, stride=nc), :] = st
            else:
                s_ref[0, 0, qs, :] = st

    blocks = [(cls, n * BAND) for cls in range(nc) for n in range(nq)]
    which = lambda blk: 0 if blk[1] == 0 else 1
    scores(*blocks[0], 0, which(blocks[0]))
    for i, (cls, qstart) in enumerate(blocks):
        if i + 1 < len(blocks):
            scores(*blocks[i + 1], (i + 1) % 2, which(blocks[i + 1]))
        finish(cls, qstart, softmax(i % 2))
    if nc > 1:
        for gi, c0 in enumerate(range(0, ATT_WIDTH, LANES)):
            o_ref[0, 0, :, c0:c0 + LANES] = obuf[gi].astype(o_ref.dtype)
        if not is_last:
            s_ref[0, 0] = sbuf[...]


def _attention(qkv, gn_g):
    w = ATT_WIDTH
    order = sorted(range(len(DILATED_PATTERNS)), key=lambda p: -DILATED_PATTERNS[p][1])
    prev = None
    for step, p in enumerate(order):
        window, r = DILATED_PATTERNS[p]
        batch, _, length, _ = qkv[p].shape
        is_last = step == len(order) - 1
        r_next = 1 if is_last else DILATED_PATTERNS[order[step + 1]][1]
        nc = r // r_next
        assert window // r == BAND and r == nc * r_next
        nq = 2 if nc > 1 else 8
        seq_tiles = length // (nq * BAND)
        assert length % (nq * BAND) == 0
        rows = nq * BAND

        if nc > 1:
            shape5 = lambda t: t.reshape(batch, nc, r_next, length, t.shape[-1])
            main = lambda wd: pl.BlockSpec((1, nc, None, rows, wd), lambda b, t: (b, 0, t // seq_tiles, t % seq_tiles, 0))
            halo = lambda wd: pl.BlockSpec(
                (1, nc, None, BAND, wd),
                lambda b, t: (b, 0, t // seq_tiles, jnp.maximum(t % seq_tiles * nq - 1, 0), 0))
            grid = (batch, r_next * seq_tiles)
            out_blk = lambda wd: pl.BlockSpec((1, 1, nc * rows, wd), lambda b, t: (b, t // seq_tiles, t % seq_tiles, 0))
        else:
            shape5 = lambda t: t
            main = lambda wd: pl.BlockSpec((1, 1, rows, wd), lambda b, t: (b, 0, t, 0))
            halo = lambda wd: pl.BlockSpec((1, 1, BAND, wd), lambda b, t: (b, 0, jnp.maximum(t * nq - 1, 0), 0))
            grid = (batch, seq_tiles)
            out_blk = lambda wd: pl.BlockSpec((1, 1, rows, wd), lambda b, t: (b, 0, t, 0))
        if nc > 1 and seq_tiles > 1:
            assert r_next == 1

        args = [shape5(qkv[p])]
        in_specs = [main(3 * w)]
        if seq_tiles > 1:
            args.append(shape5(qkv[p]))
            in_specs.append(halo(3 * w))
        if prev is not None:
            args += [shape5(prev[0]), shape5(prev[1])]
            in_specs += [main(w), main(LANES)]
        if is_last:
            args.append(gn_g)
            in_specs.append(_resident((1, w)))
        o_shape = jax.ShapeDtypeStruct((batch, r_next, length * nc, w), BF16)
        s_shape = jax.ShapeDtypeStruct((batch, r_next, length * nc, LANES), F32)
        groups = w // LANES
        scratch = [pltpu.VMEM((nc, rows + BAND, w), BF16)] * 2 + [
            pltpu.VMEM((2, groups, 2 * BAND, 2 * BAND), F32),
            pltpu.VMEM((groups, 2 * BAND, 2 * BAND), BF16),
            pltpu.VMEM((2, 2 * BAND, 2 * BAND), F32)]
        if nc > 1:
            scratch += [pltpu.VMEM((w // LANES, nc * rows, LANES), F32), pltpu.VMEM((nc * rows, LANES), F32)]
        res = pl.pallas_call(
            functools.partial(_attn_kernel, nc=nc, nq=nq, seq_tiles=seq_tiles,
                              has_prev=prev is not None, is_last=is_last),
            out_shape=o_shape if is_last else (o_shape, s_shape),
            grid=grid,
            in_specs=in_specs,
            out_specs=out_blk(w) if is_last else (out_blk(w), out_blk(LANES)),
            scratch_shapes=scratch,
            compiler_params=_params("parallel", "arbitrary"),
            name=f"attn_r{r}",
        )(*args)
        if is_last:
            return res.reshape(batch, length * nc, w)
        prev = res


def _retention_kernel(in_ref, dmask_ref, xi_ref, zeta_ref, cdec_ref, g_ref,
                      o_ref, state_ref, raw_ref, upd_ref, pre_ref):
    @pl.when(pl.program_id(1) == 0)
    def _():
        state_ref[...] = jnp.zeros_like(state_ref)

    chunk = RET_CHUNK

    def where(t, h):
        return slice(t, t + chunk), slice(h * RET_HEAD_DIM, (h + 1) * RET_HEAD_DIM)

    def stream(j, t, h):
        c0 = j * RET_WIDTH + h * RET_HEAD_DIM
        return in_ref[0, t:t + chunk, c0:c0 + RET_HEAD_DIM]

    def scores(t, h, slot):
        k = stream(1, t, h)
        raw_ref[slot] = _dot_nt(stream(0, t, h), k)
        kz = (k.astype(F32) * zeta_ref[h]).astype(BF16)
        upd_ref[slot] = _dot_tn(kz, stream(2, t, h))

    def mix(t, h, slot):
        inner = (raw_ref[slot] * dmask_ref[h]).astype(BF16)
        state = state_ref[h]
        pre_ref[slot] = (_dot(inner, stream(2, t, h))
                         + _dot(stream(0, t, h), state.astype(BF16)) * xi_ref[h])
        state_ref[h] = state * cdec_ref[h] + upd_ref[slot]

    def finish(t, h, slot):
        rows, c = where(t, h)
        o = pre_ref[slot]
        mu = jnp.mean(o, axis=-1, keepdims=True)
        var = jnp.mean(jnp.square(o - mu), axis=-1, keepdims=True)
        y = (o - mu) * lax.rsqrt(var + GN_EPS) * g_ref[:, c]
        gate = stream(3, t, h).astype(F32)
        o_ref[0, rows, c] = (gate * (1.0 / (1.0 + jnp.exp(-gate))) * y).astype(o_ref.dtype)

    units = [(t, h) for t in range(0, in_ref.shape[1], chunk) for h in range(RET_HEADS)]
    scores(*units[0], 0)
    for i, unit in enumerate(units):
        if i + 1 < len(units):
            scores(*units[i + 1], (i + 1) % 2)
        mix(*unit, i % 2)
        if i > 0:
            finish(*units[i - 1], (i - 1) % 2)
    finish(*units[-1], (len(units) - 1) % 2)


def _retention_tables():
    c = RET_CHUNK
    log_g = jnp.log(1.0 - 2.0 ** (-5.0 - jnp.arange(RET_HEADS, dtype=F32)))
    n = jnp.arange(c, dtype=F32)
    rel = n[:, None] - n[None, :]
    dmask = jnp.where(rel >= 0, jnp.exp(log_g[:, None, None] * jnp.maximum(rel, 0.0)), 0.0)
    xi = jnp.exp(log_g[:, None] * (n + 1.0))
    zeta = jnp.exp(log_g[:, None] * (c - 1.0 - n))
    cdec = jnp.exp(log_g * c)
    wide = lambda t: jnp.broadcast_to(t[:, :, None], (RET_HEADS, c, RET_HEAD_DIM))
    cdec = jnp.broadcast_to(cdec[:, None, None], (RET_HEADS, RET_HEAD_DIM, RET_HEAD_DIM))
    return dmask, wide(xi), wide(zeta), cdec


def _retention(qkvg, gn_g, chunks_per_step=8):
    batch, seq, _ = qkvg.shape
    width = RET_WIDTH
    c = RET_CHUNK
    rows = c * chunks_per_step
    blk = lambda wd: pl.BlockSpec((1, rows, wd), lambda b, n: (b, n, 0))
    tab = _resident((RET_HEADS, c, RET_HEAD_DIM))
    return pl.pallas_call(
        _retention_kernel,
        out_shape=jax.ShapeDtypeStruct((batch, seq, width), BF16),
        grid=(batch, seq // rows),
        in_specs=[blk(4 * width)] + [tab] * 4 + [_resident((1, width))],
        out_specs=blk(width),
        scratch_shapes=[pltpu.VMEM((RET_HEADS, RET_HEAD_DIM, RET_HEAD_DIM), F32),
                        pltpu.VMEM((2, c, c), F32),
                        pltpu.VMEM((2, RET_HEAD_DIM, RET_HEAD_DIM), F32),
                        pltpu.VMEM((2, c, RET_HEAD_DIM), F32)],
        compiler_params=_params("parallel", "arbitrary"),
        name="retention",
    )(qkvg, *_retention_tables(), gn_g)


def _mem_kv_kernel(mem_ref, g_ref, w_ref, k_ref, v_ref):
    d = mem_ref.shape[-1]
    kv = _dot(_rms(mem_ref[0], g_ref[...]).astype(BF16), w_ref[...])
    k_ref[0] = kv[:, :d].astype(BF16)
    v_ref[0] = kv[:, d:].astype(BF16)


def _mem_kv(mem, g, w_kv):
    batch, n_mem, d = mem.shape
    blk = pl.BlockSpec((1, n_mem, d), lambda b: (b, 0, 0))
    out = jax.ShapeDtypeStruct((batch, n_mem, d), BF16)
    return pl.pallas_call(
        _mem_kv_kernel,
        out_shape=(out, out),
        grid=(batch,),
        in_specs=[blk, _resident((1, d)), _resident((d, 2 * d))],
        out_specs=(blk, blk),
        compiler_params=_params("parallel"),
        name="mem_kv",
    )(mem, g, w_kv)


def _mix_out_kernel(a_ref, r_ref, x_ref, wout_ref, gpost_ref, gpre_ref, wq_ref, mk_ref, mv_ref,
                    wo_ref, gpost2_ref, o_ref, x1_ref, h_ref, q_ref, att_ref, *, sub):
    half = a_ref.shape[-1]
    d = x_ref.shape[-1]
    dh = d // XATT_HEADS

    def out_proj(rows):
        y = _dot(a_ref[0, rows, :], wout_ref[:half, :]) + _dot(r_ref[0, rows, :], wout_ref[half:, :])
        x1 = x_ref[0, rows, :] + _rms(y, gpost_ref[...])
        x1_ref[rows, :] = x1
        h_ref[rows, :] = _rms(x1, gpre_ref[...]).astype(BF16)

    def q_proj(rows):
        q_ref[rows, :] = (_dot(h_ref[rows, :], wq_ref[...]) * (dh ** -0.5)).astype(BF16)

    def cross_attn(rows):
        for hd in range(XATT_HEADS):
            c = slice(hd * dh, (hd + 1) * dh)
            s = _dot_nt(q_ref[rows, c], mk_ref[0, :, c])
            p = jnp.exp(s - jnp.max(s, axis=-1, keepdims=True))
            l = jnp.sum(p, axis=-1, keepdims=True)
            att_ref[rows, c] = (_dot(p.astype(BF16), mv_ref[0, :, c]) / l).astype(BF16)

    def o_proj(rows):
        y2 = _dot(att_ref[rows, :], wo_ref[...])
        o_ref[0, rows, :] = x1_ref[rows, :] + _rms(y2, gpost2_ref[...])

    tiles = [slice(r0, r0 + sub) for r0 in range(0, x_ref.shape[1], sub)]
    for stage in (out_proj, q_proj, cross_attn, o_proj):
        for rows in tiles:
            stage(rows)


def _mix_out(a, r, x, w_out, g_post, g_pre, w_q, mem_k, mem_v, w_o, g_post2, tm=1024, sub=256):
    batch, seq, d = x.shape
    half = a.shape[-1]
    n_mem = mem_k.shape[1]
    tok = lambda w: pl.BlockSpec((1, tm, w), lambda b, t: (b, t, 0))
    memblk = pl.BlockSpec((1, n_mem, d), lambda b, t: (b, 0, 0))
    vec = _resident((1, d))
    mat = _resident((d, d))
    return pl.pallas_call(
        functools.partial(_mix_out_kernel, sub=sub),
        out_shape=jax.ShapeDtypeStruct((batch, seq, d), F32),
        grid=(batch, seq // tm),
        in_specs=[tok(half), tok(half), tok(d), mat, vec, vec, mat, memblk, memblk, mat, vec],
        out_specs=tok(d),
        scratch_shapes=[pltpu.VMEM((tm, d), F32),
                        pltpu.VMEM((tm, d), BF16),
                        pltpu.VMEM((tm, d), BF16),
                        pltpu.VMEM((tm, d), BF16)],
        compiler_params=_params("parallel", "parallel"),
        name="mix_out",
    )(a, r, x, w_out, g_post, g_pre, w_q, mem_k, mem_v, w_o, g_post2)


def _ffn_kernel(x_ref, gpre_ref, wgu_ref, wdown_ref, gpost_ref, o_ref, h_ref, y_ref, *, chunks, sub):
    hidden = wdown_ref.shape[0]

    def norm(rows):
        h_ref[rows, :] = _rms(x_ref[rows, :], gpre_ref[...]).astype(BF16)

    def hidden_chunk(ci, rows):
        lo, hi = chunks[ci]
        h = h_ref[rows, :]
        g = _dot(h, wgu_ref[:, lo:hi])
        u = _dot(h, wgu_ref[:, hidden + lo:hidden + hi])
        act = (g * (1.0 / (1.0 + jnp.exp(-g))) * u).astype(BF16)
        part = _dot(act, wdown_ref[lo:hi, :])
        if ci == 0:
            y_ref[rows, :] = part
        else:
            y_ref[rows, :] += part

    def final(rows):
        o_ref[rows, :] = x_ref[rows, :] + _rms(y_ref[rows, :], gpost_ref[...])

    tiles = [slice(r0, r0 + sub) for r0 in range(0, x_ref.shape[0], sub)]
    for rows in tiles:
        norm(rows)
    for ci in range(len(chunks)):
        for rows in tiles:
            hidden_chunk(ci, rows)
    for rows in tiles:
        final(rows)


def _ffn(x2d, g_pre, w_gu, w_down, g_post, tm=1024, sub=256):
    tokens, d = x2d.shape
    hidden = w_down.shape[0]
    mxu = 256
    cut = (hidden // 2) // mxu * mxu
    chunks = ((0, cut), (cut, hidden))
    row = pl.BlockSpec((tm, d), lambda t: (t, 0))
    vec = _resident((1, d))
    return pl.pallas_call(
        functools.partial(_ffn_kernel, chunks=chunks, sub=sub),
        out_shape=jax.ShapeDtypeStruct((tokens, d), F32),
        grid=(tokens // tm,),
        in_specs=[row, vec, _resident((d, 2 * hidden)), _resident((hidden, d)), vec],
        out_specs=row,
        scratch_shapes=[pltpu.VMEM((tm, d), BF16),
                        pltpu.VMEM((tm, d), F32)],
        compiler_params=_params("parallel"),
        name="ffn",
    )(x2d, g_pre, w_gu, w_down, g_post)


def kernel(x, mem, pre_mix_g, post_mix_g, w_in, attn_gn_g, ret_gn_g, w_out,
           pre_mem_g, post_mem_g, mem_norm_g, w_q_mem, w_kv_mem, w_o_mem,
           pre_ffn_g, post_ffn_g, w_gate_up, w_down):
    batch, seq, d = x.shape
    depth = w_in.shape[0]
    for l in range(depth):
        bf = lambda w: w[l].astype(BF16)
        vec = lambda g: g[l][None, :]
        qkv, ret_in = _in_proj(x, vec(pre_mix_g), bf(w_in))
        a = _attention(qkv, vec(attn_gn_g))
        r = _retention(ret_in, vec(ret_gn_g))
        mem_k, mem_v = _mem_kv(mem, vec(mem_norm_g), bf(w_kv_mem))
        x = _mix_out(a, r, x, bf(w_out), vec(post_mix_g), vec(pre_mem_g), bf(w_q_mem),
                     mem_k, mem_v, bf(w_o_mem), vec(post_mem_g))
        x = _ffn(x.reshape(batch * seq, d), vec(pre_ffn_g), bf(w_gate_up), bf(w_down),
                 vec(post_ffn_g)).reshape(batch, seq, d)
    return x
```

```python
import functools

import numpy as np
import jax
import jax.numpy as jnp
from jax import lax
from jax.experimental import pallas as pl
from jax.experimental.pallas import tpu as pltpu

F32 = jnp.float32
BF16 = jnp.bfloat16

RMS_EPS = 1e-6
GN_EPS = 1e-5
ATT_HEADS = 8
ATT_HEAD_DIM = 64
ATT_WIDTH = ATT_HEADS * ATT_HEAD_DIM
ROT_DIM = ATT_HEAD_DIM // 4
ROPE_THETA = 500000.0
DILATED_PATTERNS = ((128, 1), (512, 4), (2048, 16))
RET_HEADS = 4
RET_HEAD_DIM = 128
RET_WIDTH = RET_HEADS * RET_HEAD_DIM
RET_CHUNK = 128
RET_ROPE_THETA = 10000.0
XATT_HEADS = 4

LANES = 128
BAND = 128
MASKED = -1e30
LOG2_E = 1.4426950408889634
VMEM_LIMIT = 56 * 1024 * 1024


def _resident(shape):
    zeros = (0,) * len(shape)
    return pl.BlockSpec(shape, lambda *_: zeros, pipeline_mode=pl.Buffered(1))


def _params(*sem):
    return pltpu.CompilerParams(dimension_semantics=sem, vmem_limit_bytes=VMEM_LIMIT)


def _cast_once(grid_rank, pairs, cols=512):
    first = pl.program_id(0) == 0
    for axis in range(1, grid_rank):
        first = jnp.logical_and(first, pl.program_id(axis) == 0)

    @pl.when(first)
    def _():
        for w_ref, wb_ref in pairs:
            for c in range(0, w_ref.shape[1], cols):
                wb_ref[:, c:c + cols] = w_ref[:, c:c + cols].astype(BF16)


def _rms(x, g):
    return x * lax.rsqrt(jnp.mean(x * x, axis=-1, keepdims=True) + RMS_EPS) * g


def _dot(a, b):
    return jnp.dot(a, b, preferred_element_type=F32)


def _dot_nt(a, b):
    return lax.dot_general(a, b, (((1,), (1,)), ((), ())), preferred_element_type=F32)


def _dot_tn(a, b):
    return lax.dot_general(a, b, (((0,), (0,)), ((), ())), preferred_element_type=F32)


def _in_proj_kernel(x_ref, g_ref, w_ref, ca_ref, sa1_ref, sa2_ref, cr_ref, sr_ref, *refs):
    att_refs = refs[:len(DILATED_PATTERNS)]
    ret_ref, h_ref, acc_ref, scr_ref, scr2_ref, wb_ref = refs[len(att_refs):]
    tm = x_ref.shape[1]
    width = ATT_WIDTH
    n_att = 3
    (_, r_one), (_, r_mid), (_, r_top) = DILATED_PATTERNS
    assert r_one == 1 and r_top % r_mid == 0
    rows_per_step = 64
    _cast_once(2, [(w_ref, wb_ref)])
    h_ref[...] = _rms(x_ref[0], g_ref[...]).astype(BF16)

    def project(i, slot):
        acc_ref[slot] = _dot(h_ref[...], wb_ref[:, i * width:(i + 1) * width])

    def rope_attn(t, rows):
        return (t * ca_ref[rows, :]
                + pltpu.roll(t, LANES - ROT_DIM // 2, 1) * sa1_ref[rows, :]
                + pltpu.roll(t, ROT_DIM // 2, 1) * sa2_ref[rows, :])

    def rope_ret(t, rows):
        return t * cr_ref[rows, :] + pltpu.roll(t, RET_HEAD_DIM // 2, 1) * sr_ref[rows, :]

    def post(i, slot):
        for gi, c0 in enumerate(range(0, width, LANES)):
            cols = slice(c0, c0 + LANES)
            for r0 in range(0, tm, rows_per_step):
                rows = slice(r0, r0 + rows_per_step)
                t = acc_ref[slot, rows, cols]
                if i == 0:
                    t = rope_attn(t, rows) * (ATT_HEAD_DIM ** -0.5 * LOG2_E)
                elif i == 1:
                    t = rope_attn(t, rows)
                elif i == n_att:
                    t = rope_ret(t, rows)
                elif i == n_att + 1:
                    t = rope_ret(t, rows) * (RET_HEAD_DIM ** -0.5)
                if i >= n_att:
                    ocols = slice((i - n_att) * width + c0, (i - n_att) * width + c0 + LANES)
                    ret_ref[0, rows, ocols] = t.astype(BF16)
                else:
                    ocols = slice(i * width + c0, i * width + c0 + LANES)
                    scr_ref[i, gi, rows, :] = t
                    att_refs[0][0, 0, rows, ocols] = t.astype(BF16)
            if i < n_att:
                for c4 in range(r_mid):
                    picked = scr_ref[i, gi, pl.ds(c4, tm // r_mid, stride=r_mid), :]
                    att_refs[1][0, c4, :, ocols] = picked.astype(BF16)
                    scr2_ref[i, gi, c4] = picked
                for c4 in range(r_mid):
                    for m in range(r_top // r_mid):
                        picked = scr2_ref[i, gi, c4, pl.ds(m, tm // r_top, stride=r_top // r_mid), :]
                        att_refs[2][0, m * r_mid + c4, :, ocols] = picked.astype(BF16)

    n_streams = n_att + 4
    project(0, 0)
    for i in range(n_streams):
        if i + 1 < n_streams:
            project(i + 1, (i + 1) % 2)
        post(i, i % 2)


def _rope_tables(seq):
    f32 = np.float32
    pos = np.arange(seq, dtype=f32)[:, None]
    half = ROT_DIM // 2
    inv = f32(ROPE_THETA) ** (-(np.arange(0, ROT_DIM, 2, dtype=f32) / f32(ROT_DIM)))
    ang = pos * inv[None, :]
    cos, sin = np.cos(ang), np.sin(ang)
    one = np.ones((seq, ATT_HEAD_DIM - ROT_DIM), f32)
    zero = np.zeros((seq, ATT_HEAD_DIM - ROT_DIM), f32)
    zh = np.zeros((seq, half), f32)
    ca = np.concatenate([cos, cos, one], axis=1)
    sa1 = np.concatenate([-sin, zh, zero], axis=1)
    sa2 = np.concatenate([zh, sin, zero], axis=1)
    reps = LANES // ATT_HEAD_DIM
    ca, sa1, sa2 = (np.tile(t, (1, reps)) for t in (ca, sa1, sa2))
    inv_r = f32(RET_ROPE_THETA) ** (-(np.arange(0, RET_HEAD_DIM, 2, dtype=f32) / f32(RET_HEAD_DIM)))
    ang_r = pos * inv_r[None, :]
    cr = np.concatenate([np.cos(ang_r), np.cos(ang_r)], axis=1)
    sr = np.concatenate([-np.sin(ang_r), np.sin(ang_r)], axis=1)
    return tuple(t.astype(f32) for t in (ca, sa1, sa2, cr, sr))


def _in_proj(x, g, w_in, tm=512):
    batch, seq, d = x.shape
    width = w_in.shape[1]
    tok = lambda w: pl.BlockSpec((1, tm, w), lambda b, t: (b, t, 0))
    tab = pl.BlockSpec((tm, LANES), lambda b, t: (t, 0))
    att_shapes, att_specs = [], []
    for _, r in DILATED_PATTERNS:
        assert tm % (r * 16) == 0
        att_shapes.append(jax.ShapeDtypeStruct((batch, r, seq // r, 3 * ATT_WIDTH), BF16))
        att_specs.append(pl.BlockSpec((1, r, tm // r, 3 * ATT_WIDTH), lambda b, t: (b, 0, t, 0)))
    ret_shape = jax.ShapeDtypeStruct((batch, seq, 4 * RET_WIDTH), BF16)
    outs = pl.pallas_call(
        _in_proj_kernel,
        out_shape=tuple(att_shapes) + (ret_shape,),
        grid=(batch, seq // tm),
        in_specs=[tok(d), _resident((1, d)), _resident((d, width))] + [tab] * 5,
        out_specs=tuple(att_specs) + (tok(4 * RET_WIDTH),),
        scratch_shapes=[pltpu.VMEM((tm, d), BF16),
                        pltpu.VMEM((2, tm, ATT_WIDTH), F32),
                        pltpu.VMEM((3, ATT_WIDTH // LANES, tm, LANES), F32),
                        pltpu.VMEM((3, ATT_WIDTH // LANES, DILATED_PATTERNS[1][1],
                                    tm // DILATED_PATTERNS[1][1], LANES), F32),
                        pltpu.VMEM((d, width), BF16)],
        compiler_params=_params("arbitrary", "arbitrary"),
        name="in_proj",
    )(x, g, w_in, *_rope_tables(seq))
    return outs[:len(att_shapes)], outs[len(att_shapes)]


def _attn_kernel(*refs, nc, nq, seq_tiles, has_prev, is_last):
    refs = list(refs)
    main_ref = refs.pop(0)
    if seq_tiles > 1:
        halo_ref = refs.pop(0)
    if has_prev:
        po_ref, ps_ref = refs[:2]
        del refs[:2]
    if is_last:
        g_ref = refs.pop(0)
    o_ref = refs.pop(0)
    if not is_last:
        s_ref = refs.pop(0)
    kbuf, vbuf, sc_buf, p_buf, bias = refs[:5]
    del refs[:5]
    if nc > 1:
        obuf, sbuf = refs

    half = ATT_HEAD_DIM
    groups = ATT_WIDTH // LANES
    chunk = 32
    kcols = slice(ATT_WIDTH, 2 * ATT_WIDTH)
    vcols = slice(2 * ATT_WIDTH, 3 * ATT_WIDTH)
    for cls in range(nc):
        if seq_tiles > 1:
            kbuf[cls, :BAND] = halo_ref[0, cls, :, kcols]
            vbuf[cls, :BAND] = halo_ref[0, cls, :, vcols]
        else:
            kbuf[cls, :BAND] = jnp.zeros((BAND, ATT_WIDTH), BF16)
            vbuf[cls, :BAND] = jnp.zeros((BAND, ATT_WIDTH), BF16)
        kbuf[cls, BAND:] = main_ref[0, cls, :, kcols]
        vbuf[cls, BAND:] = main_ref[0, cls, :, vcols]

    key = lax.broadcasted_iota(jnp.int32, (2 * BAND, 2 * BAND), 0)
    qry = lax.broadcasted_iota(jnp.int32, (2 * BAND, 2 * BAND), 1) & (BAND - 1)
    if seq_tiles == 1:
        floor = BAND
    else:
        floor = jnp.where(pl.program_id(1) == 0, BAND, 0)
    upper = key <= qry + BAND
    bias[0] = jnp.where((key >= jnp.maximum(qry, floor)) & upper, 0.0, MASKED)
    bias[1] = jnp.where((key >= qry) & upper, 0.0, MASKED)

    lane = lax.broadcasted_iota(jnp.int32, (BAND, LANES), 1)
    lo = lane < half
    row8 = lax.broadcasted_iota(jnp.int32, (8, BAND), 0)

    def scores(cls, qstart, slot, which):
        for gi, c0 in enumerate(range(0, ATT_WIDTH, LANES)):
            cols = slice(c0, c0 + LANES)
            qp = main_ref[0, cls, pl.ds(qstart, BAND), cols]
            zero = jnp.zeros_like(qp)
            q2 = jnp.concatenate([jnp.where(lo, qp, zero), jnp.where(lo, zero, qp)], axis=0)
            sc_buf[slot, gi] = _dot_nt(kbuf[cls, pl.ds(qstart, 2 * BAND), cols], q2) + bias[which]

    def softmax(slot):
        tops = []
        for gi in range(groups):
            top = sc_buf[slot, gi, 0:8, :]
            for c in range(0, 2 * BAND, chunk):
                sc = sc_buf[slot, gi, c:c + chunk, :]
                for r8 in range(0, chunk, 8):
                    top = jnp.maximum(top, sc[r8:r8 + 8, :])
            m = jnp.max(top, axis=0, keepdims=True)
            for c in range(0, 2 * BAND, chunk):
                e = jnp.exp2(sc_buf[slot, gi, c:c + chunk, :] - m)
                p_buf[gi, c:c + chunk, :] = e.astype(BF16)
            tops.append(m)
        return tops

    ones_rows = jnp.ones((16, 2 * BAND), BF16)

    def finish(cls, qstart, tops):
        qs = pl.ds(qstart, BAND)
        ks = pl.ds(qstart, 2 * BAND)
        if has_prev:
            prev_lse = ps_ref[0, cls, qs, :].T
        stats = jnp.zeros((8, BAND), F32)
        for gi, c0 in enumerate(range(0, ATT_WIDTH, LANES)):
            cols = slice(c0, c0 + LANES)
            m = tops[gi]
            lhs = jnp.concatenate([vbuf[cls, ks, cols].T, ones_rows], axis=0)
            ot = _dot(lhs, p_buf[gi])
            l = ot[LANES:LANES + 1, :]
            inv = 1.0 / l
            lse = m + jnp.log2(l)
            if has_prev:
                prev_o = po_ref[0, cls, qs, cols].astype(F32).T
            parts = []
            for hh in range(LANES // half):
                head = 2 * gi + hh
                qcols = slice(hh * BAND, (hh + 1) * BAND)
                rows = slice(hh * half, (hh + 1) * half)
                w_new = inv[:, qcols]
                lse_h = lse[:, qcols]
                if has_prev:
                    lp = prev_lse[head:head + 1, :]
                    top = jnp.maximum(lp, lse_h)
                    wp = jnp.exp2(lp - top)
                    wn = jnp.exp2(lse_h - top)
                    den = wp + wn
                    o = (wp / den) * prev_o[rows, :] + (wn * w_new / den) * ot[rows, qcols]
                    lse_h = top + jnp.log2(den)
                else:
                    o = ot[rows, qcols] * w_new
                if is_last:
                    ms = jnp.mean(o * o, axis=0, keepdims=True)
                    o = o * lax.rsqrt(ms + RMS_EPS)
                parts.append(o)
                stats = jnp.where(row8 == head, lse_h, stats)
            o = jnp.concatenate(parts, axis=0).T
            if is_last:
                o = o * g_ref[:, cols]
            if nc > 1:
                obuf[gi, pl.ds(qstart * nc + cls, BAND, stride=nc), :] = o
            else:
                o_ref[0, 0, qs, cols] = o.astype(o_ref.dtype)
        if not is_last:
            st = jnp.concatenate([stats, jnp.zeros((BAND - 8, BAND), F32)], axis=0).T
            if nc > 1:
                sbuf[pl.ds(qstart * nc + cls, BAND, stride=nc), :] = st
            else:
                s_ref[0, 0, qs, :] = st

    blocks = [(cls, n * BAND) for cls in range(nc) for n in range(nq)]
    which = lambda blk: 0 if blk[1] == 0 else 1
    scores(*blocks[0], 0, which(blocks[0]))
    for i, (cls, qstart) in enumerate(blocks):
        if i + 1 < len(blocks):
            scores(*blocks[i + 1], (i + 1) % 2, which(blocks[i + 1]))
        finish(cls, qstart, softmax(i % 2))
    if nc > 1:
        for gi, c0 in enumerate(range(0, ATT_WIDTH, LANES)):
            o_ref[0, 0, :, c0:c0 + LANES] = obuf[gi].astype(o_ref.dtype)
        if not is_last:
            s_ref[0, 0] = sbuf[...]


def _attention(qkv, gn_g):
    w = ATT_WIDTH
    order = sorted(range(len(DILATED_PATTERNS)), key=lambda p: -DILATED_PATTERNS[p][1])
    prev = None
    for step, p in enumerate(order):
        window, r = DILATED_PATTERNS[p]
        batch, _, length, _ = qkv[p].shape
        is_last = step == len(order) - 1
        r_next = 1 if is_last else DILATED_PATTERNS[order[step + 1]][1]
        nc = r // r_next
        assert window // r == BAND and r == nc * r_next
        nq = 2 if nc > 1 else 8
        seq_tiles = length // (nq * BAND)
        assert length % (nq * BAND) == 0
        rows = nq * BAND

        if nc > 1:
            shape5 = lambda t: t.reshape(batch, nc, r_next, length, t.shape[-1])
            main = lambda wd: pl.BlockSpec((1, nc, None, rows, wd), lambda b, t: (b, 0, t // seq_tiles, t % seq_tiles, 0))
            halo = lambda wd: pl.BlockSpec(
                (1, nc, None, BAND, wd),
                lambda b, t: (b, 0, t // seq_tiles, jnp.maximum(t % seq_tiles * nq - 1, 0), 0))
            grid = (batch, r_next * seq_tiles)
            out_blk = lambda wd: pl.BlockSpec((1, 1, nc * rows, wd), lambda b, t: (b, t // seq_tiles, t % seq_tiles, 0))
        else:
            shape5 = lambda t: t
            main = lambda wd: pl.BlockSpec((1, 1, rows, wd), lambda b, t: (b, 0, t, 0))
            halo = lambda wd: pl.BlockSpec((1, 1, BAND, wd), lambda b, t: (b, 0, jnp.maximum(t * nq - 1, 0), 0))
            grid = (batch, seq_tiles)
            out_blk = lambda wd: pl.BlockSpec((1, 1, rows, wd), lambda b, t: (b, 0, t, 0))
        if nc > 1 and seq_tiles > 1:
            assert r_next == 1

        args = [shape5(qkv[p])]
        in_specs = [main(3 * w)]
        if seq_tiles > 1:
            args.append(shape5(qkv[p]))
            in_specs.append(halo(3 * w))
        if prev is not None:
            args += [shape5(prev[0]), shape5(prev[1])]
            in_specs += [main(w), main(LANES)]
        if is_last:
            args.append(gn_g)
            in_specs.append(_resident((1, w)))
        o_shape = jax.ShapeDtypeStruct((batch, r_next, length * nc, w), BF16)
        s_shape = jax.ShapeDtypeStruct((batch, r_next, length * nc, LANES), F32)
        groups = w // LANES
        scratch = [pltpu.VMEM((nc, rows + BAND, w), BF16)] * 2 + [
            pltpu.VMEM((2, groups, 2 * BAND, 2 * BAND), F32),
            pltpu.VMEM((groups, 2 * BAND, 2 * BAND), BF16),
            pltpu.VMEM((2, 2 * BAND, 2 * BAND), F32)]
        if nc > 1:
            scratch += [pltpu.VMEM((w // LANES, nc * rows, LANES), F32), pltpu.VMEM((nc * rows, LANES), F32)]
        res = pl.pallas_call(
            functools.partial(_attn_kernel, nc=nc, nq=nq, seq_tiles=seq_tiles,
                              has_prev=prev is not None, is_last=is_last),
            out_shape=o_shape if is_last else (o_shape, s_shape),
            grid=grid,
            in_specs=in_specs,
            out_specs=out_blk(w) if is_last else (out_blk(w), out_blk(LANES)),
            scratch_shapes=scratch,
            compiler_params=_params("parallel", "arbitrary"),
            name=f"attn_r{r}",
        )(*args)
        if is_last:
            return res.reshape(batch, length * nc, w)
        prev = res


def _retention_kernel(in_ref, dmask_ref, xi_ref, zeta_ref, cdec_ref, g_ref,
                      o_ref, state_ref, raw_ref, upd_ref, pre_ref):
    @pl.when(pl.program_id(1) == 0)
    def _():
        state_ref[...] = jnp.zeros_like(state_ref)

    chunk = RET_CHUNK

    def where(t, h):
        return slice(t, t + chunk), slice(h * RET_HEAD_DIM, (h + 1) * RET_HEAD_DIM)

    def stream(j, t, h):
        c0 = j * RET_WIDTH + h * RET_HEAD_DIM
        return in_ref[0, t:t + chunk, c0:c0 + RET_HEAD_DIM]

    def scores(t, h, slot):
        k = stream(1, t, h)
        raw_ref[slot] = _dot_nt(stream(0, t, h), k)
        kz = (k.astype(F32) * zeta_ref[h]).astype(BF16)
        upd_ref[slot] = _dot_tn(kz, stream(2, t, h))

    def mix(t, h, slot):
        inner = (raw_ref[slot] * dmask_ref[h]).astype(BF16)
        state = state_ref[h]
        pre_ref[slot] = (_dot(inner, stream(2, t, h))
                         + _dot(stream(0, t, h), state.astype(BF16)) * xi_ref[h])
        state_ref[h] = state * cdec_ref[h] + upd_ref[slot]

    def finish(t, h, slot):
        rows, c = where(t, h)
        o = pre_ref[slot]
        mu = jnp.mean(o, axis=-1, keepdims=True)
        var = jnp.mean(jnp.square(o - mu), axis=-1, keepdims=True)
        y = (o - mu) * lax.rsqrt(var + GN_EPS) * g_ref[:, c]
        gate = stream(3, t, h).astype(F32)
        o_ref[0, rows, c] = (gate * (1.0 / (1.0 + jnp.exp(-gate))) * y).astype(o_ref.dtype)

    units = [(t, h) for t in range(0, in_ref.shape[1], chunk) for h in range(RET_HEADS)]
    scores(*units[0], 0)
    for i, unit in enumerate(units):
        if i + 1 < len(units):
            scores(*units[i + 1], (i + 1) % 2)
        mix(*unit, i % 2)
        if i > 0:
            finish(*units[i - 1], (i - 1) % 2)
    finish(*units[-1], (len(units) - 1) % 2)


def _retention_tables():
    f32 = np.float32
    c = RET_CHUNK
    log_g = np.log(f32(1.0) - f32(2.0) ** (f32(-5.0) - np.arange(RET_HEADS, dtype=f32))).astype(f32)
    n = np.arange(c, dtype=f32)
    rel = n[:, None] - n[None, :]
    dmask = np.where(rel >= 0, np.exp(log_g[:, None, None] * np.maximum(rel, f32(0.0))), f32(0.0))
    xi = np.exp(log_g[:, None] * (n + f32(1.0)))
    zeta = np.exp(log_g[:, None] * (f32(c - 1.0) - n))
    cdec = np.exp(log_g * f32(c))
    wide = lambda t: np.ascontiguousarray(np.broadcast_to(t[:, :, None], (RET_HEADS, c, RET_HEAD_DIM)), f32)
    cdec = np.ascontiguousarray(np.broadcast_to(cdec[:, None, None], (RET_HEADS, RET_HEAD_DIM, RET_HEAD_DIM)), f32)
    return dmask.astype(f32), wide(xi), wide(zeta), cdec


def _retention(qkvg, gn_g, chunks_per_step=4):
    batch, seq, _ = qkvg.shape
    width = RET_WIDTH
    c = RET_CHUNK
    rows = c * chunks_per_step
    blk = lambda wd: pl.BlockSpec((1, rows, wd), lambda b, n: (b, n, 0))
    tab = _resident((RET_HEADS, c, RET_HEAD_DIM))
    return pl.pallas_call(
        _retention_kernel,
        out_shape=jax.ShapeDtypeStruct((batch, seq, width), BF16),
        grid=(batch, seq // rows),
        in_specs=[blk(4 * width)] + [tab] * 4 + [_resident((1, width))],
        out_specs=blk(width),
        scratch_shapes=[pltpu.VMEM((RET_HEADS, RET_HEAD_DIM, RET_HEAD_DIM), F32),
                        pltpu.VMEM((2, c, c), F32),
                        pltpu.VMEM((2, RET_HEAD_DIM, RET_HEAD_DIM), F32),
                        pltpu.VMEM((2, c, RET_HEAD_DIM), F32)],
        compiler_params=_params("parallel", "arbitrary"),
        name="retention",
    )(qkvg, *_retention_tables(), gn_g)


def _mem_kv_kernel(mem_ref, g_ref, w32_ref, k_ref, v_ref, w_ref):
    d = mem_ref.shape[-1]
    _cast_once(1, [(w32_ref, w_ref)])
    kv = _dot(_rms(mem_ref[0], g_ref[...]).astype(BF16), w_ref[...])
    k_ref[0] = kv[:, :d].astype(BF16)
    v_ref[0] = kv[:, d:].astype(BF16)


def _mem_kv(mem, g, w_kv):
    batch, n_mem, d = mem.shape
    blk = pl.BlockSpec((1, n_mem, d), lambda b: (b, 0, 0))
    out = jax.ShapeDtypeStruct((batch, n_mem, d), BF16)
    return pl.pallas_call(
        _mem_kv_kernel,
        out_shape=(out, out),
        grid=(batch,),
        in_specs=[blk, _resident((1, d)), _resident((d, 2 * d))],
        out_specs=(blk, blk),
        scratch_shapes=[pltpu.VMEM((d, 2 * d), BF16)],
        compiler_params=_params("arbitrary"),
        name="mem_kv",
    )(mem, g, w_kv)


def _mix_out_kernel(a_ref, r_ref, x_ref, wout32_ref, gpost_ref, gpre_ref, wq32_ref, mk_ref, mv_ref,
                    wo32_ref, gpost2_ref, o_ref, x1_ref, h_ref, q_ref, att_ref,
                    wout_ref, wq_ref, wo_ref, *, sub):
    half = a_ref.shape[-1]
    d = x_ref.shape[-1]
    dh = d // XATT_HEADS
    _cast_once(2, [(wout32_ref, wout_ref), (wq32_ref, wq_ref), (wo32_ref, wo_ref)])

    def out_proj(rows):
        y = _dot(a_ref[0, rows, :], wout_ref[:half, :]) + _dot(r_ref[0, rows, :], wout_ref[half:, :])
        x1 = x_ref[0, rows, :] + _rms(y, gpost_ref[...])
        x1_ref[rows, :] = x1
        h_ref[rows, :] = _rms(x1, gpre_ref[...]).astype(BF16)

    def q_proj(rows):
        q_ref[rows, :] = (_dot(h_ref[rows, :], wq_ref[...]) * (dh ** -0.5)).astype(BF16)

    def cross_attn(rows):
        for hd in range(XATT_HEADS):
            c = slice(hd * dh, (hd + 1) * dh)
            s = _dot_nt(q_ref[rows, c], mk_ref[0, :, c])
            p = jnp.exp(s - jnp.max(s, axis=-1, keepdims=True))
            l = jnp.sum(p, axis=-1, keepdims=True)
            att_ref[rows, c] = (_dot(p.astype(BF16), mv_ref[0, :, c]) / l).astype(BF16)

    def o_proj(rows):
        y2 = _dot(att_ref[rows, :], wo_ref[...])
        o_ref[0, rows, :] = x1_ref[rows, :] + _rms(y2, gpost2_ref[...])

    tiles = [slice(r0, r0 + sub) for r0 in range(0, x_ref.shape[1], sub)]
    for stage in (out_proj, q_proj, cross_attn, o_proj):
        for rows in tiles:
            stage(rows)


def _mix_out(a, r, x, w_out, g_post, g_pre, w_q, mem_k, mem_v, w_o, g_post2, tm=1024, sub=256):
    batch, seq, d = x.shape
    half = a.shape[-1]
    n_mem = mem_k.shape[1]
    tok = lambda w: pl.BlockSpec((1, tm, w), lambda b, t: (b, t, 0))
    memblk = pl.BlockSpec((1, n_mem, d), lambda b, t: (b, 0, 0))
    vec = _resident((1, d))
    mat = _resident((d, d))
    return pl.pallas_call(
        functools.partial(_mix_out_kernel, sub=sub),
        out_shape=jax.ShapeDtypeStruct((batch, seq, d), F32),
        grid=(batch, seq // tm),
        in_specs=[tok(half), tok(half), tok(d), mat, vec, vec, mat, memblk, memblk, mat, vec],
        out_specs=tok(d),
        scratch_shapes=[pltpu.VMEM((tm, d), F32),
                        pltpu.VMEM((tm, d), BF16),
                        pltpu.VMEM((tm, d), BF16),
                        pltpu.VMEM((tm, d), BF16)]
        + [pltpu.VMEM((d, d), BF16)] * 3,
        compiler_params=_params("arbitrary", "arbitrary"),
        name="mix_out",
    )(a, r, x, w_out, g_post, g_pre, w_q, mem_k, mem_v, w_o, g_post2)


def _ffn_kernel(x_ref, gpre_ref, wgu_ref, wdown_ref, gpost_ref, o_ref, h_ref, y_ref, *, chunks, sub):
    hidden = wdown_ref.shape[0]

    def norm(rows):
        h_ref[rows, :] = _rms(x_ref[rows, :], gpre_ref[...]).astype(BF16)

    def hidden_chunk(ci, rows):
        lo, hi = chunks[ci]
        h = h_ref[rows, :]
        g = _dot(h, wgu_ref[:, lo:hi])
        u = _dot(h, wgu_ref[:, hidden + lo:hidden + hi])
        act = (g * (1.0 / (1.0 + jnp.exp(-g))) * u).astype(BF16)
        part = _dot(act, wdown_ref[lo:hi, :])
        if ci == 0:
            y_ref[rows, :] = part
        else:
            y_ref[rows, :] += part

    def final(rows):
        o_ref[rows, :] = x_ref[rows, :] + _rms(y_ref[rows, :], gpost_ref[...])

    tiles = [slice(r0, r0 + sub) for r0 in range(0, x_ref.shape[0], sub)]
    for rows in tiles:
        norm(rows)
    for ci in range(len(chunks)):
        for rows in tiles:
            hidden_chunk(ci, rows)
    for rows in tiles:
        final(rows)


def _ffn(x2d, g_pre, w_gu, w_down, g_post, tm=1024, sub=256):
    tokens, d = x2d.shape
    hidden = w_down.shape[0]
    mxu = 256
    cut = (hidden // 2) // mxu * mxu
    chunks = ((0, cut), (cut, hidden))
    row = pl.BlockSpec((tm, d), lambda t: (t, 0))
    vec = _resident((1, d))
    return pl.pallas_call(
        functools.partial(_ffn_kernel, chunks=chunks, sub=sub),
        out_shape=jax.ShapeDtypeStruct((tokens, d), F32),
        grid=(tokens // tm,),
        in_specs=[row, vec, _resident((d, 2 * hidden)), _resident((hidden, d)), vec],
        out_specs=row,
        scratch_shapes=[pltpu.VMEM((tm, d), BF16),
                        pltpu.VMEM((tm, d), F32)],
        compiler_params=_params("parallel"),
        name="ffn",
    )(x2d, g_pre, w_gu, w_down, g_post)


def kernel(x, mem, pre_mix_g, post_mix_g, w_in, attn_gn_g, ret_gn_g, w_out,
           pre_mem_g, post_mem_g, mem_norm_g, w_q_mem, w_kv_mem, w_o_mem,
           pre_ffn_g, post_ffn_g, w_gate_up, w_down):
    batch, seq, d = x.shape
    depth = w_in.shape[0]
    for l in range(depth):
        bf = lambda w: w[l].astype(BF16)
        vec = lambda g: g[l][None, :]
        qkv, ret_in = _in_proj(x, vec(pre_mix_g), w_in[l])
        a = _attention(qkv, vec(attn_gn_g))
        r = _retention(ret_in, vec(ret_gn_g))
        mem_k, mem_v = _mem_kv(mem, vec(mem_norm_g), w_kv_mem[l])
        x = _mix_out(a, r, x, w_out[l], vec(post_mix_g), vec(pre_mem_g), w_q_mem[l],
                     mem_k, mem_v, w_o_mem[l], vec(post_mem_g))
        x = _ffn(x.reshape(batch * seq, d), vec(pre_ffn_g), bf(w_gate_up), bf(w_down),
                 vec(post_ffn_g)).reshape(batch, seq, d)
    return x
```

```python
import functools

import numpy as np
import jax
import jax.numpy as jnp
from jax import lax
from jax.experimental import pallas as pl
from jax.experimental.pallas import tpu as pltpu

F32 = jnp.float32
BF16 = jnp.bfloat16

RMS_EPS = 1e-6
GN_EPS = 1e-5
ATT_HEADS = 8
ATT_HEAD_DIM = 64
ATT_WIDTH = ATT_HEADS * ATT_HEAD_DIM
ROT_DIM = ATT_HEAD_DIM // 4
ROPE_THETA = 500000.0
DILATED_PATTERNS = ((128, 1), (512, 4), (2048, 16))
RET_HEADS = 4
RET_HEAD_DIM = 128
RET_WIDTH = RET_HEADS * RET_HEAD_DIM
RET_CHUNK = 128
RET_ROPE_THETA = 10000.0
XATT_HEADS = 4

LANES = 128
BAND = 128
MASKED = -1e30
LOG2_E = 1.4426950408889634
VMEM_LIMIT = 56 * 1024 * 1024


def _resident(shape):
    zeros = (0,) * len(shape)
    return pl.BlockSpec(shape, lambda *_: zeros, pipeline_mode=pl.Buffered(1))


def _params(*sem):
    return pltpu.CompilerParams(dimension_semantics=sem, vmem_limit_bytes=VMEM_LIMIT)


def _cast_once(grid_rank, pairs, cols=512):
    first = pl.program_id(0) == 0
    for axis in range(1, grid_rank):
        first = jnp.logical_and(first, pl.program_id(axis) == 0)

    @pl.when(first)
    def _():
        for w_ref, wb_ref in pairs:
            for c in range(0, w_ref.shape[1], cols):
                wb_ref[:, c:c + cols] = w_ref[:, c:c + cols].astype(BF16)


def _rms(x, g):
    return x * lax.rsqrt(jnp.mean(x * x, axis=-1, keepdims=True) + RMS_EPS) * g


def _dot(a, b):
    return jnp.dot(a, b, preferred_element_type=F32)


def _dot_nt(a, b):
    return lax.dot_general(a, b, (((1,), (1,)), ((), ())), preferred_element_type=F32)


def _dot_tn(a, b):
    return lax.dot_general(a, b, (((0,), (0,)), ((), ())), preferred_element_type=F32)


def _in_proj_kernel(x_ref, g_ref, w_ref, ca_ref, sa1_ref, sa2_ref, cr_ref, sr_ref, *refs):
    att_refs = refs[:len(DILATED_PATTERNS)]
    ret_ref, h_ref, acc_ref, scr_ref, scr2_ref, wb_ref = refs[len(att_refs):]
    tm = x_ref.shape[1]
    width = ATT_WIDTH
    n_att = 3
    (_, r_one), (_, r_mid), (_, r_top) = DILATED_PATTERNS
    assert r_one == 1 and r_top % r_mid == 0
    rows_per_step = 64
    _cast_once(2, [(w_ref, wb_ref)])
    h_ref[...] = _rms(x_ref[0], g_ref[...]).astype(BF16)

    def project(i, slot):
        acc_ref[slot] = _dot(h_ref[...], wb_ref[:, i * width:(i + 1) * width])

    def rope_attn(t, rows):
        return (t * ca_ref[rows, :]
                + pltpu.roll(t, LANES - ROT_DIM // 2, 1) * sa1_ref[rows, :]
                + pltpu.roll(t, ROT_DIM // 2, 1) * sa2_ref[rows, :])

    def rope_ret(t, rows):
        return t * cr_ref[rows, :] + pltpu.roll(t, RET_HEAD_DIM // 2, 1) * sr_ref[rows, :]

    def post(i, slot):
        for gi, c0 in enumerate(range(0, width, LANES)):
            cols = slice(c0, c0 + LANES)
            for r0 in range(0, tm, rows_per_step):
                rows = slice(r0, r0 + rows_per_step)
                t = acc_ref[slot, rows, cols]
                if i == 0:
                    t = rope_attn(t, rows) * (ATT_HEAD_DIM ** -0.5 * LOG2_E)
                elif i == 1:
                    t = rope_attn(t, rows)
                elif i == n_att:
                    t = rope_ret(t, rows)
                elif i == n_att + 1:
                    t = rope_ret(t, rows) * (RET_HEAD_DIM ** -0.5)
                if i >= n_att:
                    ocols = slice((i - n_att) * width + c0, (i - n_att) * width + c0 + LANES)
                    ret_ref[0, rows, ocols] = t.astype(BF16)
                else:
                    ocols = slice(i * width + c0, i * width + c0 + LANES)
                    scr_ref[i, gi, rows, :] = t
                    att_refs[0][0, 0, rows, ocols] = t.astype(BF16)
            if i < n_att:
                for c4 in range(r_mid):
                    picked = scr_ref[i, gi, pl.ds(c4, tm // r_mid, stride=r_mid), :]
                    att_refs[1][0, c4, :, ocols] = picked.astype(BF16)
                    scr2_ref[i, gi, c4] = picked
                for c4 in range(r_mid):
                    for m in range(r_top // r_mid):
                        picked = scr2_ref[i, gi, c4, pl.ds(m, tm // r_top, stride=r_top // r_mid), :]
                        att_refs[2][0, m * r_mid + c4, :, ocols] = picked.astype(BF16)

    n_streams = n_att + 4
    project(0, 0)
    for i in range(n_streams):
        if i + 1 < n_streams:
            project(i + 1, (i + 1) % 2)
        post(i, i % 2)


def _rope_tables(seq):
    f32 = np.float32
    pos = np.arange(seq, dtype=f32)[:, None]
    half = ROT_DIM // 2
    inv = f32(ROPE_THETA) ** (-(np.arange(0, ROT_DIM, 2, dtype=f32) / f32(ROT_DIM)))
    ang = pos * inv[None, :]
    cos, sin = np.cos(ang), np.sin(ang)
    one = np.ones((seq, ATT_HEAD_DIM - ROT_DIM), f32)
    zero = np.zeros((seq, ATT_HEAD_DIM - ROT_DIM), f32)
    zh = np.zeros((seq, half), f32)
    ca = np.concatenate([cos, cos, one], axis=1)
    sa1 = np.concatenate([-sin, zh, zero], axis=1)
    sa2 = np.concatenate([zh, sin, zero], axis=1)
    reps = LANES // ATT_HEAD_DIM
    ca, sa1, sa2 = (np.tile(t, (1, reps)) for t in (ca, sa1, sa2))
    inv_r = f32(RET_ROPE_THETA) ** (-(np.arange(0, RET_HEAD_DIM, 2, dtype=f32) / f32(RET_HEAD_DIM)))
    ang_r = pos * inv_r[None, :]
    cr = np.concatenate([np.cos(ang_r), np.cos(ang_r)], axis=1)
    sr = np.concatenate([-np.sin(ang_r), np.sin(ang_r)], axis=1)
    return tuple(t.astype(f32) for t in (ca, sa1, sa2, cr, sr))


def _in_proj(x, g, w_in, tm=512):
    batch, seq, d = x.shape
    width = w_in.shape[1]
    tok = lambda w: pl.BlockSpec((1, tm, w), lambda b, t: (b, t, 0))
    tab = pl.BlockSpec((tm, LANES), lambda b, t: (t, 0))
    att_shapes, att_specs = [], []
    for _, r in DILATED_PATTERNS:
        assert tm % (r * 16) == 0
        att_shapes.append(jax.ShapeDtypeStruct((batch, r, seq // r, 3 * ATT_WIDTH), BF16))
        att_specs.append(pl.BlockSpec((1, r, tm // r, 3 * ATT_WIDTH), lambda b, t: (b, 0, t, 0)))
    ret_shape = jax.ShapeDtypeStruct((batch, seq, 4 * RET_WIDTH), BF16)
    outs = pl.pallas_call(
        _in_proj_kernel,
        out_shape=tuple(att_shapes) + (ret_shape,),
        grid=(batch, seq // tm),
        in_specs=[tok(d), _resident((1, d)), _resident((d, width))] + [tab] * 5,
        out_specs=tuple(att_specs) + (tok(4 * RET_WIDTH),),
        scratch_shapes=[pltpu.VMEM((tm, d), BF16),
                        pltpu.VMEM((2, tm, ATT_WIDTH), F32),
                        pltpu.VMEM((3, ATT_WIDTH // LANES, tm, LANES), F32),
                        pltpu.VMEM((3, ATT_WIDTH // LANES, DILATED_PATTERNS[1][1],
                                    tm // DILATED_PATTERNS[1][1], LANES), F32),
                        pltpu.VMEM((d, width), BF16)],
        compiler_params=_params("arbitrary", "arbitrary"),
        name="in_proj",
    )(x, g, w_in, *_rope_tables(seq))
    return outs[:len(att_shapes)], outs[len(att_shapes)]


def _attn_kernel(*refs, nc, nq, seq_tiles, has_prev, is_last):
    refs = list(refs)
    main_ref = refs.pop(0)
    if seq_tiles > 1:
        halo_ref = refs.pop(0)
    if has_prev:
        po_ref, ps_ref = refs[:2]
        del refs[:2]
    if is_last:
        g_ref = refs.pop(0)
    o_ref = refs.pop(0)
    if not is_last:
        s_ref = refs.pop(0)
    kbuf, vbuf, sc_buf, p_buf, bias = refs[:5]
    del refs[:5]
    if nc > 1:
        obuf, sbuf = refs

    half = ATT_HEAD_DIM
    groups = ATT_WIDTH // LANES
    chunk = 32
    kcols = slice(ATT_WIDTH, 2 * ATT_WIDTH)
    vcols = slice(2 * ATT_WIDTH, 3 * ATT_WIDTH)
    for cls in range(nc):
        if seq_tiles > 1:
            kbuf[cls, :BAND] = halo_ref[0, cls, :, kcols]
            vbuf[cls, :BAND] = halo_ref[0, cls, :, vcols]
        else:
            kbuf[cls, :BAND] = jnp.zeros((BAND, ATT_WIDTH), BF16)
            vbuf[cls, :BAND] = jnp.zeros((BAND, ATT_WIDTH), BF16)
        kbuf[cls, BAND:] = main_ref[0, cls, :, kcols]
        vbuf[cls, BAND:] = main_ref[0, cls, :, vcols]

    key = lax.broadcasted_iota(jnp.int32, (2 * BAND, 2 * BAND), 0)
    qry = lax.broadcasted_iota(jnp.int32, (2 * BAND, 2 * BAND), 1) & (BAND - 1)
    if seq_tiles == 1:
        floor = BAND
    else:
        floor = jnp.where(pl.program_id(1) == 0, BAND, 0)
    upper = key <= qry + BAND
    bias[0] = jnp.where((key >= jnp.maximum(qry, floor)) & upper, 0.0, MASKED)
    bias[1] = jnp.where((key >= qry) & upper, 0.0, MASKED)

    lane = lax.broadcasted_iota(jnp.int32, (BAND, LANES), 1)
    lo = lane < half
    row8 = lax.broadcasted_iota(jnp.int32, (8, BAND), 0)

    def scores(cls, qstart, slot, which):
        for gi, c0 in enumerate(range(0, ATT_WIDTH, LANES)):
            cols = slice(c0, c0 + LANES)
            qp = main_ref[0, cls, pl.ds(qstart, BAND), cols]
            zero = jnp.zeros_like(qp)
            q2 = jnp.concatenate([jnp.where(lo, qp, zero), jnp.where(lo, zero, qp)], axis=0)
            sc_buf[slot, gi] = _dot_nt(kbuf[cls, pl.ds(qstart, 2 * BAND), cols], q2) + bias[which]

    def softmax(slot):
        tops = []
        for gi in range(groups):
            top = sc_buf[slot, gi, 0:8, :]
            for c in range(0, 2 * BAND, chunk):
                sc = sc_buf[slot, gi, c:c + chunk, :]
                for r8 in range(0, chunk, 8):
                    top = jnp.maximum(top, sc[r8:r8 + 8, :])
            m = jnp.max(top, axis=0, keepdims=True)
            for c in range(0, 2 * BAND, chunk):
                e = jnp.exp2(sc_buf[slot, gi, c:c + chunk, :] - m)
                p_buf[gi, c:c + chunk, :] = e.astype(BF16)
            tops.append(m)
        return tops

    ones_rows = jnp.ones((16, 2 * BAND), BF16)

    def finish(cls, qstart, tops):
        qs = pl.ds(qstart, BAND)
        ks = pl.ds(qstart, 2 * BAND)
        if has_prev:
            prev_lse = ps_ref[0, cls, qs, :].T
        stats = jnp.zeros((8, BAND), F32)
        for gi, c0 in enumerate(range(0, ATT_WIDTH, LANES)):
            cols = slice(c0, c0 + LANES)
            m = tops[gi]
            lhs = jnp.concatenate([vbuf[cls, ks, cols].T, ones_rows], axis=0)
            ot = _dot(lhs, p_buf[gi])
            l = ot[LANES:LANES + 1, :]
            inv = 1.0 / l
            lse = m + jnp.log2(l)
            if has_prev:
                prev_o = po_ref[0, cls, qs, cols].astype(F32).T
            parts = []
            for hh in range(LANES // half):
                head = 2 * gi + hh
                qcols = slice(hh * BAND, (hh + 1) * BAND)
                rows = slice(hh * half, (hh + 1) * half)
                w_new = inv[:, qcols]
                lse_h = lse[:, qcols]
                if has_prev:
                    lp = prev_lse[head:head + 1, :]
                    top = jnp.maximum(lp, lse_h)
                    wp = jnp.exp2(lp - top)
                    wn = jnp.exp2(lse_h - top)
                    den = wp + wn
                    o = (wp / den) * prev_o[rows, :] + (wn * w_new / den) * ot[rows, qcols]
                    lse_h = top + jnp.log2(den)
                else:
                    o = ot[rows, qcols] * w_new
                if is_last:
                    ms = jnp.mean(o * o, axis=0, keepdims=True)
                    o = o * lax.rsqrt(ms + RMS_EPS)
                parts.append(o)
                stats = jnp.where(row8 == head, lse_h, stats)
            o = jnp.concatenate(parts, axis=0).T
            if is_last:
                o = o * g_ref[:, cols]
            if nc > 1:
                obuf[gi, pl.ds(qstart * nc + cls, BAND, stride=nc), :] = o
            else:
                o_ref[0, 0, qs, cols] = o.astype(o_ref.dtype)
        if not is_last:
            st = jnp.concatenate([stats, jnp.zeros((BAND - 8, BAND), F32)], axis=0).T
            if nc > 1:
                sbuf[pl.ds(qstart * nc + cls, BAND, stride=nc), :] = st
            else:
                s_ref[0, 0, qs, :] = st

    blocks = [(cls, n * BAND) for cls in range(nc) for n in range(nq)]
    which = lambda blk: 0 if blk[1] == 0 else 1
    scores(*blocks[0], 0, which(blocks[0]))
    for i, (cls, qstart) in enumerate(blocks):
        if i + 1 < len(blocks):
            scores(*blocks[i + 1], (i + 1) % 2, which(blocks[i + 1]))
        finish(cls, qstart, softmax(i % 2))
    if nc > 1:
        for gi, c0 in enumerate(range(0, ATT_WIDTH, LANES)):
            o_ref[0, 0, :, c0:c0 + LANES] = obuf[gi].astype(o_ref.dtype)
        if not is_last:
            s_ref[0, 0] = sbuf[...]


def _attention(qkv, gn_g):
    w = ATT_WIDTH
    order = sorted(range(len(DILATED_PATTERNS)), key=lambda p: -DILATED_PATTERNS[p][1])
    prev = None
    for step, p in enumerate(order):
        window, r = DILATED_PATTERNS[p]
        batch, _, length, _ = qkv[p].shape
        is_last = step == len(order) - 1
        r_next = 1 if is_last else DILATED_PATTERNS[order[step + 1]][1]
        nc = r // r_next
        assert window // r == BAND and r == nc * r_next
        nq = 2 if nc > 1 else 8
        seq_tiles = length // (nq * BAND)
        assert length % (nq * BAND) == 0
        rows = nq * BAND

        if nc > 1:
            shape5 = lambda t: t.reshape(batch, nc, r_next, length, t.shape[-1])
            main = lambda wd: pl.BlockSpec((1, nc, None, rows, wd), lambda b, t: (b, 0, t // seq_tiles, t % seq_tiles, 0))
            halo = lambda wd: pl.BlockSpec(
                (1, nc, None, BAND, wd),
                lambda b, t: (b, 0, t // seq_tiles, jnp.maximum(t % seq_tiles * nq - 1, 0), 0))
            grid = (batch, r_next * seq_tiles)
            out_blk = lambda wd: pl.BlockSpec((1, 1, nc * rows, wd), lambda b, t: (b, t // seq_tiles, t % seq_tiles, 0))
        else:
            shape5 = lambda t: t
            main = lambda wd: pl.BlockSpec((1, 1, rows, wd), lambda b, t: (b, 0, t, 0))
            halo = lambda wd: pl.BlockSpec((1, 1, BAND, wd), lambda b, t: (b, 0, jnp.maximum(t * nq - 1, 0), 0))
            grid = (batch, seq_tiles)
            out_blk = lambda wd: pl.BlockSpec((1, 1, rows, wd), lambda b, t: (b, 0, t, 0))
        if nc > 1 and seq_tiles > 1:
            assert r_next == 1

        args = [shape5(qkv[p])]
        in_specs = [main(3 * w)]
        if seq_tiles > 1:
            args.append(shape5(qkv[p]))
            in_specs.append(halo(3 * w))
        if prev is not None:
            args += [shape5(prev[0]), shape5(prev[1])]
            in_specs += [main(w), main(LANES)]
        if is_last:
            args.append(gn_g)
            in_specs.append(_resident((1, w)))
        o_shape = jax.ShapeDtypeStruct((batch, r_next, length * nc, w), BF16)
        s_shape = jax.ShapeDtypeStruct((batch, r_next, length * nc, LANES), F32)
        groups = w // LANES
        scratch = [pltpu.VMEM((nc, rows + BAND, w), BF16)] * 2 + [
            pltpu.VMEM((2, groups, 2 * BAND, 2 * BAND), F32),
            pltpu.VMEM((groups, 2 * BAND, 2 * BAND), BF16),
            pltpu.VMEM((2, 2 * BAND, 2 * BAND), F32)]
        if nc > 1:
            scratch += [pltpu.VMEM((w // LANES, nc * rows, LANES), F32), pltpu.VMEM((nc * rows, LANES), F32)]
        res = pl.pallas_call(
            functools.partial(_attn_kernel, nc=nc, nq=nq, seq_tiles=seq_tiles,
                              has_prev=prev is not None, is_last=is_last),
            out_shape=o_shape if is_last else (o_shape, s_shape),
            grid=grid,
            in_specs=in_specs,
            out_specs=out_blk(w) if is_last else (out_blk(w), out_blk(LANES)),
            scratch_shapes=scratch,
            compiler_params=_params("parallel", "arbitrary"),
            name=f"attn_r{r}",
        )(*args)
        if is_last:
            return res.reshape(batch, length * nc, w)
        prev = res


def _retention_kernel(in_ref, dmask_ref, xi_ref, zeta_ref, cdec_ref, g_ref, *refs, n_cast):
    cast_in = refs[:n_cast]
    o_ref = refs[n_cast]
    cast_out = refs[n_cast + 1:2 * n_cast + 1]
    state_ref, raw_ref, upd_ref, pre_ref = refs[2 * n_cast + 1:]

    @pl.when(pl.program_id(1) == 0)
    def _():
        state_ref[...] = jnp.zeros_like(state_ref)

    for src, dst in zip(cast_in, cast_out):
        dst[...] = src[...].astype(BF16)

    chunk = RET_CHUNK

    def where(t, h):
        return slice(t, t + chunk), slice(h * RET_HEAD_DIM, (h + 1) * RET_HEAD_DIM)

    def stream(j, t, h):
        c0 = j * RET_WIDTH + h * RET_HEAD_DIM
        return in_ref[0, t:t + chunk, c0:c0 + RET_HEAD_DIM]

    def scores(t, h, slot):
        k = stream(1, t, h)
        raw_ref[slot] = _dot_nt(stream(0, t, h), k)
        kz = (k.astype(F32) * zeta_ref[h]).astype(BF16)
        upd_ref[slot] = _dot_tn(kz, stream(2, t, h))

    def mix(t, h, slot):
        inner = (raw_ref[slot] * dmask_ref[h]).astype(BF16)
        state = state_ref[h]
        pre_ref[slot] = (_dot(inner, stream(2, t, h))
                         + _dot(stream(0, t, h), state.astype(BF16)) * xi_ref[h])
        state_ref[h] = state * cdec_ref[h] + upd_ref[slot]

    def finish(t, h, slot):
        rows, c = where(t, h)
        o = pre_ref[slot]
        mu = jnp.mean(o, axis=-1, keepdims=True)
        var = jnp.mean(jnp.square(o - mu), axis=-1, keepdims=True)
        y = (o - mu) * lax.rsqrt(var + GN_EPS) * g_ref[:, c]
        gate = stream(3, t, h).astype(F32)
        o_ref[0, rows, c] = (gate * (1.0 / (1.0 + jnp.exp(-gate))) * y).astype(o_ref.dtype)

    units = [(t, h) for t in range(0, in_ref.shape[1], chunk) for h in range(RET_HEADS)]
    scores(*units[0], 0)
    for i, unit in enumerate(units):
        if i + 1 < len(units):
            scores(*units[i + 1], (i + 1) % 2)
        mix(*unit, i % 2)
        if i > 0:
            finish(*units[i - 1], (i - 1) % 2)
    finish(*units[-1], (len(units) - 1) % 2)


def _retention_tables():
    f32 = np.float32
    c = RET_CHUNK
    log_g = np.log(f32(1.0) - f32(2.0) ** (f32(-5.0) - np.arange(RET_HEADS, dtype=f32))).astype(f32)
    n = np.arange(c, dtype=f32)
    rel = n[:, None] - n[None, :]
    dmask = np.where(rel >= 0, np.exp(log_g[:, None, None] * np.maximum(rel, f32(0.0))), f32(0.0))
    xi = np.exp(log_g[:, None] * (n + f32(1.0)))
    zeta = np.exp(log_g[:, None] * (f32(c - 1.0) - n))
    cdec = np.exp(log_g * f32(c))
    wide = lambda t: np.ascontiguousarray(np.broadcast_to(t[:, :, None], (RET_HEADS, c, RET_HEAD_DIM)), f32)
    cdec = np.ascontiguousarray(np.broadcast_to(cdec[:, None, None], (RET_HEADS, RET_HEAD_DIM, RET_HEAD_DIM)), f32)
    return dmask.astype(f32), wide(xi), wide(zeta), cdec


def _retention(qkvg, gn_g, to_cast=(), chunks_per_step=4):
    batch, seq, _ = qkvg.shape
    width = RET_WIDTH
    c = RET_CHUNK
    rows = c * chunks_per_step
    steps_per_batch = seq // rows
    n_steps = batch * steps_per_batch
    blk = lambda wd: pl.BlockSpec((1, rows, wd), lambda b, n: (b, n, 0))
    tab = _resident((RET_HEADS, c, RET_HEAD_DIM))
    cast_specs, cast_shapes = [], []
    for w in to_cast:
        nb = max(k for k in range(1, n_steps + 1)
                 if n_steps % k == 0 and w.shape[0] % k == 0 and (w.shape[0] // k) % 16 == 0)
        hold = n_steps // nb
        cast_specs.append(pl.BlockSpec((w.shape[0] // nb, w.shape[1]),
                                       lambda b, n, hold=hold: ((b * steps_per_batch + n) // hold, 0)))
        cast_shapes.append(jax.ShapeDtypeStruct(w.shape, BF16))
    res = pl.pallas_call(
        functools.partial(_retention_kernel, n_cast=len(to_cast)),
        out_shape=(jax.ShapeDtypeStruct((batch, seq, width), BF16), *cast_shapes),
        grid=(batch, seq // rows),
        in_specs=[blk(4 * width)] + [tab] * 4 + [_resident((1, width))] + cast_specs,
        out_specs=(blk(width), *cast_specs),
        scratch_shapes=[pltpu.VMEM((RET_HEADS, RET_HEAD_DIM, RET_HEAD_DIM), F32),
                        pltpu.VMEM((2, c, c), F32),
                        pltpu.VMEM((2, RET_HEAD_DIM, RET_HEAD_DIM), F32),
                        pltpu.VMEM((2, c, RET_HEAD_DIM), F32)],
        compiler_params=_params("arbitrary", "arbitrary"),
        name="retention",
    )(qkvg, *_retention_tables(), gn_g, *to_cast)
    return res[0], res[1:]


def _mem_kv_kernel(mem_ref, g_ref, w32_ref, k_ref, v_ref, w_ref):
    d = mem_ref.shape[-1]
    _cast_once(1, [(w32_ref, w_ref)])
    kv = _dot(_rms(mem_ref[0], g_ref[...]).astype(BF16), w_ref[...])
    k_ref[0] = kv[:, :d].astype(BF16)
    v_ref[0] = kv[:, d:].astype(BF16)


def _mem_kv(mem, g, w_kv):
    batch, n_mem, d = mem.shape
    blk = pl.BlockSpec((1, n_mem, d), lambda b: (b, 0, 0))
    out = jax.ShapeDtypeStruct((batch, n_mem, d), BF16)
    return pl.pallas_call(
        _mem_kv_kernel,
        out_shape=(out, out),
        grid=(batch,),
        in_specs=[blk, _resident((1, d)), _resident((d, 2 * d))],
        out_specs=(blk, blk),
        scratch_shapes=[pltpu.VMEM((d, 2 * d), BF16)],
        compiler_params=_params("arbitrary"),
        name="mem_kv",
    )(mem, g, w_kv)


def _mix_out_kernel(a_ref, r_ref, x_ref, wout32_ref, gpost_ref, gpre_ref, wq32_ref, mk_ref, mv_ref,
                    wo32_ref, gpost2_ref, o_ref, x1_ref, h_ref, q_ref, att_ref,
                    wout_ref, wq_ref, wo_ref, *, sub):
    half = a_ref.shape[-1]
    d = x_ref.shape[-1]
    dh = d // XATT_HEADS
    _cast_once(2, [(wout32_ref, wout_ref), (wq32_ref, wq_ref), (wo32_ref, wo_ref)])

    def out_proj(rows):
        y = _dot(a_ref[0, rows, :], wout_ref[:half, :]) + _dot(r_ref[0, rows, :], wout_ref[half:, :])
        x1 = x_ref[0, rows, :] + _rms(y, gpost_ref[...])
        x1_ref[rows, :] = x1
        h_ref[rows, :] = _rms(x1, gpre_ref[...]).astype(BF16)

    def q_proj(rows):
        q_ref[rows, :] = (_dot(h_ref[rows, :], wq_ref[...]) * (dh ** -0.5)).astype(BF16)

    def cross_attn(rows):
        for hd in range(XATT_HEADS):
            c = slice(hd * dh, (hd + 1) * dh)
            s = _dot_nt(q_ref[rows, c], mk_ref[0, :, c])
            p = jnp.exp(s - jnp.max(s, axis=-1, keepdims=True))
            l = jnp.sum(p, axis=-1, keepdims=True)
            att_ref[rows, c] = (_dot(p.astype(BF16), mv_ref[0, :, c]) / l).astype(BF16)

    def o_proj(rows):
        y2 = _dot(att_ref[rows, :], wo_ref[...])
        o_ref[0, rows, :] = x1_ref[rows, :] + _rms(y2, gpost2_ref[...])

    tiles = [slice(r0, r0 + sub) for r0 in range(0, x_ref.shape[1], sub)]
    for stage in (out_proj, q_proj, cross_attn, o_proj):
        for rows in tiles:
            stage(rows)


def _mix_out(a, r, x, w_out, g_post, g_pre, w_q, mem_k, mem_v, w_o, g_post2, tm=1024, sub=256):
    batch, seq, d = x.shape
    half = a.shape[-1]
    n_mem = mem_k.shape[1]
    tok = lambda w: pl.BlockSpec((1, tm, w), lambda b, t: (b, t, 0))
    memblk = pl.BlockSpec((1, n_mem, d), lambda b, t: (b, 0, 0))
    vec = _resident((1, d))
    mat = _resident((d, d))
    return pl.pallas_call(
        functools.partial(_mix_out_kernel, sub=sub),
        out_shape=jax.ShapeDtypeStruct((batch, seq, d), F32),
        grid=(batch, seq // tm),
        in_specs=[tok(half), tok(half), tok(d), mat, vec, vec, mat, memblk, memblk, mat, vec],
        out_specs=tok(d),
        scratch_shapes=[pltpu.VMEM((tm, d), F32),
                        pltpu.VMEM((tm, d), BF16),
                        pltpu.VMEM((tm, d), BF16),
                        pltpu.VMEM((tm, d), BF16)]
        + [pltpu.VMEM((d, d), BF16)] * 3,
        compiler_params=_params("arbitrary", "arbitrary"),
        name="mix_out",
    )(a, r, x, w_out, g_post, g_pre, w_q, mem_k, mem_v, w_o, g_post2)


def _ffn_kernel(x_ref, gpre_ref, wgu_ref, wdown_ref, gpost_ref, o_ref, h_ref, y_ref, *, chunks, sub):
    hidden = wdown_ref.shape[0]

    def norm(rows):
        h_ref[rows, :] = _rms(x_ref[rows, :], gpre_ref[...]).astype(BF16)

    def hidden_chunk(ci, rows):
        lo, hi = chunks[ci]
        h = h_ref[rows, :]
        g = _dot(h, wgu_ref[:, lo:hi])
        u = _dot(h, wgu_ref[:, hidden + lo:hidden + hi])
        act = (g * (1.0 / (1.0 + jnp.exp(-g))) * u).astype(BF16)
        part = _dot(act, wdown_ref[lo:hi, :])
        if ci == 0:
            y_ref[rows, :] = part
        else:
            y_ref[rows, :] += part

    def final(rows):
        o_ref[rows, :] = x_ref[rows, :] + _rms(y_ref[rows, :], gpost_ref[...])

    tiles = [slice(r0, r0 + sub) for r0 in range(0, x_ref.shape[0], sub)]
    for rows in tiles:
        norm(rows)
    for ci in range(len(chunks)):
        for rows in tiles:
            hidden_chunk(ci, rows)
    for rows in tiles:
        final(rows)


def _ffn(x2d, g_pre, w_gu, w_down, g_post, tm=1024, sub=256):
    tokens, d = x2d.shape
    hidden = w_down.shape[0]
    mxu = 256
    cut = (hidden // 2) // mxu * mxu
    chunks = ((0, cut), (cut, hidden))
    row = pl.BlockSpec((tm, d), lambda t: (t, 0))
    vec = _resident((1, d))
    return pl.pallas_call(
        functools.partial(_ffn_kernel, chunks=chunks, sub=sub),
        out_shape=jax.ShapeDtypeStruct((tokens, d), F32),
        grid=(tokens // tm,),
        in_specs=[row, vec, _resident((d, 2 * hidden)), _resident((hidden, d)), vec],
        out_specs=row,
        scratch_shapes=[pltpu.VMEM((tm, d), BF16),
                        pltpu.VMEM((tm, d), F32)],
        compiler_params=_params("parallel"),
        name="ffn",
    )(x2d, g_pre, w_gu, w_down, g_post)


def kernel(x, mem, pre_mix_g, post_mix_g, w_in, attn_gn_g, ret_gn_g, w_out,
           pre_mem_g, post_mem_g, mem_norm_g, w_q_mem, w_kv_mem, w_o_mem,
           pre_ffn_g, post_ffn_g, w_gate_up, w_down):
    batch, seq, d = x.shape
    depth = w_in.shape[0]
    for l in range(depth):
        vec = lambda g: g[l][None, :]
        qkv, ret_in = _in_proj(x, vec(pre_mix_g), w_in[l])
        a = _attention(qkv, vec(attn_gn_g))
        r, (w_gu, w_dn) = _retention(ret_in, vec(ret_gn_g), to_cast=(w_gate_up[l], w_down[l]))
        mem_k, mem_v = _mem_kv(mem, vec(mem_norm_g), w_kv_mem[l])
        x = _mix_out(a, r, x, w_out[l], vec(post_mix_g), vec(pre_mem_g), w_q_mem[l],
                     mem_k, mem_v, w_o_mem[l], vec(post_mem_g))
        x = _ffn(x.reshape(batch * seq, d), vec(pre_ffn_g), w_gu, w_dn,
                 vec(post_ffn_g)).reshape(batch, seq, d)
    return x
```

```python
import functools

import numpy as np
import jax
import jax.numpy as jnp
from jax import lax
from jax.experimental import pallas as pl
from jax.experimental.pallas import tpu as pltpu

F32 = jnp.float32
BF16 = jnp.bfloat16

RMS_EPS = 1e-6
GN_EPS = 1e-5
ATT_HEADS = 8
ATT_HEAD_DIM = 64
ATT_WIDTH = ATT_HEADS * ATT_HEAD_DIM
ROT_DIM = ATT_HEAD_DIM // 4
ROPE_THETA = 500000.0
DILATED_PATTERNS = ((128, 1), (512, 4), (2048, 16))
RET_HEADS = 4
RET_HEAD_DIM = 128
RET_WIDTH = RET_HEADS * RET_HEAD_DIM
RET_CHUNK = 128
RET_ROPE_THETA = 10000.0
XATT_HEADS = 4

LANES = 128
BAND = 128
MASKED = -1e30
LOG2_E = 1.4426950408889634
VMEM_LIMIT = 56 * 1024 * 1024


def _resident(shape):
    zeros = (0,) * len(shape)
    return pl.BlockSpec(shape, lambda *_: zeros, pipeline_mode=pl.Buffered(1))


def _params(*sem):
    return pltpu.CompilerParams(dimension_semantics=sem, vmem_limit_bytes=VMEM_LIMIT)


def _cast_once(grid_rank, pairs, cols=512):
    first = pl.program_id(0) == 0
    for axis in range(1, grid_rank):
        first = jnp.logical_and(first, pl.program_id(axis) == 0)

    @pl.when(first)
    def _():
        for w_ref, wb_ref in pairs:
            for c in range(0, w_ref.shape[1], cols):
                wb_ref[:, c:c + cols] = w_ref[:, c:c + cols].astype(BF16)


def _rms(x, g):
    return x * lax.rsqrt(jnp.mean(x * x, axis=-1, keepdims=True) + RMS_EPS) * g


def _dot(a, b):
    return jnp.dot(a, b, preferred_element_type=F32)


def _dot_nt(a, b):
    return lax.dot_general(a, b, (((1,), (1,)), ((), ())), preferred_element_type=F32)


def _dot_tn(a, b):
    return lax.dot_general(a, b, (((0,), (0,)), ((), ())), preferred_element_type=F32)


def _in_proj_kernel(x_ref, g_ref, w_ref, ca_ref, sa1_ref, sa2_ref, cr_ref, sr_ref, *refs):
    att_refs = refs[:len(DILATED_PATTERNS)]
    ret_ref, h_ref, acc_ref, scr_ref, scr2_ref, wb_ref = refs[len(att_refs):]
    tm = x_ref.shape[1]
    width = ATT_WIDTH
    n_att = 3
    (_, r_one), (_, r_mid), (_, r_top) = DILATED_PATTERNS
    assert r_one == 1 and r_top % r_mid == 0
    rows_per_step = 64
    _cast_once(2, [(w_ref, wb_ref)])
    h_ref[...] = _rms(x_ref[0], g_ref[...]).astype(BF16)

    def project(i, slot):
        acc_ref[slot] = _dot(h_ref[...], wb_ref[:, i * width:(i + 1) * width])

    def rope_attn(t, rows):
        return (t * ca_ref[rows, :]
                + pltpu.roll(t, LANES - ROT_DIM // 2, 1) * sa1_ref[rows, :]
                + pltpu.roll(t, ROT_DIM // 2, 1) * sa2_ref[rows, :])

    def rope_ret(t, rows):
        return t * cr_ref[rows, :] + pltpu.roll(t, RET_HEAD_DIM // 2, 1) * sr_ref[rows, :]

    def post(i, slot):
        for gi, c0 in enumerate(range(0, width, LANES)):
            cols = slice(c0, c0 + LANES)
            for r0 in range(0, tm, rows_per_step):
                rows = slice(r0, r0 + rows_per_step)
                t = acc_ref[slot, rows, cols]
                if i == 0:
                    t = rope_attn(t, rows) * (ATT_HEAD_DIM ** -0.5 * LOG2_E)
                elif i == 1:
                    t = rope_attn(t, rows)
                elif i == n_att:
                    t = rope_ret(t, rows)
                elif i == n_att + 1:
                    t = rope_ret(t, rows) * (RET_HEAD_DIM ** -0.5)
                if i >= n_att:
                    ocols = slice((i - n_att) * width + c0, (i - n_att) * width + c0 + LANES)
                    ret_ref[0, rows, ocols] = t.astype(BF16)
                else:
                    ocols = slice(i * width + c0, i * width + c0 + LANES)
                    scr_ref[i, gi, rows, :] = t
                    att_refs[0][0, 0, rows, ocols] = t.astype(BF16)
            if i < n_att:
                for c4 in range(r_mid):
                    picked = scr_ref[i, gi, pl.ds(c4, tm // r_mid, stride=r_mid), :]
                    att_refs[1][0, c4, :, ocols] = picked.astype(BF16)
                    scr2_ref[i, gi, c4] = picked
                for c4 in range(r_mid):
                    for m in range(r_top // r_mid):
                        picked = scr2_ref[i, gi, c4, pl.ds(m, tm // r_top, stride=r_top // r_mid), :]
                        att_refs[2][0, m * r_mid + c4, :, ocols] = picked.astype(BF16)

    n_streams = n_att + 4
    project(0, 0)
    for i in range(n_streams):
        if i + 1 < n_streams:
            project(i + 1, (i + 1) % 2)
        post(i, i % 2)


def _rope_tables(seq):
    f32 = np.float32
    pos = np.arange(seq, dtype=f32)[:, None]
    half = ROT_DIM // 2
    inv = f32(ROPE_THETA) ** (-(np.arange(0, ROT_DIM, 2, dtype=f32) / f32(ROT_DIM)))
    ang = pos * inv[None, :]
    cos, sin = np.cos(ang), np.sin(ang)
    one = np.ones((seq, ATT_HEAD_DIM - ROT_DIM), f32)
    zero = np.zeros((seq, ATT_HEAD_DIM - ROT_DIM), f32)
    zh = np.zeros((seq, half), f32)
    ca = np.concatenate([cos, cos, one], axis=1)
    sa1 = np.concatenate([-sin, zh, zero], axis=1)
    sa2 = np.concatenate([zh, sin, zero], axis=1)
    reps = LANES // ATT_HEAD_DIM
    ca, sa1, sa2 = (np.tile(t, (1, reps)) for t in (ca, sa1, sa2))
    inv_r = f32(RET_ROPE_THETA) ** (-(np.arange(0, RET_HEAD_DIM, 2, dtype=f32) / f32(RET_HEAD_DIM)))
    ang_r = pos * inv_r[None, :]
    cr = np.concatenate([np.cos(ang_r), np.cos(ang_r)], axis=1)
    sr = np.concatenate([-np.sin(ang_r), np.sin(ang_r)], axis=1)
    return tuple(t.astype(f32) for t in (ca, sa1, sa2, cr, sr))


def _in_proj(x, g, w_in, tm=512):
    batch, seq, d = x.shape
    width = w_in.shape[1]
    tok = lambda w: pl.BlockSpec((1, tm, w), lambda b, t: (b, t, 0))
    tab = pl.BlockSpec((tm, LANES), lambda b, t: (t, 0))
    att_shapes, att_specs = [], []
    for _, r in DILATED_PATTERNS:
        assert tm % (r * 16) == 0
        att_shapes.append(jax.ShapeDtypeStruct((batch, r, seq // r, 3 * ATT_WIDTH), BF16))
        att_specs.append(pl.BlockSpec((1, r, tm // r, 3 * ATT_WIDTH), lambda b, t: (b, 0, t, 0)))
    ret_shape = jax.ShapeDtypeStruct((batch, seq, 4 * RET_WIDTH), BF16)
    outs = pl.pallas_call(
        _in_proj_kernel,
        out_shape=tuple(att_shapes) + (ret_shape,),
        grid=(batch, seq // tm),
        in_specs=[tok(d), _resident((1, d)), _resident((d, width))] + [tab] * 5,
        out_specs=tuple(att_specs) + (tok(4 * RET_WIDTH),),
        scratch_shapes=[pltpu.VMEM((tm, d), BF16),
                        pltpu.VMEM((2, tm, ATT_WIDTH), F32),
                        pltpu.VMEM((3, ATT_WIDTH // LANES, tm, LANES), F32),
                        pltpu.VMEM((3, ATT_WIDTH // LANES, DILATED_PATTERNS[1][1],
                                    tm // DILATED_PATTERNS[1][1], LANES), F32),
                        pltpu.VMEM((d, width), BF16)],
        compiler_params=_params("arbitrary", "arbitrary"),
        name="in_proj",
    )(x, g, w_in, *_rope_tables(seq))
    return outs[:len(att_shapes)], outs[len(att_shapes)]


def _attn_kernel(*refs, nc, nq, seq_tiles, has_prev, is_last, fold_mask):
    refs = list(refs)
    main_ref = refs.pop(0)
    if seq_tiles > 1:
        halo_ref = refs.pop(0)
    if has_prev:
        po_ref, ps_ref = refs[:2]
        del refs[:2]
    if is_last:
        g_ref = refs.pop(0)
    o_ref = refs.pop(0)
    if not is_last:
        s_ref = refs.pop(0)
    kbuf, vbuf, sc_buf, p_buf, bias, onehot = refs[:6]
    del refs[:6]
    if nc > 1:
        obuf, sbuf = refs

    half = ATT_HEAD_DIM
    groups = ATT_WIDTH // LANES
    chunk = 32
    kcols = slice(ATT_WIDTH, 2 * ATT_WIDTH)
    vcols = slice(2 * ATT_WIDTH, 3 * ATT_WIDTH)
    for cls in range(nc):
        if seq_tiles > 1:
            kbuf[cls, :BAND] = halo_ref[0, cls, :, kcols]
            vbuf[cls, :BAND] = halo_ref[0, cls, :, vcols]
        else:
            kbuf[cls, :BAND] = jnp.zeros((BAND, ATT_WIDTH), BF16)
            vbuf[cls, :BAND] = jnp.zeros((BAND, ATT_WIDTH), BF16)
        kbuf[cls, BAND:] = main_ref[0, cls, :, kcols]
        vbuf[cls, BAND:] = main_ref[0, cls, :, vcols]

    width = BAND if fold_mask else 2 * BAND
    key = lax.broadcasted_iota(jnp.int32, (2 * BAND, width), 0)
    qry = lax.broadcasted_iota(jnp.int32, (2 * BAND, width), 1) & (BAND - 1)
    if seq_tiles == 1:
        floor = BAND
    else:
        floor = jnp.where(pl.program_id(1) == 0, BAND, 0)
    upper = key <= qry + BAND
    bias[0] = jnp.where((key >= jnp.maximum(qry, floor)) & upper, 0.0, MASKED).astype(bias.dtype)
    bias[1] = jnp.where((key >= qry) & upper, 0.0, MASKED).astype(bias.dtype)
    if fold_mask:
        eye = (lax.broadcasted_iota(jnp.int32, (BAND, 2 * BAND), 0)
               == (lax.broadcasted_iota(jnp.int32, (BAND, 2 * BAND), 1) & (BAND - 1)))
        onehot[...] = jnp.where(eye, 1.0, 0.0).astype(BF16)

    row8 = lax.broadcasted_iota(jnp.int32, (8, BAND), 0)
    lo = lax.broadcasted_iota(jnp.int32, (BAND, LANES), 1) < half

    def scores(cls, qstart, slot, which):
        for gi, c0 in enumerate(range(0, ATT_WIDTH, LANES)):
            cols = slice(c0, c0 + LANES)
            qp = main_ref[0, cls, pl.ds(qstart, BAND), cols]
            kp = kbuf[cls, pl.ds(qstart, 2 * BAND), cols]
            if fold_mask:
                qt = qp.T
                zero = jnp.zeros((half, BAND), BF16)
                q_ext = jnp.concatenate([
                    jnp.concatenate([qt[:half], zero], axis=0),
                    jnp.concatenate([zero, qt[half:]], axis=0)], axis=1)
                q_ext = jnp.concatenate([q_ext, onehot[...]], axis=0)
                sc_buf[slot, gi] = _dot(jnp.concatenate([kp, bias[which]], axis=1), q_ext)
            else:
                zero = jnp.zeros_like(qp)
                q2 = jnp.concatenate([jnp.where(lo, qp, zero), jnp.where(lo, zero, qp)], axis=0)
                sc_buf[slot, gi] = _dot_nt(kp, q2) + bias[which]

    def softmax(slot):
        tops = []
        for gi in range(groups):
            top = sc_buf[slot, gi, 0:8, :]
            for c in range(0, 2 * BAND, chunk):
                sc = sc_buf[slot, gi, c:c + chunk, :]
                for r8 in range(0, chunk, 8):
                    top = jnp.maximum(top, sc[r8:r8 + 8, :])
            m = jnp.max(top, axis=0, keepdims=True)
            for c in range(0, 2 * BAND, chunk):
                e = jnp.exp2(sc_buf[slot, gi, c:c + chunk, :] - m)
                p_buf[gi, c:c + chunk, :] = e.astype(BF16)
            tops.append(m)
        return tops

    ones_rows = jnp.ones((16, 2 * BAND), BF16)

    def finish(cls, qstart, tops):
        qs = pl.ds(qstart, BAND)
        ks = pl.ds(qstart, 2 * BAND)
        if has_prev:
            prev_lse = ps_ref[0, cls, qs, :].T
        stats = jnp.zeros((8, BAND), F32)
        for gi, c0 in enumerate(range(0, ATT_WIDTH, LANES)):
            cols = slice(c0, c0 + LANES)
            m = tops[gi]
            lhs = jnp.concatenate([vbuf[cls, ks, cols].T, ones_rows], axis=0)
            ot = _dot(lhs, p_buf[gi])
            l = ot[LANES:LANES + 1, :]
            inv = 1.0 / l
            lse = m + jnp.log2(l)
            if has_prev:
                prev_o = po_ref[0, cls, qs, cols].astype(F32).T
            parts = []
            for hh in range(LANES // half):
                head = 2 * gi + hh
                qcols = slice(hh * BAND, (hh + 1) * BAND)
                rows = slice(hh * half, (hh + 1) * half)
                w_new = inv[:, qcols]
                lse_h = lse[:, qcols]
                if has_prev:
                    lp = prev_lse[head:head + 1, :]
                    top = jnp.maximum(lp, lse_h)
                    wp = jnp.exp2(lp - top)
                    wn = jnp.exp2(lse_h - top)
                    den = wp + wn
                    o = (wp / den) * prev_o[rows, :] + (wn * w_new / den) * ot[rows, qcols]
                    lse_h = top + jnp.log2(den)
                else:
                    o = ot[rows, qcols] * w_new
                if is_last:
                    ms = jnp.mean(o * o, axis=0, keepdims=True)
                    o = o * lax.rsqrt(ms + RMS_EPS)
                parts.append(o)
                stats = jnp.where(row8 == head, lse_h, stats)
            o = jnp.concatenate(parts, axis=0).T
            if is_last:
                o = o * g_ref[:, cols]
            if nc > 1:
                obuf[gi, pl.ds(qstart * nc + cls, BAND, stride=nc), :] = o
            else:
                o_ref[0, 0, qs, cols] = o.astype(o_ref.dtype)
        if not is_last:
            st = jnp.concatenate([stats, jnp.zeros((BAND - 8, BAND), F32)], axis=0).T
            if nc > 1:
                sbuf[pl.ds(qstart * nc + cls, BAND, stride=nc), :] = st
            else:
                s_ref[0, 0, qs, :] = st

    blocks = [(cls, n * BAND) for cls in range(nc) for n in range(nq)]
    which = lambda blk: 0 if blk[1] == 0 else 1
    scores(*blocks[0], 0, which(blocks[0]))
    for i, (cls, qstart) in enumerate(blocks):
        if i + 1 < len(blocks):
            scores(*blocks[i + 1], (i + 1) % 2, which(blocks[i + 1]))
        finish(cls, qstart, softmax(i % 2))
    if nc > 1:
        for gi, c0 in enumerate(range(0, ATT_WIDTH, LANES)):
            o_ref[0, 0, :, c0:c0 + LANES] = obuf[gi].astype(o_ref.dtype)
        if not is_last:
            s_ref[0, 0] = sbuf[...]


def _attention(qkv, gn_g):
    w = ATT_WIDTH
    order = sorted(range(len(DILATED_PATTERNS)), key=lambda p: -DILATED_PATTERNS[p][1])
    prev = None
    for step, p in enumerate(order):
        window, r = DILATED_PATTERNS[p]
        batch, _, length, _ = qkv[p].shape
        is_last = step == len(order) - 1
        r_next = 1 if is_last else DILATED_PATTERNS[order[step + 1]][1]
        nc = r // r_next
        assert window // r == BAND and r == nc * r_next
        nq = 2 if nc > 1 else 8
        seq_tiles = length // (nq * BAND)
        assert length % (nq * BAND) == 0
        rows = nq * BAND

        if nc > 1:
            shape5 = lambda t: t.reshape(batch, nc, r_next, length, t.shape[-1])
            main = lambda wd: pl.BlockSpec((1, nc, None, rows, wd), lambda b, t: (b, 0, t // seq_tiles, t % seq_tiles, 0))
            halo = lambda wd: pl.BlockSpec(
                (1, nc, None, BAND, wd),
                lambda b, t: (b, 0, t // seq_tiles, jnp.maximum(t % seq_tiles * nq - 1, 0), 0))
            grid = (batch, r_next * seq_tiles)
            out_blk = lambda wd: pl.BlockSpec((1, 1, nc * rows, wd), lambda b, t: (b, t // seq_tiles, t % seq_tiles, 0))
        else:
            shape5 = lambda t: t
            main = lambda wd: pl.BlockSpec((1, 1, rows, wd), lambda b, t: (b, 0, t, 0))
            halo = lambda wd: pl.BlockSpec((1, 1, BAND, wd), lambda b, t: (b, 0, jnp.maximum(t * nq - 1, 0), 0))
            grid = (batch, seq_tiles)
            out_blk = lambda wd: pl.BlockSpec((1, 1, rows, wd), lambda b, t: (b, 0, t, 0))
        if nc > 1 and seq_tiles > 1:
            assert r_next == 1

        args = [shape5(qkv[p])]
        in_specs = [main(3 * w)]
        if seq_tiles > 1:
            args.append(shape5(qkv[p]))
            in_specs.append(halo(3 * w))
        if prev is not None:
            args += [shape5(prev[0]), shape5(prev[1])]
            in_specs += [main(w), main(LANES)]
        if is_last:
            args.append(gn_g)
            in_specs.append(_resident((1, w)))
        o_shape = jax.ShapeDtypeStruct((batch, r_next, length * nc, w), BF16)
        s_shape = jax.ShapeDtypeStruct((batch, r_next, length * nc, LANES), F32)
        groups = w // LANES
        fold_mask = prev is None or is_last
        scratch = [pltpu.VMEM((nc, rows + BAND, w), BF16)] * 2 + [
            pltpu.VMEM((2, groups, 2 * BAND, 2 * BAND), F32),
            pltpu.VMEM((groups, 2 * BAND, 2 * BAND), BF16),
            pltpu.VMEM((2, 2 * BAND, BAND), BF16) if fold_mask else pltpu.VMEM((2, 2 * BAND, 2 * BAND), F32),
            pltpu.VMEM((BAND, 2 * BAND), BF16)]
        if nc > 1:
            scratch += [pltpu.VMEM((w // LANES, nc * rows, LANES), F32), pltpu.VMEM((nc * rows, LANES), F32)]
        res = pl.pallas_call(
            functools.partial(_attn_kernel, nc=nc, nq=nq, seq_tiles=seq_tiles,
                              has_prev=prev is not None, is_last=is_last, fold_mask=fold_mask),
            out_shape=o_shape if is_last else (o_shape, s_shape),
            grid=grid,
            in_specs=in_specs,
            out_specs=out_blk(w) if is_last else (out_blk(w), out_blk(LANES)),
            scratch_shapes=scratch,
            compiler_params=_params("parallel", "arbitrary"),
            name=f"attn_r{r}",
        )(*args)
        if is_last:
            return res.reshape(batch, length * nc, w)
        prev = res


def _retention_kernel(in_ref, dmask_ref, xi_ref, zeta_ref, cdec_ref, g_ref, *refs, n_cast):
    cast_in = refs[:n_cast]
    o_ref = refs[n_cast]
    cast_out = refs[n_cast + 1:2 * n_cast + 1]
    state_ref, raw_ref, upd_ref, pre_ref = refs[2 * n_cast + 1:]

    @pl.when(pl.program_id(1) == 0)
    def _():
        state_ref[...] = jnp.zeros_like(state_ref)

    for src, dst in zip(cast_in, cast_out):
        dst[...] = src[...].astype(BF16)

    chunk = RET_CHUNK

    def where(t, h):
        return slice(t, t + chunk), slice(h * RET_HEAD_DIM, (h + 1) * RET_HEAD_DIM)

    def stream(j, t, h):
        c0 = j * RET_WIDTH + h * RET_HEAD_DIM
        return in_ref[0, t:t + chunk, c0:c0 + RET_HEAD_DIM]

    def scores(t, h, slot):
        k = stream(1, t, h)
        raw_ref[slot] = _dot_nt(stream(0, t, h), k)
        kz = (k.astype(F32) * zeta_ref[h]).astype(BF16)
        upd_ref[slot] = _dot_tn(kz, stream(2, t, h))

    def mix(t, h, slot):
        inner = (raw_ref[slot] * dmask_ref[h]).astype(BF16)
        state = state_ref[h]
        pre_ref[slot] = (_dot(inner, stream(2, t, h))
                         + _dot(stream(0, t, h), state.astype(BF16)) * xi_ref[h])
        state_ref[h] = state * cdec_ref[h] + upd_ref[slot]

    def finish(t, h, slot):
        rows, c = where(t, h)
        o = pre_ref[slot]
        mu = jnp.mean(o, axis=-1, keepdims=True)
        var = jnp.mean(jnp.square(o - mu), axis=-1, keepdims=True)
        y = (o - mu) * lax.rsqrt(var + GN_EPS) * g_ref[:, c]
        gate = stream(3, t, h).astype(F32)
        o_ref[0, rows, c] = (gate * (1.0 / (1.0 + jnp.exp(-gate))) * y).astype(o_ref.dtype)

    units = [(t, h) for t in range(0, in_ref.shape[1], chunk) for h in range(RET_HEADS)]
    scores(*units[0], 0)
    for i, unit in enumerate(units):
        if i + 1 < len(units):
            scores(*units[i + 1], (i + 1) % 2)
        mix(*unit, i % 2)
        if i > 0:
            finish(*units[i - 1], (i - 1) % 2)
    finish(*units[-1], (len(units) - 1) % 2)


def _retention_tables():
    f32 = np.float32
    c = RET_CHUNK
    log_g = np.log(f32(1.0) - f32(2.0) ** (f32(-5.0) - np.arange(RET_HEADS, dtype=f32))).astype(f32)
    n = np.arange(c, dtype=f32)
    rel = n[:, None] - n[None, :]
    dmask = np.where(rel >= 0, np.exp(log_g[:, None, None] * np.maximum(rel, f32(0.0))), f32(0.0))
    xi = np.exp(log_g[:, None] * (n + f32(1.0)))
    zeta = np.exp(log_g[:, None] * (f32(c - 1.0) - n))
    cdec = np.exp(log_g * f32(c))
    wide = lambda t: np.ascontiguousarray(np.broadcast_to(t[:, :, None], (RET_HEADS, c, RET_HEAD_DIM)), f32)
    cdec = np.ascontiguousarray(np.broadcast_to(cdec[:, None, None], (RET_HEADS, RET_HEAD_DIM, RET_HEAD_DIM)), f32)
    return dmask.astype(f32), wide(xi), wide(zeta), cdec


def _retention(qkvg, gn_g, to_cast=(), chunks_per_step=4):
    batch, seq, _ = qkvg.shape
    width = RET_WIDTH
    c = RET_CHUNK
    rows = c * chunks_per_step
    steps_per_batch = seq // rows
    n_steps = batch * steps_per_batch
    blk = lambda wd: pl.BlockSpec((1, rows, wd), lambda b, n: (b, n, 0))
    tab = _resident((RET_HEADS, c, RET_HEAD_DIM))
    cast_specs, cast_shapes = [], []
    for w in to_cast:
        nb = max(k for k in range(1, n_steps + 1)
                 if n_steps % k == 0 and w.shape[0] % k == 0 and (w.shape[0] // k) % 16 == 0)
        hold = n_steps // nb
        cast_specs.append(pl.BlockSpec((w.shape[0] // nb, w.shape[1]),
                                       lambda b, n, hold=hold: ((b * steps_per_batch + n) // hold, 0)))
        cast_shapes.append(jax.ShapeDtypeStruct(w.shape, BF16))
    res = pl.pallas_call(
        functools.partial(_retention_kernel, n_cast=len(to_cast)),
        out_shape=(jax.ShapeDtypeStruct((batch, seq, width), BF16), *cast_shapes),
        grid=(batch, seq // rows),
        in_specs=[blk(4 * width)] + [tab] * 4 + [_resident((1, width))] + cast_specs,
        out_specs=(blk(width), *cast_specs),
        scratch_shapes=[pltpu.VMEM((RET_HEADS, RET_HEAD_DIM, RET_HEAD_DIM), F32),
                        pltpu.VMEM((2, c, c), F32),
                        pltpu.VMEM((2, RET_HEAD_DIM, RET_HEAD_DIM), F32),
                        pltpu.VMEM((2, c, RET_HEAD_DIM), F32)],
        compiler_params=_params("arbitrary", "arbitrary"),
        name="retention",
    )(qkvg, *_retention_tables(), gn_g, *to_cast)
    return res[0], res[1:]


def _mem_kv_kernel(mem_ref, g_ref, w32_ref, k_ref, v_ref, w_ref):
    d = mem_ref.shape[-1]
    _cast_once(1, [(w32_ref, w_ref)])
    kv = _dot(_rms(mem_ref[0], g_ref[...]).astype(BF16), w_ref[...])
    k_ref[0] = kv[:, :d].astype(BF16)
    v_ref[0] = kv[:, d:].astype(BF16)


def _mem_kv(mem, g, w_kv):
    batch, n_mem, d = mem.shape
    blk = pl.BlockSpec((1, n_mem, d), lambda b: (b, 0, 0))
    out = jax.ShapeDtypeStruct((batch, n_mem, d), BF16)
    return pl.pallas_call(
        _mem_kv_kernel,
        out_shape=(out, out),
        grid=(batch,),
        in_specs=[blk, _resident((1, d)), _resident((d, 2 * d))],
        out_specs=(blk, blk),
        scratch_shapes=[pltpu.VMEM((d, 2 * d), BF16)],
        compiler_params=_params("arbitrary"),
        name="mem_kv",
    )(mem, g, w_kv)


def _mix_out_kernel(a_ref, r_ref, x_ref, wout32_ref, gpost_ref, gpre_ref, wq32_ref, mk_ref, mv_ref,
                    wo32_ref, gpost2_ref, o_ref, x1_ref, h_ref, q_ref, att_ref,
                    wout_ref, wq_ref, wo_ref, *, sub):
    half = a_ref.shape[-1]
    d = x_ref.shape[-1]
    dh = d // XATT_HEADS
    _cast_once(2, [(wout32_ref, wout_ref), (wq32_ref, wq_ref), (wo32_ref, wo_ref)])

    def out_proj(rows):
        y = _dot(a_ref[0, rows, :], wout_ref[:half, :]) + _dot(r_ref[0, rows, :], wout_ref[half:, :])
        x1 = x_ref[0, rows, :] + _rms(y, gpost_ref[...])
        x1_ref[rows, :] = x1
        h_ref[rows, :] = _rms(x1, gpre_ref[...]).astype(BF16)

    def q_proj(rows):
        q_ref[rows, :] = (_dot(h_ref[rows, :], wq_ref[...]) * (dh ** -0.5)).astype(BF16)

    def cross_attn(rows):
        for hd in range(XATT_HEADS):
            c = slice(hd * dh, (hd + 1) * dh)
            s = _dot_nt(q_ref[rows, c], mk_ref[0, :, c])
            p = jnp.exp(s - jnp.max(s, axis=-1, keepdims=True))
            l = jnp.sum(p, axis=-1, keepdims=True)
            att_ref[rows, c] = (_dot(p.astype(BF16), mv_ref[0, :, c]) / l).astype(BF16)

    def o_proj(rows):
        y2 = _dot(att_ref[rows, :], wo_ref[...])
        o_ref[0, rows, :] = x1_ref[rows, :] + _rms(y2, gpost2_ref[...])

    tiles = [slice(r0, r0 + sub) for r0 in range(0, x_ref.shape[1], sub)]
    for stage in (out_proj, q_proj, cross_attn, o_proj):
        for rows in tiles:
            stage(rows)


def _mix_out(a, r, x, w_out, g_post, g_pre, w_q, mem_k, mem_v, w_o, g_post2, tm=1024, sub=256):
    batch, seq, d = x.shape
    half = a.shape[-1]
    n_mem = mem_k.shape[1]
    tok = lambda w: pl.BlockSpec((1, tm, w), lambda b, t: (b, t, 0))
    memblk = pl.BlockSpec((1, n_mem, d), lambda b, t: (b, 0, 0))
    vec = _resident((1, d))
    mat = _resident((d, d))
    return pl.pallas_call(
        functools.partial(_mix_out_kernel, sub=sub),
        out_shape=jax.ShapeDtypeStruct((batch, seq, d), F32),
        grid=(batch, seq // tm),
        in_specs=[tok(half), tok(half), tok(d), mat, vec, vec, mat, memblk, memblk, mat, vec],
        out_specs=tok(d),
        scratch_shapes=[pltpu.VMEM((tm, d), F32),
                        pltpu.VMEM((tm, d), BF16),
                        pltpu.VMEM((tm, d), BF16),
                        pltpu.VMEM((tm, d), BF16)]
        + [pltpu.VMEM((d, d), BF16)] * 3,
        compiler_params=_params("arbitrary", "arbitrary"),
        name="mix_out",
    )(a, r, x, w_out, g_post, g_pre, w_q, mem_k, mem_v, w_o, g_post2)


def _ffn_kernel(x_ref, gpre_ref, wgu_ref, wdown_ref, gpost_ref, o_ref, h_ref, y_ref, *, chunks, sub):
    hidden = wdown_ref.shape[0]

    def norm(rows):
        h_ref[rows, :] = _rms(x_ref[rows, :], gpre_ref[...]).astype(BF16)

    def hidden_chunk(ci, rows):
        lo, hi = chunks[ci]
        h = h_ref[rows, :]
        g = _dot(h, wgu_ref[:, lo:hi])
        u = _dot(h, wgu_ref[:, hidden + lo:hidden + hi])
        act = (g * (1.0 / (1.0 + jnp.exp(-g))) * u).astype(BF16)
        part = _dot(act, wdown_ref[lo:hi, :])
        if ci == 0:
            y_ref[rows, :] = part
        else:
            y_ref[rows, :] += part

    def final(rows):
        o_ref[rows, :] = x_ref[rows, :] + _rms(y_ref[rows, :], gpost_ref[...])

    tiles = [slice(r0, r0 + sub) for r0 in range(0, x_ref.shape[0], sub)]
    for rows in tiles:
        norm(rows)
    for ci in range(len(chunks)):
        for rows in tiles:
            hidden_chunk(ci, rows)
    for rows in tiles:
        final(rows)


def _ffn(x2d, g_pre, w_gu, w_down, g_post, tm=1024, sub=256):
    tokens, d = x2d.shape
    hidden = w_down.shape[0]
    mxu = 256
    cut = (hidden // 2) // mxu * mxu
    chunks = ((0, cut), (cut, hidden))
    row = pl.BlockSpec((tm, d), lambda t: (t, 0))
    vec = _resident((1, d))
    return pl.pallas_call(
        functools.partial(_ffn_kernel, chunks=chunks, sub=sub),
        out_shape=jax.ShapeDtypeStruct((tokens, d), F32),
        grid=(tokens // tm,),
        in_specs=[row, vec, _resident((d, 2 * hidden)), _resident((hidden, d)), vec],
        out_specs=row,
        scratch_shapes=[pltpu.VMEM((tm, d), BF16),
                        pltpu.VMEM((tm, d), F32)],
        compiler_params=_params("parallel"),
        name="ffn",
    )(x2d, g_pre, w_gu, w_down, g_post)


def kernel(x, mem, pre_mix_g, post_mix_g, w_in, attn_gn_g, ret_gn_g, w_out,
           pre_mem_g, post_mem_g, mem_norm_g, w_q_mem, w_kv_mem, w_o_mem,
           pre_ffn_g, post_ffn_g, w_gate_up, w_down):
    batch, seq, d = x.shape
    depth = w_in.shape[0]
    for l in range(depth):
        vec = lambda g: g[l][None, :]
        qkv, ret_in = _in_proj(x, vec(pre_mix_g), w_in[l])
        a = _attention(qkv, vec(attn_gn_g))
        r, (w_gu, w_dn) = _retention(ret_in, vec(ret_gn_g), to_cast=(w_gate_up[l], w_down[l]))
        mem_k, mem_v = _mem_kv(mem, vec(mem_norm_g), w_kv_mem[l])
        x = _mix_out(a, r, x, w_out[l], vec(post_mix_g), vec(pre_mem_g), w_q_mem[l],
                     mem_k, mem_v, w_o_mem[l], vec(post_mem_g))
        x = _ffn(x.reshape(batch * seq, d), vec(pre_ffn_g), w_gu, w_dn,
                 vec(post_ffn_g)).reshape(batch, seq, d)
    return x
```

```python
import functools

import numpy as np
import jax
import jax.numpy as jnp
from jax import lax
from jax.experimental import pallas as pl
from jax.experimental.pallas import tpu as pltpu

F32 = jnp.float32
BF16 = jnp.bfloat16

RMS_EPS = 1e-6
GN_EPS = 1e-5
ATT_HEADS = 8
ATT_HEAD_DIM = 64
ATT_WIDTH = ATT_HEADS * ATT_HEAD_DIM
ROT_DIM = ATT_HEAD_DIM // 4
ROPE_THETA = 500000.0
DILATED_PATTERNS = ((128, 1), (512, 4), (2048, 16))
RET_HEADS = 4
RET_HEAD_DIM = 128
RET_WIDTH = RET_HEADS * RET_HEAD_DIM
RET_CHUNK = 128
RET_ROPE_THETA = 10000.0
XATT_HEADS = 4

LANES = 128
BAND = 128
MASKED = -1e30
LOG2_E = 1.4426950408889634
VMEM_LIMIT = 56 * 1024 * 1024


def _resident(shape):
    zeros = (0,) * len(shape)
    return pl.BlockSpec(shape, lambda *_: zeros, pipeline_mode=pl.Buffered(1))


def _params(*sem):
    return pltpu.CompilerParams(dimension_semantics=sem, vmem_limit_bytes=VMEM_LIMIT)


def _cast_once(grid_rank, pairs, cols=512):
    first = pl.program_id(0) == 0
    for axis in range(1, grid_rank):
        first = jnp.logical_and(first, pl.program_id(axis) == 0)

    @pl.when(first)
    def _():
        for w_ref, wb_ref in pairs:
            for c in range(0, w_ref.shape[1], cols):
                wb_ref[:, c:c + cols] = w_ref[:, c:c + cols].astype(BF16)


def _rms(x, g):
    return x * lax.rsqrt(jnp.mean(x * x, axis=-1, keepdims=True) + RMS_EPS) * g


def _dot(a, b):
    return jnp.dot(a, b, preferred_element_type=F32)


def _dot_nt(a, b):
    return lax.dot_general(a, b, (((1,), (1,)), ((), ())), preferred_element_type=F32)


def _dot_tn(a, b):
    return lax.dot_general(a, b, (((0,), (0,)), ((), ())), preferred_element_type=F32)


def _in_proj_kernel(x_ref, g_ref, w_ref, ca_ref, sa1_ref, sa2_ref, cr_ref, sr_ref, *refs):
    att_refs = refs[:len(DILATED_PATTERNS)]
    ret_ref, h_ref, acc_ref, scr_ref, scr2_ref, wb_ref = refs[len(att_refs):]
    tm = x_ref.shape[1]
    width = ATT_WIDTH
    n_att = 3
    (_, r_one), (_, r_mid), (_, r_top) = DILATED_PATTERNS
    assert r_one == 1 and r_top % r_mid == 0
    rows_per_step = 64
    _cast_once(2, [(w_ref, wb_ref)])
    h_ref[...] = _rms(x_ref[0], g_ref[...]).astype(BF16)

    def project(i, slot):
        acc_ref[slot] = _dot(h_ref[...], wb_ref[:, i * width:(i + 1) * width])

    def rope_attn(t, rows):
        return (t * ca_ref[rows, :]
                + pltpu.roll(t, LANES - ROT_DIM // 2, 1) * sa1_ref[rows, :]
                + pltpu.roll(t, ROT_DIM // 2, 1) * sa2_ref[rows, :])

    def rope_ret(t, rows):
        return t * cr_ref[rows, :] + pltpu.roll(t, RET_HEAD_DIM // 2, 1) * sr_ref[rows, :]

    def post(i, slot):
        for gi, c0 in enumerate(range(0, width, LANES)):
            cols = slice(c0, c0 + LANES)
            for r0 in range(0, tm, rows_per_step):
                rows = slice(r0, r0 + rows_per_step)
                t = acc_ref[slot, rows, cols]
                if i == 0:
                    t = rope_attn(t, rows) * (ATT_HEAD_DIM ** -0.5 * LOG2_E)
                elif i == 1:
                    t = rope_attn(t, rows)
                elif i == n_att:
                    t = rope_ret(t, rows)
                elif i == n_att + 1:
                    t = rope_ret(t, rows) * (RET_HEAD_DIM ** -0.5)
                if i >= n_att:
                    ocols = slice((i - n_att) * width + c0, (i - n_att) * width + c0 + LANES)
                    ret_ref[0, rows, ocols] = t.astype(BF16)
                else:
                    ocols = slice(i * width + c0, i * width + c0 + LANES)
                    scr_ref[i, gi, rows, :] = t
                    att_refs[0][0, 0, rows, ocols] = t.astype(BF16)
            if i < n_att:
                for c4 in range(r_mid):
                    picked = scr_ref[i, gi, pl.ds(c4, tm // r_mid, stride=r_mid), :]
                    att_refs[1][0, c4, :, ocols] = picked.astype(BF16)
                    scr2_ref[i, gi, c4] = picked
                for c4 in range(r_mid):
                    for m in range(r_top // r_mid):
                        picked = scr2_ref[i, gi, c4, pl.ds(m, tm // r_top, stride=r_top // r_mid), :]
                        att_refs[2][0, m * r_mid + c4, :, ocols] = picked.astype(BF16)

    n_streams = n_att + 4
    project(0, 0)
    for i in range(n_streams):
        if i + 1 < n_streams:
            project(i + 1, (i + 1) % 2)
        post(i, i % 2)


def _rope_tables(seq):
    f32 = np.float32
    pos = np.arange(seq, dtype=f32)[:, None]
    half = ROT_DIM // 2
    inv = f32(ROPE_THETA) ** (-(np.arange(0, ROT_DIM, 2, dtype=f32) / f32(ROT_DIM)))
    ang = pos * inv[None, :]
    cos, sin = np.cos(ang), np.sin(ang)
    one = np.ones((seq, ATT_HEAD_DIM - ROT_DIM), f32)
    zero = np.zeros((seq, ATT_HEAD_DIM - ROT_DIM), f32)
    zh = np.zeros((seq, half), f32)
    ca = np.concatenate([cos, cos, one], axis=1)
    sa1 = np.concatenate([-sin, zh, zero], axis=1)
    sa2 = np.concatenate([zh, sin, zero], axis=1)
    reps = LANES // ATT_HEAD_DIM
    ca, sa1, sa2 = (np.tile(t, (1, reps)) for t in (ca, sa1, sa2))
    inv_r = f32(RET_ROPE_THETA) ** (-(np.arange(0, RET_HEAD_DIM, 2, dtype=f32) / f32(RET_HEAD_DIM)))
    ang_r = pos * inv_r[None, :]
    cr = np.concatenate([np.cos(ang_r), np.cos(ang_r)], axis=1)
    sr = np.concatenate([-np.sin(ang_r), np.sin(ang_r)], axis=1)
    return tuple(t.astype(f32) for t in (ca, sa1, sa2, cr, sr))


def _in_proj(x, g, w_in, tm=512):
    batch, seq, d = x.shape
    width = w_in.shape[1]
    tok = lambda w: pl.BlockSpec((1, tm, w), lambda b, t: (b, t, 0))
    tab = pl.BlockSpec((tm, LANES), lambda b, t: (t, 0))
    att_shapes, att_specs = [], []
    for _, r in DILATED_PATTERNS:
        assert tm % (r * 16) == 0
        att_shapes.append(jax.ShapeDtypeStruct((batch, r, seq // r, 3 * ATT_WIDTH), BF16))
        att_specs.append(pl.BlockSpec((1, r, tm // r, 3 * ATT_WIDTH), lambda b, t: (b, 0, t, 0)))
    ret_shape = jax.ShapeDtypeStruct((batch, seq, 4 * RET_WIDTH), BF16)
    outs = pl.pallas_call(
        _in_proj_kernel,
        out_shape=tuple(att_shapes) + (ret_shape,),
        grid=(batch, seq // tm),
        in_specs=[tok(d), _resident((1, d)), _resident((d, width))] + [tab] * 5,
        out_specs=tuple(att_specs) + (tok(4 * RET_WIDTH),),
        scratch_shapes=[pltpu.VMEM((tm, d), BF16),
                        pltpu.VMEM((2, tm, ATT_WIDTH), F32),
                        pltpu.VMEM((3, ATT_WIDTH // LANES, tm, LANES), F32),
                        pltpu.VMEM((3, ATT_WIDTH // LANES, DILATED_PATTERNS[1][1],
                                    tm // DILATED_PATTERNS[1][1], LANES), F32),
                        pltpu.VMEM((d, width), BF16)],
        compiler_params=_params("arbitrary", "arbitrary"),
        name="in_proj",
    )(x, g, w_in, *_rope_tables(seq))
    return outs[:len(att_shapes)], outs[len(att_shapes)]


def _attn_kernel(*refs, nc, nq, seq_tiles, has_prev, is_last, fold_mask):
    refs = list(refs)
    main_ref = refs.pop(0)
    if seq_tiles > 1:
        halo_ref = refs.pop(0)
    if has_prev:
        po_ref, ps_ref = refs[:2]
        del refs[:2]
    if is_last:
        g_ref = refs.pop(0)
    o_ref = refs.pop(0)
    if not is_last:
        s_ref = refs.pop(0)
    kbuf, vbuf, sc_buf, p_buf, bias, onehot = refs[:6]
    del refs[:6]
    if nc > 1:
        obuf, sbuf = refs

    half = ATT_HEAD_DIM
    groups = ATT_WIDTH // LANES
    chunk = 32
    kcols = slice(ATT_WIDTH, 2 * ATT_WIDTH)
    vcols = slice(2 * ATT_WIDTH, 3 * ATT_WIDTH)
    for cls in range(nc):
        if seq_tiles > 1:
            kbuf[cls, :BAND] = halo_ref[0, cls, :, kcols]
            vbuf[cls, :BAND] = halo_ref[0, cls, :, vcols]
        else:
            kbuf[cls, :BAND] = jnp.zeros((BAND, ATT_WIDTH), BF16)
            vbuf[cls, :BAND] = jnp.zeros((BAND, ATT_WIDTH), BF16)
        kbuf[cls, BAND:] = main_ref[0, cls, :, kcols]
        vbuf[cls, BAND:] = main_ref[0, cls, :, vcols]

    width = BAND if fold_mask else 2 * BAND
    key = lax.broadcasted_iota(jnp.int32, (2 * BAND, width), 0)
    qry = lax.broadcasted_iota(jnp.int32, (2 * BAND, width), 1) & (BAND - 1)
    if seq_tiles == 1:
        floor = BAND
    else:
        floor = jnp.where(pl.program_id(1) == 0, BAND, 0)
    upper = key <= qry + BAND
    bias[0] = jnp.where((key >= jnp.maximum(qry, floor)) & upper, 0.0, MASKED).astype(bias.dtype)
    bias[1] = jnp.where((key >= qry) & upper, 0.0, MASKED).astype(bias.dtype)
    if fold_mask:
        eye = (lax.broadcasted_iota(jnp.int32, (BAND, 2 * BAND), 0)
               == (lax.broadcasted_iota(jnp.int32, (BAND, 2 * BAND), 1) & (BAND - 1)))
        onehot[...] = jnp.where(eye, 1.0, 0.0).astype(BF16)

    row8 = lax.broadcasted_iota(jnp.int32, (8, BAND), 0)
    lo = lax.broadcasted_iota(jnp.int32, (BAND, LANES), 1) < half

    def scores(cls, qstart, slot, which):
        for gi, c0 in enumerate(range(0, ATT_WIDTH, LANES)):
            cols = slice(c0, c0 + LANES)
            qp = main_ref[0, cls, pl.ds(qstart, BAND), cols]
            kp = kbuf[cls, pl.ds(qstart, 2 * BAND), cols]
            if fold_mask:
                qt = qp.T
                zero = jnp.zeros((half, BAND), BF16)
                q_ext = jnp.concatenate([
                    jnp.concatenate([qt[:half], zero], axis=0),
                    jnp.concatenate([zero, qt[half:]], axis=0)], axis=1)
                q_ext = jnp.concatenate([q_ext, onehot[...]], axis=0)
                sc_buf[slot, gi] = _dot(jnp.concatenate([kp, bias[which]], axis=1), q_ext)
            else:
                zero = jnp.zeros_like(qp)
                q2 = jnp.concatenate([jnp.where(lo, qp, zero), jnp.where(lo, zero, qp)], axis=0)
                sc_buf[slot, gi] = _dot_nt(kp, q2) + bias[which]

    def softmax(slot):
        tops = []
        for gi in range(groups):
            top = sc_buf[slot, gi, 0:8, :]
            for c in range(0, 2 * BAND, chunk):
                sc = sc_buf[slot, gi, c:c + chunk, :]
                for r8 in range(0, chunk, 8):
                    top = jnp.maximum(top, sc[r8:r8 + 8, :])
            m = jnp.max(top, axis=0, keepdims=True)
            for c in range(0, 2 * BAND, chunk):
                e = jnp.exp2(sc_buf[slot, gi, c:c + chunk, :] - m)
                p_buf[gi, c:c + chunk, :] = e.astype(BF16)
            tops.append(m)
        return tops

    ones_rows = jnp.ones((16, 2 * BAND), BF16)

    def finish(cls, qstart, tops):
        qs = pl.ds(qstart, BAND)
        ks = pl.ds(qstart, 2 * BAND)
        if has_prev:
            prev_lse = ps_ref[0, cls, qs, :].T
        stats = jnp.zeros((8, BAND), F32)
        for gi, c0 in enumerate(range(0, ATT_WIDTH, LANES)):
            cols = slice(c0, c0 + LANES)
            m = tops[gi]
            lhs = jnp.concatenate([vbuf[cls, ks, cols].T, ones_rows], axis=0)
            ot = _dot(lhs, p_buf[gi])
            l = ot[LANES:LANES + 1, :]
            inv = 1.0 / l
            lse = m + jnp.log2(l)
            if has_prev:
                prev_o = po_ref[0, cls, qs, cols].astype(F32).T
            parts = []
            for hh in range(LANES // half):
                head = 2 * gi + hh
                qcols = slice(hh * BAND, (hh + 1) * BAND)
                rows = slice(hh * half, (hh + 1) * half)
                w_new = inv[:, qcols]
                lse_h = lse[:, qcols]
                if has_prev:
                    lp = prev_lse[head:head + 1, :]
                    top = jnp.maximum(lp, lse_h)
                    wp = jnp.exp2(lp - top)
                    wn = jnp.exp2(lse_h - top)
                    den = wp + wn
                    o = (wp / den) * prev_o[rows, :] + (wn * w_new / den) * ot[rows, qcols]
                    lse_h = top + jnp.log2(den)
                else:
                    o = ot[rows, qcols] * w_new
                if is_last:
                    ms = jnp.mean(o * o, axis=0, keepdims=True)
                    o = o * lax.rsqrt(ms + RMS_EPS)
                parts.append(o)
                stats = jnp.where(row8 == head, lse_h, stats)
            o = jnp.concatenate(parts, axis=0).T
            if is_last:
                o = o * g_ref[:, cols]
            if nc > 1:
                obuf[gi, pl.ds(qstart * nc + cls, BAND, stride=nc), :] = o
            else:
                o_ref[0, 0, qs, cols] = o.astype(o_ref.dtype)
        if not is_last:
            st = jnp.concatenate([stats, jnp.zeros((BAND - 8, BAND), F32)], axis=0).T
            if nc > 1:
                sbuf[pl.ds(qstart * nc + cls, BAND, stride=nc), :] = st
            else:
                s_ref[0, 0, qs, :] = st

    blocks = [(cls, n * BAND) for cls in range(nc) for n in range(nq)]
    which = lambda blk: 0 if blk[1] == 0 else 1
    scores(*blocks[0], 0, which(blocks[0]))
    for i, (cls, qstart) in enumerate(blocks):
        if i + 1 < len(blocks):
            scores(*blocks[i + 1], (i + 1) % 2, which(blocks[i + 1]))
        finish(cls, qstart, softmax(i % 2))
    if nc > 1:
        for gi, c0 in enumerate(range(0, ATT_WIDTH, LANES)):
            o_ref[0, 0, :, c0:c0 + LANES] = obuf[gi].astype(o_ref.dtype)
        if not is_last:
            s_ref[0, 0] = sbuf[...]


def _attention(qkv, gn_g):
    w = ATT_WIDTH
    order = sorted(range(len(DILATED_PATTERNS)), key=lambda p: -DILATED_PATTERNS[p][1])
    prev = None
    for step, p in enumerate(order):
        window, r = DILATED_PATTERNS[p]
        batch, _, length, _ = qkv[p].shape
        is_last = step == len(order) - 1
        r_next = 1 if is_last else DILATED_PATTERNS[order[step + 1]][1]
        nc = r // r_next
        assert window // r == BAND and r == nc * r_next
        nq = 2 if nc > 1 else 8
        seq_tiles = length // (nq * BAND)
        assert length % (nq * BAND) == 0
        rows = nq * BAND

        if nc > 1:
            shape5 = lambda t: t.reshape(batch, nc, r_next, length, t.shape[-1])
            main = lambda wd: pl.BlockSpec((1, nc, None, rows, wd), lambda b, t: (b, 0, t // seq_tiles, t % seq_tiles, 0))
            halo = lambda wd: pl.BlockSpec(
                (1, nc, None, BAND, wd),
                lambda b, t: (b, 0, t // seq_tiles, jnp.maximum(t % seq_tiles * nq - 1, 0), 0))
            grid = (batch, r_next * seq_tiles)
            out_blk = lambda wd: pl.BlockSpec((1, 1, nc * rows, wd), lambda b, t: (b, t // seq_tiles, t % seq_tiles, 0))
        else:
            shape5 = lambda t: t
            main = lambda wd: pl.BlockSpec((1, 1, rows, wd), lambda b, t: (b, 0, t, 0))
            halo = lambda wd: pl.BlockSpec((1, 1, BAND, wd), lambda b, t: (b, 0, jnp.maximum(t * nq - 1, 0), 0))
            grid = (batch, seq_tiles)
            out_blk = lambda wd: pl.BlockSpec((1, 1, rows, wd), lambda b, t: (b, 0, t, 0))
        if nc > 1 and seq_tiles > 1:
            assert r_next == 1

        args = [shape5(qkv[p])]
        in_specs = [main(3 * w)]
        if seq_tiles > 1:
            args.append(shape5(qkv[p]))
            in_specs.append(halo(3 * w))
        if prev is not None:
            args += [shape5(prev[0]), shape5(prev[1])]
            in_specs += [main(w), main(LANES)]
        if is_last:
            args.append(gn_g)
            in_specs.append(_resident((1, w)))
        o_shape = jax.ShapeDtypeStruct((batch, r_next, length * nc, w), BF16)
        s_shape = jax.ShapeDtypeStruct((batch, r_next, length * nc, LANES), F32)
        groups = w // LANES
        fold_mask = prev is None or is_last
        scratch = [pltpu.VMEM((nc, rows + BAND, w), BF16)] * 2 + [
            pltpu.VMEM((2, groups, 2 * BAND, 2 * BAND), F32),
            pltpu.VMEM((groups, 2 * BAND, 2 * BAND), BF16),
            pltpu.VMEM((2, 2 * BAND, BAND), BF16) if fold_mask else pltpu.VMEM((2, 2 * BAND, 2 * BAND), F32),
            pltpu.VMEM((BAND, 2 * BAND), BF16)]
        if nc > 1:
            scratch += [pltpu.VMEM((w // LANES, nc * rows, LANES), F32), pltpu.VMEM((nc * rows, LANES), F32)]
        res = pl.pallas_call(
            functools.partial(_attn_kernel, nc=nc, nq=nq, seq_tiles=seq_tiles,
                              has_prev=prev is not None, is_last=is_last, fold_mask=fold_mask),
            out_shape=o_shape if is_last else (o_shape, s_shape),
            grid=grid,
            in_specs=in_specs,
            out_specs=out_blk(w) if is_last else (out_blk(w), out_blk(LANES)),
            scratch_shapes=scratch,
            compiler_params=_params("parallel", "arbitrary"),
            name=f"attn_r{r}",
        )(*args)
        if is_last:
            return res.reshape(batch, length * nc, w)
        prev = res


def _retention_kernel(in_ref, dmask_ref, xi_ref, zeta_ref, cdec_ref, g_ref, *refs, n_cast):
    cast_in = refs[:n_cast]
    o_ref = refs[n_cast]
    cast_out = refs[n_cast + 1:2 * n_cast + 1]
    state_ref, raw_ref, upd_ref, pre_ref = refs[2 * n_cast + 1:]

    @pl.when(pl.program_id(1) == 0)
    def _():
        state_ref[...] = jnp.zeros_like(state_ref)

    for src, dst in zip(cast_in, cast_out):
        dst[...] = src[...].astype(BF16)

    chunk = RET_CHUNK

    def where(t, h):
        return slice(t, t + chunk), slice(h * RET_HEAD_DIM, (h + 1) * RET_HEAD_DIM)

    def stream(j, t, h):
        c0 = j * RET_WIDTH + h * RET_HEAD_DIM
        return in_ref[0, t:t + chunk, c0:c0 + RET_HEAD_DIM]

    def scores(t, h, slot):
        k = stream(1, t, h)
        raw_ref[slot] = _dot_nt(stream(0, t, h), k)
        kz = (k.astype(F32) * zeta_ref[h]).astype(BF16)
        upd_ref[slot] = _dot_tn(kz, stream(2, t, h))

    def mix(t, h, slot):
        inner = (raw_ref[slot] * dmask_ref[h]).astype(BF16)
        state = state_ref[h]
        pre_ref[slot] = (_dot(inner, stream(2, t, h))
                         + _dot(stream(0, t, h), state.astype(BF16)) * xi_ref[h])
        state_ref[h] = state * cdec_ref[h] + upd_ref[slot]

    def finish(t, h, slot):
        rows, c = where(t, h)
        o = pre_ref[slot]
        mu = jnp.mean(o, axis=-1, keepdims=True)
        var = jnp.mean(jnp.square(o - mu), axis=-1, keepdims=True)
        y = (o - mu) * lax.rsqrt(var + GN_EPS) * g_ref[:, c]
        gate = stream(3, t, h).astype(F32)
        o_ref[0, rows, c] = (gate * (1.0 / (1.0 + jnp.exp(-gate))) * y).astype(o_ref.dtype)

    units = [(t, h) for t in range(0, in_ref.shape[1], chunk) for h in range(RET_HEADS)]
    scores(*units[0], 0)
    for i, unit in enumerate(units):
        if i + 1 < len(units):
            scores(*units[i + 1], (i + 1) % 2)
        mix(*unit, i % 2)
        if i > 0:
            finish(*units[i - 1], (i - 1) % 2)
    finish(*units[-1], (len(units) - 1) % 2)


def _retention_tables():
    f32 = np.float32
    c = RET_CHUNK
    log_g = np.log(f32(1.0) - f32(2.0) ** (f32(-5.0) - np.arange(RET_HEADS, dtype=f32))).astype(f32)
    n = np.arange(c, dtype=f32)
    rel = n[:, None] - n[None, :]
    dmask = np.where(rel >= 0, np.exp(log_g[:, None, None] * np.maximum(rel, f32(0.0))), f32(0.0))
    xi = np.exp(log_g[:, None] * (n + f32(1.0)))
    zeta = np.exp(log_g[:, None] * (f32(c - 1.0) - n))
    cdec = np.exp(log_g * f32(c))
    wide = lambda t: np.ascontiguousarray(np.broadcast_to(t[:, :, None], (RET_HEADS, c, RET_HEAD_DIM)), f32)
    cdec = np.ascontiguousarray(np.broadcast_to(cdec[:, None, None], (RET_HEADS, RET_HEAD_DIM, RET_HEAD_DIM)), f32)
    return dmask.astype(f32), wide(xi), wide(zeta), cdec


def _retention(qkvg, gn_g, to_cast=(), chunks_per_step=4):
    batch, seq, _ = qkvg.shape
    width = RET_WIDTH
    c = RET_CHUNK
    rows = c * chunks_per_step
    steps_per_batch = seq // rows
    n_steps = batch * steps_per_batch
    blk = lambda wd: pl.BlockSpec((1, rows, wd), lambda b, n: (b, n, 0))
    tab = _resident((RET_HEADS, c, RET_HEAD_DIM))
    cast_specs, cast_shapes = [], []
    for w in to_cast:
        nb = max(k for k in range(1, n_steps + 1)
                 if n_steps % k == 0 and w.shape[0] % k == 0 and (w.shape[0] // k) % 16 == 0)
        hold = n_steps // nb
        cast_specs.append(pl.BlockSpec((w.shape[0] // nb, w.shape[1]),
                                       lambda b, n, hold=hold: ((b * steps_per_batch + n) // hold, 0)))
        cast_shapes.append(jax.ShapeDtypeStruct(w.shape, BF16))
    res = pl.pallas_call(
        functools.partial(_retention_kernel, n_cast=len(to_cast)),
        out_shape=(jax.ShapeDtypeStruct((batch, seq, width), BF16), *cast_shapes),
        grid=(batch, seq // rows),
        in_specs=[blk(4 * width)] + [tab] * 4 + [_resident((1, width))] + cast_specs,
        out_specs=(blk(width), *cast_specs),
        scratch_shapes=[pltpu.VMEM((RET_HEADS, RET_HEAD_DIM, RET_HEAD_DIM), F32),
                        pltpu.VMEM((2, c, c), F32),
                        pltpu.VMEM((2, RET_HEAD_DIM, RET_HEAD_DIM), F32),
                        pltpu.VMEM((2, c, RET_HEAD_DIM), F32)],
        compiler_params=_params("arbitrary", "arbitrary"),
        name="retention",
    )(qkvg, *_retention_tables(), gn_g, *to_cast)
    return res[0], res[1:]


def _mem_kv_kernel(mem_ref, g_ref, w32_ref, k_ref, v_ref, w_ref):
    d = mem_ref.shape[-1]
    _cast_once(1, [(w32_ref, w_ref)])
    kv = _dot(_rms(mem_ref[0], g_ref[...]).astype(BF16), w_ref[...])
    k_ref[0] = kv[:, :d].astype(BF16)
    v_ref[0] = kv[:, d:].astype(BF16)


def _mem_kv(mem, g, w_kv):
    batch, n_mem, d = mem.shape
    blk = pl.BlockSpec((1, n_mem, d), lambda b: (b, 0, 0))
    out = jax.ShapeDtypeStruct((batch, n_mem, d), BF16)
    return pl.pallas_call(
        _mem_kv_kernel,
        out_shape=(out, out),
        grid=(batch,),
        in_specs=[blk, _resident((1, d)), _resident((d, 2 * d))],
        out_specs=(blk, blk),
        scratch_shapes=[pltpu.VMEM((d, 2 * d), BF16)],
        compiler_params=_params("arbitrary"),
        name="mem_kv",
    )(mem, g, w_kv)


def _mix_out_kernel(a_ref, r_ref, x_ref, wout_ref, gpost_ref, gpre_ref, wq_ref, mk_ref, mv_ref,
                    wo_ref, gpost2_ref, o_ref, x1_ref, h_ref, q_ref, att_ref, *, sub):
    half = a_ref.shape[-1]
    d = x_ref.shape[-1]
    dh = d // XATT_HEADS

    def out_proj(rows):
        y = _dot(a_ref[0, rows, :], wout_ref[:half, :]) + _dot(r_ref[0, rows, :], wout_ref[half:, :])
        x1 = x_ref[0, rows, :] + _rms(y, gpost_ref[...])
        x1_ref[rows, :] = x1
        h_ref[rows, :] = _rms(x1, gpre_ref[...]).astype(BF16)

    def q_proj(rows):
        q_ref[rows, :] = (_dot(h_ref[rows, :], wq_ref[...]) * (dh ** -0.5)).astype(BF16)

    def cross_attn(rows):
        for hd in range(XATT_HEADS):
            c = slice(hd * dh, (hd + 1) * dh)
            s = _dot_nt(q_ref[rows, c], mk_ref[0, :, c])
            p = jnp.exp(s - jnp.max(s, axis=-1, keepdims=True))
            l = jnp.sum(p, axis=-1, keepdims=True)
            att_ref[rows, c] = (_dot(p.astype(BF16), mv_ref[0, :, c]) / l).astype(BF16)

    def o_proj(rows):
        y2 = _dot(att_ref[rows, :], wo_ref[...])
        o_ref[0, rows, :] = x1_ref[rows, :] + _rms(y2, gpost2_ref[...])

    tiles = [slice(r0, r0 + sub) for r0 in range(0, x_ref.shape[1], sub)]
    for stage in (out_proj, q_proj, cross_attn, o_proj):
        for rows in tiles:
            stage(rows)


def _mix_out(a, r, x, w_out, g_post, g_pre, w_q, mem_k, mem_v, w_o, g_post2, tm=1024, sub=256):
    batch, seq, d = x.shape
    half = a.shape[-1]
    n_mem = mem_k.shape[1]
    tok = lambda w: pl.BlockSpec((1, tm, w), lambda b, t: (b, t, 0))
    memblk = pl.BlockSpec((1, n_mem, d), lambda b, t: (b, 0, 0))
    vec = _resident((1, d))
    mat = _resident((d, d))
    return pl.pallas_call(
        functools.partial(_mix_out_kernel, sub=sub),
        out_shape=jax.ShapeDtypeStruct((batch, seq, d), F32),
        grid=(batch, seq // tm),
        in_specs=[tok(half), tok(half), tok(d), mat, vec, vec, mat, memblk, memblk, mat, vec],
        out_specs=tok(d),
        scratch_shapes=[pltpu.VMEM((tm, d), F32),
                        pltpu.VMEM((tm, d), BF16),
                        pltpu.VMEM((tm, d), BF16),
                        pltpu.VMEM((tm, d), BF16)],
        compiler_params=_params("parallel", "parallel"),
        name="mix_out",
    )(a, r, x, w_out, g_post, g_pre, w_q, mem_k, mem_v, w_o, g_post2)


def _ffn_kernel(x_ref, gpre_ref, wgu_ref, wdown_ref, gpost_ref, o_ref, h_ref, y_ref, *, chunks, sub):
    hidden = wdown_ref.shape[0]

    def norm(rows):
        h_ref[rows, :] = _rms(x_ref[rows, :], gpre_ref[...]).astype(BF16)

    def hidden_chunk(ci, rows):
        lo, hi = chunks[ci]
        h = h_ref[rows, :]
        g = _dot(h, wgu_ref[:, lo:hi])
        u = _dot(h, wgu_ref[:, hidden + lo:hidden + hi])
        act = (g * (1.0 / (1.0 + jnp.exp(-g))) * u).astype(BF16)
        part = _dot(act, wdown_ref[lo:hi, :])
        if ci == 0:
            y_ref[rows, :] = part
        else:
            y_ref[rows, :] += part

    def final(rows):
        o_ref[rows, :] = x_ref[rows, :] + _rms(y_ref[rows, :], gpost_ref[...])

    tiles = [slice(r0, r0 + sub) for r0 in range(0, x_ref.shape[0], sub)]
    for rows in tiles:
        norm(rows)
    for ci in range(len(chunks)):
        for rows in tiles:
            hidden_chunk(ci, rows)
    for rows in tiles:
        final(rows)


def _ffn(x2d, g_pre, w_gu, w_down, g_post, tm=1024, sub=256):
    tokens, d = x2d.shape
    hidden = w_down.shape[0]
    mxu = 256
    cut = (hidden // 2) // mxu * mxu
    chunks = ((0, cut), (cut, hidden))
    row = pl.BlockSpec((tm, d), lambda t: (t, 0))
    vec = _resident((1, d))
    return pl.pallas_call(
        functools.partial(_ffn_kernel, chunks=chunks, sub=sub),
        out_shape=jax.ShapeDtypeStruct((tokens, d), F32),
        grid=(tokens // tm,),
        in_specs=[row, vec, _resident((d, 2 * hidden)), _resident((hidden, d)), vec],
        out_specs=row,
        scratch_shapes=[pltpu.VMEM((tm, d), BF16),
                        pltpu.VMEM((tm, d), F32)],
        compiler_params=_params("parallel"),
        name="ffn",
    )(x2d, g_pre, w_gu, w_down, g_post)


def kernel(x, mem, pre_mix_g, post_mix_g, w_in, attn_gn_g, ret_gn_g, w_out,
           pre_mem_g, post_mem_g, mem_norm_g, w_q_mem, w_kv_mem, w_o_mem,
           pre_ffn_g, post_ffn_g, w_gate_up, w_down):
    batch, seq, d = x.shape
    depth = w_in.shape[0]
    for l in range(depth):
        vec = lambda g: g[l][None, :]
        qkv, ret_in = _in_proj(x, vec(pre_mix_g), w_in[l])
        a = _attention(qkv, vec(attn_gn_g))
        r, (w_gu, w_dn, w_o1, w_q, w_o2) = _retention(
            ret_in, vec(ret_gn_g),
            to_cast=(w_gate_up[l], w_down[l], w_out[l], w_q_mem[l], w_o_mem[l]))
        mem_k, mem_v = _mem_kv(mem, vec(mem_norm_g), w_kv_mem[l])
        x = _mix_out(a, r, x, w_o1, vec(post_mix_g), vec(pre_mem_g), w_q,
                     mem_k, mem_v, w_o2, vec(post_mem_g))
        x = _ffn(x.reshape(batch * seq, d), vec(pre_ffn_g), w_gu, w_dn,
                 vec(post_ffn_g)).reshape(batch, seq, d)
    return x
```

```python
import functools

import numpy as np
import jax
import jax.numpy as jnp
from jax import lax
from jax.experimental import pallas as pl
from jax.experimental.pallas import tpu as pltpu

F32 = jnp.float32
BF16 = jnp.bfloat16

RMS_EPS = 1e-6
GN_EPS = 1e-5
ATT_HEADS = 8
ATT_HEAD_DIM = 64
ATT_WIDTH = ATT_HEADS * ATT_HEAD_DIM
ROT_DIM = ATT_HEAD_DIM // 4
ROPE_THETA = 500000.0
DILATED_PATTERNS = ((128, 1), (512, 4), (2048, 16))
RET_HEADS = 4
RET_HEAD_DIM = 128
RET_WIDTH = RET_HEADS * RET_HEAD_DIM
RET_CHUNK = 128
RET_ROPE_THETA = 10000.0
XATT_HEADS = 4

LANES = 128
BAND = 128
MASKED = -1e30
LOG2_E = 1.4426950408889634
VMEM_LIMIT = 56 * 1024 * 1024


def _resident(shape):
    zeros = (0,) * len(shape)
    return pl.BlockSpec(shape, lambda *_: zeros, pipeline_mode=pl.Buffered(1))


def _params(*sem):
    return pltpu.CompilerParams(dimension_semantics=sem, vmem_limit_bytes=VMEM_LIMIT)


def _cast_once(grid_rank, pairs, cols=512):
    first = pl.program_id(0) == 0
    for axis in range(1, grid_rank):
        first = jnp.logical_and(first, pl.program_id(axis) == 0)

    @pl.when(first)
    def _():
        for w_ref, wb_ref in pairs:
            for c in range(0, w_ref.shape[1], cols):
                wb_ref[:, c:c + cols] = w_ref[:, c:c + cols].astype(BF16)


def _rms(x, g):
    return x * lax.rsqrt(jnp.mean(x * x, axis=-1, keepdims=True) + RMS_EPS) * g


def _dot(a, b):
    return jnp.dot(a, b, preferred_element_type=F32)


def _dot_nt(a, b):
    return lax.dot_general(a, b, (((1,), (1,)), ((), ())), preferred_element_type=F32)


def _dot_tn(a, b):
    return lax.dot_general(a, b, (((0,), (0,)), ((), ())), preferred_element_type=F32)


def _in_proj_kernel(x_ref, g_ref, w_ref, ca_ref, sa1_ref, sa2_ref, cr_ref, sr_ref, *refs):
    att_refs = refs[:len(DILATED_PATTERNS)]
    ret_ref, h_ref, acc_ref, scr_ref, scr2_ref, wb_ref = refs[len(att_refs):]
    tm = x_ref.shape[1]
    width = ATT_WIDTH
    n_att = 3
    (_, r_one), (_, r_mid), (_, r_top) = DILATED_PATTERNS
    assert r_one == 1 and r_top % r_mid == 0
    rows_per_step = 64
    _cast_once(2, [(w_ref, wb_ref)])
    h_ref[...] = _rms(x_ref[0], g_ref[...]).astype(BF16)

    def project(i, slot):
        acc_ref[slot] = _dot(h_ref[...], wb_ref[:, i * width:(i + 1) * width])

    def rope_attn(t, rows):
        return (t * ca_ref[rows, :]
                + pltpu.roll(t, LANES - ROT_DIM // 2, 1) * sa1_ref[rows, :]
                + pltpu.roll(t, ROT_DIM // 2, 1) * sa2_ref[rows, :])

    def rope_ret(t, rows):
        return t * cr_ref[rows, :] + pltpu.roll(t, RET_HEAD_DIM // 2, 1) * sr_ref[rows, :]

    def post(i, slot):
        for gi, c0 in enumerate(range(0, width, LANES)):
            cols = slice(c0, c0 + LANES)
            for r0 in range(0, tm, rows_per_step):
                rows = slice(r0, r0 + rows_per_step)
                t = acc_ref[slot, rows, cols]
                if i == 0:
                    t = rope_attn(t, rows) * (ATT_HEAD_DIM ** -0.5 * LOG2_E)
                elif i == 1:
                    t = rope_attn(t, rows)
                elif i == n_att:
                    t = rope_ret(t, rows)
                elif i == n_att + 1:
                    t = rope_ret(t, rows) * (RET_HEAD_DIM ** -0.5)
                if i >= n_att:
                    ocols = slice((i - n_att) * width + c0, (i - n_att) * width + c0 + LANES)
                    ret_ref[0, rows, ocols] = t.astype(BF16)
                else:
                    ocols = slice(i * width + c0, i * width + c0 + LANES)
                    scr_ref[i, gi, rows, :] = t
                    att_refs[0][0, 0, rows, ocols] = t.astype(BF16)
            if i < n_att:
                for c4 in range(r_mid):
                    picked = scr_ref[i, gi, pl.ds(c4, tm // r_mid, stride=r_mid), :]
                    att_refs[1][0, c4, :, ocols] = picked.astype(BF16)
                    scr2_ref[i, gi, c4] = picked
                for c4 in range(r_mid):
                    for m in range(r_top // r_mid):
                        picked = scr2_ref[i, gi, c4, pl.ds(m, tm // r_top, stride=r_top // r_mid), :]
                        att_refs[2][0, m * r_mid + c4, :, ocols] = picked.astype(BF16)

    n_streams = n_att + 4
    project(0, 0)
    for i in range(n_streams):
        if i + 1 < n_streams:
            project(i + 1, (i + 1) % 2)
        post(i, i % 2)


def _rope_tables(seq):
    f32 = np.float32
    pos = np.arange(seq, dtype=f32)[:, None]
    half = ROT_DIM // 2
    inv = f32(ROPE_THETA) ** (-(np.arange(0, ROT_DIM, 2, dtype=f32) / f32(ROT_DIM)))
    ang = pos * inv[None, :]
    cos, sin = np.cos(ang), np.sin(ang)
    one = np.ones((seq, ATT_HEAD_DIM - ROT_DIM), f32)
    zero = np.zeros((seq, ATT_HEAD_DIM - ROT_DIM), f32)
    zh = np.zeros((seq, half), f32)
    ca = np.concatenate([cos, cos, one], axis=1)
    sa1 = np.concatenate([-sin, zh, zero], axis=1)
    sa2 = np.concatenate([zh, sin, zero], axis=1)
    reps = LANES // ATT_HEAD_DIM
    ca, sa1, sa2 = (np.tile(t, (1, reps)) for t in (ca, sa1, sa2))
    inv_r = f32(RET_ROPE_THETA) ** (-(np.arange(0, RET_HEAD_DIM, 2, dtype=f32) / f32(RET_HEAD_DIM)))
    ang_r = pos * inv_r[None, :]
    cr = np.concatenate([np.cos(ang_r), np.cos(ang_r)], axis=1)
    sr = np.concatenate([-np.sin(ang_r), np.sin(ang_r)], axis=1)
    return tuple(t.astype(f32) for t in (ca, sa1, sa2, cr, sr))


def _in_proj(x, g, w_in, tm=512):
    batch, seq, d = x.shape
    width = w_in.shape[1]
    tok = lambda w: pl.BlockSpec((1, tm, w), lambda b, t: (b, t, 0))
    tab = pl.BlockSpec((tm, LANES), lambda b, t: (t, 0))
    att_shapes, att_specs = [], []
    for _, r in DILATED_PATTERNS:
        assert tm % (r * 16) == 0
        att_shapes.append(jax.ShapeDtypeStruct((batch, r, seq // r, 3 * ATT_WIDTH), BF16))
        att_specs.append(pl.BlockSpec((1, r, tm // r, 3 * ATT_WIDTH), lambda b, t: (b, 0, t, 0)))
    ret_shape = jax.ShapeDtypeStruct((batch, seq, 4 * RET_WIDTH), BF16)
    outs = pl.pallas_call(
        _in_proj_kernel,
        out_shape=tuple(att_shapes) + (ret_shape,),
        grid=(batch, seq // tm),
        in_specs=[tok(d), _resident((1, d)), _resident((d, width))] + [tab] * 5,
        out_specs=tuple(att_specs) + (tok(4 * RET_WIDTH),),
        scratch_shapes=[pltpu.VMEM((tm, d), BF16),
                        pltpu.VMEM((2, tm, ATT_WIDTH), F32),
                        pltpu.VMEM((3, ATT_WIDTH // LANES, tm, LANES), F32),
                        pltpu.VMEM((3, ATT_WIDTH // LANES, DILATED_PATTERNS[1][1],
                                    tm // DILATED_PATTERNS[1][1], LANES), F32),
                        pltpu.VMEM((d, width), BF16)],
        compiler_params=_params("arbitrary", "arbitrary"),
        name="in_proj",
    )(x, g, w_in, *_rope_tables(seq))
    return outs[:len(att_shapes)], outs[len(att_shapes)]


def _attn_kernel(*refs, nc, nq, seq_tiles, has_prev, is_last, fold_mask):
    refs = list(refs)
    main_ref = refs.pop(0)
    if seq_tiles > 1:
        halo_ref = refs.pop(0)
    if has_prev:
        po_ref, ps_ref = refs[:2]
        del refs[:2]
    if is_last:
        g_ref = refs.pop(0)
    o_ref = refs.pop(0)
    if not is_last:
        s_ref = refs.pop(0)
    kbuf, vbuf, p_buf, bias, onehot = refs[:5]
    del refs[:5]
    if nc > 1:
        obuf, sbuf = refs

    half = ATT_HEAD_DIM
    groups = ATT_WIDTH // LANES
    kcols = slice(ATT_WIDTH, 2 * ATT_WIDTH)
    vcols = slice(2 * ATT_WIDTH, 3 * ATT_WIDTH)
    for cls in range(nc):
        if seq_tiles > 1:
            kbuf[cls, :BAND] = halo_ref[0, cls, :, kcols]
            vbuf[cls, :BAND] = halo_ref[0, cls, :, vcols]
        else:
            kbuf[cls, :BAND] = jnp.zeros((BAND, ATT_WIDTH), BF16)
            vbuf[cls, :BAND] = jnp.zeros((BAND, ATT_WIDTH), BF16)
        kbuf[cls, BAND:] = main_ref[0, cls, :, kcols]
        vbuf[cls, BAND:] = main_ref[0, cls, :, vcols]

    width = BAND if fold_mask else 2 * BAND
    key = lax.broadcasted_iota(jnp.int32, (2 * BAND, width), 0)
    qry = lax.broadcasted_iota(jnp.int32, (2 * BAND, width), 1) & (BAND - 1)
    if seq_tiles == 1:
        floor = BAND
    else:
        floor = jnp.where(pl.program_id(1) == 0, BAND, 0)
    upper = key <= qry + BAND
    bias[0] = jnp.where((key >= jnp.maximum(qry, floor)) & upper, 0.0, MASKED).astype(bias.dtype)
    bias[1] = jnp.where((key >= qry) & upper, 0.0, MASKED).astype(bias.dtype)
    if fold_mask:
        eye = (lax.broadcasted_iota(jnp.int32, (BAND, 2 * BAND), 0)
               == (lax.broadcasted_iota(jnp.int32, (BAND, 2 * BAND), 1) & (BAND - 1)))
        onehot[...] = jnp.where(eye, 1.0, 0.0).astype(BF16)

    row8 = lax.broadcasted_iota(jnp.int32, (8, BAND), 0)
    lo = lax.broadcasted_iota(jnp.int32, (BAND, LANES), 1) < half

    def scores(cls, qstart, slot, which):
        tops = []
        for gi, c0 in enumerate(range(0, ATT_WIDTH, LANES)):
            cols = slice(c0, c0 + LANES)
            qp = main_ref[0, cls, pl.ds(qstart, BAND), cols]
            kp = kbuf[cls, pl.ds(qstart, 2 * BAND), cols]
            if fold_mask:
                qt = qp.T
                zero = jnp.zeros((half, BAND), BF16)
                q_ext = jnp.concatenate([
                    jnp.concatenate([qt[:half], zero], axis=0),
                    jnp.concatenate([zero, qt[half:]], axis=0)], axis=1)
                q_ext = jnp.concatenate([q_ext, onehot[...]], axis=0)
                s = _dot(jnp.concatenate([kp, bias[which]], axis=1), q_ext)
            else:
                zero = jnp.zeros_like(qp)
                q2 = jnp.concatenate([jnp.where(lo, qp, zero), jnp.where(lo, zero, qp)], axis=0)
                s = _dot_nt(kp, q2) + bias[which]
            pair = []
            for rows in (slice(0, BAND), slice(BAND, 2 * BAND)):
                sh = s[rows, :]
                parts = [sh[r8:r8 + 8, :] for r8 in range(0, BAND, 8)]
                while len(parts) > 1:
                    parts = [jnp.maximum(x, y) for x, y in zip(parts[0::2], parts[1::2])]
                m = jnp.max(parts[0], axis=0, keepdims=True)
                p_buf[slot, gi, rows, :] = jnp.exp2(sh - m).astype(BF16)
                pair.append(m)
            tops.append(pair)
        return tops

    ones_rows = jnp.ones((16, BAND), BF16)

    def finish(cls, qstart, slot, tops):
        qs = pl.ds(qstart, BAND)
        ks = pl.ds(qstart, 2 * BAND)
        if has_prev:
            prev_lse = ps_ref[0, cls, qs, :].T
        stats = jnp.zeros((8, BAND), F32)
        for gi, c0 in enumerate(range(0, ATT_WIDTH, LANES)):
            cols = slice(c0, c0 + LANES)
            m_prev, m_own = tops[gi]
            m = jnp.maximum(m_prev, m_own)
            vt = vbuf[cls, ks, cols].T
            ot = (_dot(jnp.concatenate([vt[:, :BAND], ones_rows], axis=0), p_buf[slot, gi, :BAND, :])
                  * jnp.exp2(m_prev - m)
                  + _dot(jnp.concatenate([vt[:, BAND:], ones_rows], axis=0), p_buf[slot, gi, BAND:, :])
                  * jnp.exp2(m_own - m))
            l = ot[LANES:LANES + 1, :]
            inv = 1.0 / l
            lse = m + jnp.log2(l)
            if has_prev:
                prev_o = po_ref[0, cls, qs, cols].astype(F32).T
            parts = []
            for hh in range(LANES // half):
                head = 2 * gi + hh
                qcols = slice(hh * BAND, (hh + 1) * BAND)
                rows = slice(hh * half, (hh + 1) * half)
                w_new = inv[:, qcols]
                lse_h = lse[:, qcols]
                if has_prev:
                    lp = prev_lse[head:head + 1, :]
                    top = jnp.maximum(lp, lse_h)
                    wp = jnp.exp2(lp - top)
                    wn = jnp.exp2(lse_h - top)
                    den = wp + wn
                    o = (wp / den) * prev_o[rows, :] + (wn * w_new / den) * ot[rows, qcols]
                    lse_h = top + jnp.log2(den)
                else:
                    o = ot[rows, qcols] * w_new
                if is_last:
                    ms = jnp.mean(o * o, axis=0, keepdims=True)
                    o = o * lax.rsqrt(ms + RMS_EPS)
                parts.append(o)
                stats = jnp.where(row8 == head, lse_h, stats)
            o = jnp.concatenate(parts, axis=0).T
            if is_last:
                o = o * g_ref[:, cols]
            if nc > 1:
                obuf[gi, pl.ds(qstart * nc + cls, BAND, stride=nc), :] = o
            else:
                o_ref[0, 0, qs, cols] = o.astype(o_ref.dtype)
        if not is_last:
            st = jnp.concatenate([stats, jnp.zeros((BAND - 8, BAND), F32)], axis=0).T
            if nc > 1:
                sbuf[pl.ds(qstart * nc + cls, BAND, stride=nc), :] = st
            else:
                s_ref[0, 0, qs, :] = st

    blocks = [(cls, n * BAND) for cls in range(nc) for n in range(nq)]
    which = lambda blk: 0 if blk[1] == 0 else 1
    tops = scores(*blocks[0], 0, which(blocks[0]))
    for i, (cls, qstart) in enumerate(blocks):
        if i + 1 < len(blocks):
            tops_next = scores(*blocks[i + 1], (i + 1) % 2, which(blocks[i + 1]))
        finish(cls, qstart, i % 2, tops)
        tops = tops_next
    if nc > 1:
        for gi, c0 in enumerate(range(0, ATT_WIDTH, LANES)):
            o_ref[0, 0, :, c0:c0 + LANES] = obuf[gi].astype(o_ref.dtype)
        if not is_last:
            s_ref[0, 0] = sbuf[...]


def _attention(qkv, gn_g):
    w = ATT_WIDTH
    order = sorted(range(len(DILATED_PATTERNS)), key=lambda p: -DILATED_PATTERNS[p][1])
    prev = None
    for step, p in enumerate(order):
        window, r = DILATED_PATTERNS[p]
        batch, _, length, _ = qkv[p].shape
        is_last = step == len(order) - 1
        r_next = 1 if is_last else DILATED_PATTERNS[order[step + 1]][1]
        nc = r // r_next
        assert window // r == BAND and r == nc * r_next
        nq = 2 if nc > 1 else 8
        seq_tiles = length // (nq * BAND)
        assert length % (nq * BAND) == 0
        rows = nq * BAND

        if nc > 1:
            shape5 = lambda t: t.reshape(batch, nc, r_next, length, t.shape[-1])
            main = lambda wd: pl.BlockSpec((1, nc, None, rows, wd), lambda b, t: (b, 0, t // seq_tiles, t % seq_tiles, 0))
            halo = lambda wd: pl.BlockSpec(
                (1, nc, None, BAND, wd),
                lambda b, t: (b, 0, t // seq_tiles, jnp.maximum(t % seq_tiles * nq - 1, 0), 0))
            grid = (batch, r_next * seq_tiles)
            out_blk = lambda wd: pl.BlockSpec((1, 1, nc * rows, wd), lambda b, t: (b, t // seq_tiles, t % seq_tiles, 0))
        else:
            shape5 = lambda t: t
            main = lambda wd: pl.BlockSpec((1, 1, rows, wd), lambda b, t: (b, 0, t, 0))
            halo = lambda wd: pl.BlockSpec((1, 1, BAND, wd), lambda b, t: (b, 0, jnp.maximum(t * nq - 1, 0), 0))
            grid = (batch, seq_tiles)
            out_blk = lambda wd: pl.BlockSpec((1, 1, rows, wd), lambda b, t: (b, 0, t, 0))
        if nc > 1 and seq_tiles > 1:
            assert r_next == 1

        args = [shape5(qkv[p])]
        in_specs = [main(3 * w)]
        if seq_tiles > 1:
            args.append(shape5(qkv[p]))
            in_specs.append(halo(3 * w))
        if prev is not None:
            args += [shape5(prev[0]), shape5(prev[1])]
            in_specs += [main(w), main(LANES)]
        if is_last:
            args.append(gn_g)
            in_specs.append(_resident((1, w)))
        o_shape = jax.ShapeDtypeStruct((batch, r_next, length * nc, w), BF16)
        s_shape = jax.ShapeDtypeStruct((batch, r_next, length * nc, LANES), F32)
        groups = w // LANES
        fold_mask = prev is None or is_last
        scratch = [pltpu.VMEM((nc, rows + BAND, w), BF16)] * 2 + [
            pltpu.VMEM((2, groups, 2 * BAND, 2 * BAND), BF16),
            pltpu.VMEM((2, 2 * BAND, BAND), BF16) if fold_mask else pltpu.VMEM((2, 2 * BAND, 2 * BAND), F32),
            pltpu.VMEM((BAND, 2 * BAND), BF16)]
        if nc > 1:
            scratch += [pltpu.VMEM((w // LANES, nc * rows, LANES), F32), pltpu.VMEM((nc * rows, LANES), F32)]
        res = pl.pallas_call(
            functools.partial(_attn_kernel, nc=nc, nq=nq, seq_tiles=seq_tiles,
                              has_prev=prev is not None, is_last=is_last, fold_mask=fold_mask),
            out_shape=o_shape if is_last else (o_shape, s_shape),
            grid=grid,
            in_specs=in_specs,
            out_specs=out_blk(w) if is_last else (out_blk(w), out_blk(LANES)),
            scratch_shapes=scratch,
            compiler_params=_params("parallel", "arbitrary"),
            name=f"attn_r{r}",
        )(*args)
        if is_last:
            return res.reshape(batch, length * nc, w)
        prev = res


def _retention_kernel(in_ref, dmask_ref, xi_ref, zeta_ref, cdec_ref, g_ref, *refs, n_cast):
    cast_in = refs[:n_cast]
    o_ref = refs[n_cast]
    cast_out = refs[n_cast + 1:2 * n_cast + 1]
    state_ref, raw_ref, upd_ref, pre_ref = refs[2 * n_cast + 1:]

    @pl.when(pl.program_id(1) == 0)
    def _():
        state_ref[...] = jnp.zeros_like(state_ref)

    for src, dst in zip(cast_in, cast_out):
        dst[...] = src[...].astype(BF16)

    chunk = RET_CHUNK

    def where(t, h):
        return slice(t, t + chunk), slice(h * RET_HEAD_DIM, (h + 1) * RET_HEAD_DIM)

    def stream(j, t, h):
        c0 = j * RET_WIDTH + h * RET_HEAD_DIM
        return in_ref[0, t:t + chunk, c0:c0 + RET_HEAD_DIM]

    def scores(t, h, slot):
        k = stream(1, t, h)
        raw_ref[slot] = _dot_nt(stream(0, t, h), k)
        kz = (k.astype(F32) * zeta_ref[h]).astype(BF16)
        upd_ref[slot] = _dot_tn(kz, stream(2, t, h))

    def mix(t, h, slot):
        inner = (raw_ref[slot] * dmask_ref[h]).astype(BF16)
        state = state_ref[h]
        pre_ref[slot] = (_dot(inner, stream(2, t, h))
                         + _dot(stream(0, t, h), state.astype(BF16)) * xi_ref[h])
        state_ref[h] = state * cdec_ref[h] + upd_ref[slot]

    def finish(t, h, slot):
        rows, c = where(t, h)
        o = pre_ref[slot]
        mu = jnp.mean(o, axis=-1, keepdims=True)
        var = jnp.mean(jnp.square(o - mu), axis=-1, keepdims=True)
        y = (o - mu) * lax.rsqrt(var + GN_EPS) * g_ref[:, c]
        gate = stream(3, t, h).astype(F32)
        o_ref[0, rows, c] = (gate * (1.0 / (1.0 + jnp.exp(-gate))) * y).astype(o_ref.dtype)

    units = [(t, h) for t in range(0, in_ref.shape[1], chunk) for h in range(RET_HEADS)]
    scores(*units[0], 0)
    for i, unit in enumerate(units):
        if i + 1 < len(units):
            scores(*units[i + 1], (i + 1) % 2)
        mix(*unit, i % 2)
        if i > 0:
            finish(*units[i - 1], (i - 1) % 2)
    finish(*units[-1], (len(units) - 1) % 2)


def _retention_tables():
    f32 = np.float32
    c = RET_CHUNK
    log_g = np.log(f32(1.0) - f32(2.0) ** (f32(-5.0) - np.arange(RET_HEADS, dtype=f32))).astype(f32)
    n = np.arange(c, dtype=f32)
    rel = n[:, None] - n[None, :]
    dmask = np.where(rel >= 0, np.exp(log_g[:, None, None] * np.maximum(rel, f32(0.0))), f32(0.0))
    xi = np.exp(log_g[:, None] * (n + f32(1.0)))
    zeta = np.exp(log_g[:, None] * (f32(c - 1.0) - n))
    cdec = np.exp(log_g * f32(c))
    wide = lambda t: np.ascontiguousarray(np.broadcast_to(t[:, :, None], (RET_HEADS, c, RET_HEAD_DIM)), f32)
    cdec = np.ascontiguousarray(np.broadcast_to(cdec[:, None, None], (RET_HEADS, RET_HEAD_DIM, RET_HEAD_DIM)), f32)
    return dmask.astype(f32), wide(xi), wide(zeta), cdec


def _retention(qkvg, gn_g, to_cast=(), chunks_per_step=4):
    batch, seq, _ = qkvg.shape
    width = RET_WIDTH
    c = RET_CHUNK
    rows = c * chunks_per_step
    steps_per_batch = seq // rows
    n_steps = batch * steps_per_batch
    blk = lambda wd: pl.BlockSpec((1, rows, wd), lambda b, n: (b, n, 0))
    tab = _resident((RET_HEADS, c, RET_HEAD_DIM))
    cast_specs, cast_shapes = [], []
    for w in to_cast:
        nb = max(k for k in range(1, n_steps + 1)
                 if n_steps % k == 0 and w.shape[0] % k == 0 and (w.shape[0] // k) % 16 == 0)
        hold = n_steps // nb
        cast_specs.append(pl.BlockSpec((w.shape[0] // nb, w.shape[1]),
                                       lambda b, n, hold=hold: ((b * steps_per_batch + n) // hold, 0)))
        cast_shapes.append(jax.ShapeDtypeStruct(w.shape, BF16))
    res = pl.pallas_call(
        functools.partial(_retention_kernel, n_cast=len(to_cast)),
        out_shape=(jax.ShapeDtypeStruct((batch, seq, width), BF16), *cast_shapes),
        grid=(batch, seq // rows),
        in_specs=[blk(4 * width)] + [tab] * 4 + [_resident((1, width))] + cast_specs,
        out_specs=(blk(width), *cast_specs),
        scratch_shapes=[pltpu.VMEM((RET_HEADS, RET_HEAD_DIM, RET_HEAD_DIM), F32),
                        pltpu.VMEM((2, c, c), F32),
                        pltpu.VMEM((2, RET_HEAD_DIM, RET_HEAD_DIM), F32),
                        pltpu.VMEM((2, c, RET_HEAD_DIM), F32)],
        compiler_params=_params("arbitrary", "arbitrary"),
        name="retention",
    )(qkvg, *_retention_tables(), gn_g, *to_cast)
    return res[0], res[1:]


def _mem_kv_kernel(mem_ref, g_ref, w32_ref, k_ref, v_ref, w_ref):
    d = mem_ref.shape[-1]
    _cast_once(1, [(w32_ref, w_ref)])
    kv = _dot(_rms(mem_ref[0], g_ref[...]).astype(BF16), w_ref[...])
    k_ref[0] = kv[:, :d].astype(BF16)
    v_ref[0] = kv[:, d:].astype(BF16)


def _mem_kv(mem, g, w_kv):
    batch, n_mem, d = mem.shape
    blk = pl.BlockSpec((1, n_mem, d), lambda b: (b, 0, 0))
    out = jax.ShapeDtypeStruct((batch, n_mem, d), BF16)
    return pl.pallas_call(
        _mem_kv_kernel,
        out_shape=(out, out),
        grid=(batch,),
        in_specs=[blk, _resident((1, d)), _resident((d, 2 * d))],
        out_specs=(blk, blk),
        scratch_shapes=[pltpu.VMEM((d, 2 * d), BF16)],
        compiler_params=_params("arbitrary"),
        name="mem_kv",
    )(mem, g, w_kv)


def _mix_out_kernel(a_ref, r_ref, x_ref, wout_ref, gpost_ref, gpre_ref, wq_ref, mk_ref, mv_ref,
                    wo_ref, gpost2_ref, o_ref, x1_ref, h_ref, q_ref, att_ref, *, sub):
    half = a_ref.shape[-1]
    d = x_ref.shape[-1]
    dh = d // XATT_HEADS

    def out_proj(rows):
        y = _dot(a_ref[0, rows, :], wout_ref[:half, :]) + _dot(r_ref[0, rows, :], wout_ref[half:, :])
        x1 = x_ref[0, rows, :] + _rms(y, gpost_ref[...])
        x1_ref[rows, :] = x1
        h_ref[rows, :] = _rms(x1, gpre_ref[...]).astype(BF16)

    def q_proj(rows):
        q_ref[rows, :] = (_dot(h_ref[rows, :], wq_ref[...]) * (dh ** -0.5)).astype(BF16)

    def cross_attn(rows):
        for hd in range(XATT_HEADS):
            c = slice(hd * dh, (hd + 1) * dh)
            s = _dot_nt(q_ref[rows, c], mk_ref[0, :, c])
            p = jnp.exp(s - jnp.max(s, axis=-1, keepdims=True))
            l = jnp.sum(p, axis=-1, keepdims=True)
            att_ref[rows, c] = (_dot(p.astype(BF16), mv_ref[0, :, c]) / l).astype(BF16)

    def o_proj(rows):
        y2 = _dot(att_ref[rows, :], wo_ref[...])
        o_ref[0, rows, :] = x1_ref[rows, :] + _rms(y2, gpost2_ref[...])

    tiles = [slice(r0, r0 + sub) for r0 in range(0, x_ref.shape[1], sub)]
    for stage in (out_proj, q_proj, cross_attn, o_proj):
        for rows in tiles:
            stage(rows)


def _mix_out(a, r, x, w_out, g_post, g_pre, w_q, mem_k, mem_v, w_o, g_post2, tm=1024, sub=256):
    batch, seq, d = x.shape
    half = a.shape[-1]
    n_mem = mem_k.shape[1]
    tok = lambda w: pl.BlockSpec((1, tm, w), lambda b, t: (b, t, 0))
    memblk = pl.BlockSpec((1, n_mem, d), lambda b, t: (b, 0, 0))
    vec = _resident((1, d))
    mat = _resident((d, d))
    return pl.pallas_call(
        functools.partial(_mix_out_kernel, sub=sub),
        out_shape=jax.ShapeDtypeStruct((batch, seq, d), F32),
        grid=(batch, seq // tm),
        in_specs=[tok(half), tok(half), tok(d), mat, vec, vec, mat, memblk, memblk, mat, vec],
        out_specs=tok(d),
        scratch_shapes=[pltpu.VMEM((tm, d), F32),
                        pltpu.VMEM((tm, d), BF16),
                        pltpu.VMEM((tm, d), BF16),
                        pltpu.VMEM((tm, d), BF16)],
        compiler_params=_params("parallel", "parallel"),
        name="mix_out",
    )(a, r, x, w_out, g_post, g_pre, w_q, mem_k, mem_v, w_o, g_post2)


def _ffn_kernel(x_ref, gpre_ref, wgu_ref, wdown_ref, gpost_ref, o_ref, h_ref, y_ref, *, chunks, sub):
    hidden = wdown_ref.shape[0]

    def norm(rows):
        h_ref[rows, :] = _rms(x_ref[rows, :], gpre_ref[...]).astype(BF16)

    def hidden_chunk(ci, rows):
        lo, hi = chunks[ci]
        h = h_ref[rows, :]
        g = _dot(h, wgu_ref[:, lo:hi])
        u = _dot(h, wgu_ref[:, hidden + lo:hidden + hi])
        act = (g * (1.0 / (1.0 + jnp.exp(-g))) * u).astype(BF16)
        part = _dot(act, wdown_ref[lo:hi, :])
        if ci == 0:
            y_ref[rows, :] = part
        else:
            y_ref[rows, :] += part

    def final(rows):
        o_ref[rows, :] = x_ref[rows, :] + _rms(y_ref[rows, :], gpost_ref[...])

    tiles = [slice(r0, r0 + sub) for r0 in range(0, x_ref.shape[0], sub)]
    for rows in tiles:
        norm(rows)
    for ci in range(len(chunks)):
        for rows in tiles:
            hidden_chunk(ci, rows)
    for rows in tiles:
        final(rows)


def _ffn(x2d, g_pre, w_gu, w_down, g_post, tm=1024, sub=256):
    tokens, d = x2d.shape
    hidden = w_down.shape[0]
    mxu = 256
    cut = (hidden // 2) // mxu * mxu
    chunks = ((0, cut), (cut, hidden))
    row = pl.BlockSpec((tm, d), lambda t: (t, 0))
    vec = _resident((1, d))
    return pl.pallas_call(
        functools.partial(_ffn_kernel, chunks=chunks, sub=sub),
        out_shape=jax.ShapeDtypeStruct((tokens, d), F32),
        grid=(tokens // tm,),
        in_specs=[row, vec, _resident((d, 2 * hidden)), _resident((hidden, d)), vec],
        out_specs=row,
        scratch_shapes=[pltpu.VMEM((tm, d), BF16),
                        pltpu.VMEM((tm, d), F32)],
        compiler_params=_params("parallel"),
        name="ffn",
    )(x2d, g_pre, w_gu, w_down, g_post)


def kernel(x, mem, pre_mix_g, post_mix_g, w_in, attn_gn_g, ret_gn_g, w_out,
           pre_mem_g, post_mem_g, mem_norm_g, w_q_mem, w_kv_mem, w_o_mem,
           pre_ffn_g, post_ffn_g, w_gate_up, w_down):
    batch, seq, d = x.shape
    depth = w_in.shape[0]
    for l in range(depth):
        vec = lambda g: g[l][None, :]
        qkv, ret_in = _in_proj(x, vec(pre_mix_g), w_in[l])
        a = _attention(qkv, vec(attn_gn_g))
        r, (w_gu, w_dn, w_o1, w_q, w_o2) = _retention(
            ret_in, vec(ret_gn_g),
            to_cast=(w_gate_up[l], w_down[l], w_out[l], w_q_mem[l], w_o_mem[l]))
        mem_k, mem_v = _mem_kv(mem, vec(mem_norm_g), w_kv_mem[l])
        x = _mix_out(a, r, x, w_o1, vec(post_mix_g), vec(pre_mem_g), w_q,
                     mem_k, mem_v, w_o2, vec(post_mem_g))
        x = _ffn(x.reshape(batch * seq, d), vec(pre_ffn_g), w_gu, w_dn,
                 vec(post_ffn_g)).reshape(batch, seq, d)
    return x
```

```python
import functools

import numpy as np
import jax
import jax.numpy as jnp
from jax import lax
from jax.experimental import pallas as pl
from jax.experimental.pallas import tpu as pltpu

F32 = jnp.float32
BF16 = jnp.bfloat16

RMS_EPS = 1e-6
GN_EPS = 1e-5
ATT_HEADS = 8
ATT_HEAD_DIM = 64
ATT_WIDTH = ATT_HEADS * ATT_HEAD_DIM
ROT_DIM = ATT_HEAD_DIM // 4
ROPE_THETA = 500000.0
DILATED_PATTERNS = ((128, 1), (512, 4), (2048, 16))
RET_HEADS = 4
RET_HEAD_DIM = 128
RET_WIDTH = RET_HEADS * RET_HEAD_DIM
RET_CHUNK = 128
RET_ROPE_THETA = 10000.0
XATT_HEADS = 4

LANES = 128
BAND = 128
MASKED = -1e30
LOG2_E = 1.4426950408889634
VMEM_LIMIT = 56 * 1024 * 1024


def _resident(shape):
    zeros = (0,) * len(shape)
    return pl.BlockSpec(shape, lambda *_: zeros, pipeline_mode=pl.Buffered(1))


def _params(*sem):
    return pltpu.CompilerParams(dimension_semantics=sem, vmem_limit_bytes=VMEM_LIMIT)


def _cast_once(grid_rank, pairs, cols=512):
    first = pl.program_id(0) == 0
    for axis in range(1, grid_rank):
        first = jnp.logical_and(first, pl.program_id(axis) == 0)

    @pl.when(first)
    def _():
        for w_ref, wb_ref in pairs:
            for c in range(0, w_ref.shape[1], cols):
                wb_ref[:, c:c + cols] = w_ref[:, c:c + cols].astype(BF16)


def _rms(x, g):
    return x * lax.rsqrt(jnp.mean(x * x, axis=-1, keepdims=True) + RMS_EPS) * g


def _dot(a, b):
    return jnp.dot(a, b, preferred_element_type=F32)


def _dot_nt(a, b):
    return lax.dot_general(a, b, (((1,), (1,)), ((), ())), preferred_element_type=F32)


def _dot_tn(a, b):
    return lax.dot_general(a, b, (((0,), (0,)), ((), ())), preferred_element_type=F32)


def _in_proj_kernel(x_ref, g_ref, w_ref, ca_ref, sa1_ref, sa2_ref, cr_ref, sr_ref, *refs):
    att_refs = refs[:len(DILATED_PATTERNS)]
    ret_ref, h_ref, acc_ref, scr_ref, scr2_ref, wb_ref = refs[len(att_refs):]
    tm = x_ref.shape[1]
    width = ATT_WIDTH
    n_att = 3
    (_, r_one), (_, r_mid), (_, r_top) = DILATED_PATTERNS
    assert r_one == 1 and r_top % r_mid == 0
    rows_per_step = 64
    _cast_once(2, [(w_ref, wb_ref)])
    h_ref[...] = _rms(x_ref[0], g_ref[...]).astype(BF16)

    def project(i, slot):
        acc_ref[slot] = _dot(h_ref[...], wb_ref[:, i * width:(i + 1) * width])

    def rope_attn(t, rows):
        return (t * ca_ref[rows, :]
                + pltpu.roll(t, LANES - ROT_DIM // 2, 1) * sa1_ref[rows, :]
                + pltpu.roll(t, ROT_DIM // 2, 1) * sa2_ref[rows, :])

    def rope_ret(t, rows):
        return t * cr_ref[rows, :] + pltpu.roll(t, RET_HEAD_DIM // 2, 1) * sr_ref[rows, :]

    def post(i, slot):
        for gi, c0 in enumerate(range(0, width, LANES)):
            cols = slice(c0, c0 + LANES)
            for r0 in range(0, tm, rows_per_step):
                rows = slice(r0, r0 + rows_per_step)
                t = acc_ref[slot, rows, cols]
                if i == 0:
                    t = rope_attn(t, rows) * (ATT_HEAD_DIM ** -0.5 * LOG2_E)
                elif i == 1:
                    t = rope_attn(t, rows)
                elif i == n_att:
                    t = rope_ret(t, rows)
                elif i == n_att + 1:
                    t = rope_ret(t, rows) * (RET_HEAD_DIM ** -0.5)
                if i >= n_att:
                    ocols = slice((i - n_att) * width + c0, (i - n_att) * width + c0 + LANES)
                    ret_ref[0, rows, ocols] = t.astype(BF16)
                else:
                    ocols = slice(i * width + c0, i * width + c0 + LANES)
                    scr_ref[i, gi, rows, :] = t
                    att_refs[0][0, 0, rows, ocols] = t.astype(BF16)
            if i < n_att:
                for c4 in range(r_mid):
                    picked = scr_ref[i, gi, pl.ds(c4, tm // r_mid, stride=r_mid), :]
                    att_refs[1][0, c4, :, ocols] = picked.astype(BF16)
                    scr2_ref[i, gi, c4] = picked
                for c4 in range(r_mid):
                    for m in range(r_top // r_mid):
                        picked = scr2_ref[i, gi, c4, pl.ds(m, tm // r_top, stride=r_top // r_mid), :]
                        att_refs[2][0, m * r_mid + c4, :, ocols] = picked.astype(BF16)

    n_streams = n_att + 4
    project(0, 0)
    for i in range(n_streams):
        if i + 1 < n_streams:
            project(i + 1, (i + 1) % 2)
        post(i, i % 2)


def _rope_tables(seq):
    f32 = np.float32
    pos = np.arange(seq, dtype=f32)[:, None]
    half = ROT_DIM // 2
    inv = f32(ROPE_THETA) ** (-(np.arange(0, ROT_DIM, 2, dtype=f32) / f32(ROT_DIM)))
    ang = pos * inv[None, :]
    cos, sin = np.cos(ang), np.sin(ang)
    one = np.ones((seq, ATT_HEAD_DIM - ROT_DIM), f32)
    zero = np.zeros((seq, ATT_HEAD_DIM - ROT_DIM), f32)
    zh = np.zeros((seq, half), f32)
    ca = np.concatenate([cos, cos, one], axis=1)
    sa1 = np.concatenate([-sin, zh, zero], axis=1)
    sa2 = np.concatenate([zh, sin, zero], axis=1)
    reps = LANES // ATT_HEAD_DIM
    ca, sa1, sa2 = (np.tile(t, (1, reps)) for t in (ca, sa1, sa2))
    inv_r = f32(RET_ROPE_THETA) ** (-(np.arange(0, RET_HEAD_DIM, 2, dtype=f32) / f32(RET_HEAD_DIM)))
    ang_r = pos * inv_r[None, :]
    cr = np.concatenate([np.cos(ang_r), np.cos(ang_r)], axis=1)
    sr = np.concatenate([-np.sin(ang_r), np.sin(ang_r)], axis=1)
    return tuple(t.astype(f32) for t in (ca, sa1, sa2, cr, sr))


def _in_proj(x, g, w_in, tm=512):
    batch, seq, d = x.shape
    width = w_in.shape[1]
    tok = lambda w: pl.BlockSpec((1, tm, w), lambda b, t: (b, t, 0))
    tab = pl.BlockSpec((tm, LANES), lambda b, t: (t, 0))
    att_shapes, att_specs = [], []
    for _, r in DILATED_PATTERNS:
        assert tm % (r * 16) == 0
        att_shapes.append(jax.ShapeDtypeStruct((batch, r, seq // r, 3 * ATT_WIDTH), BF16))
        att_specs.append(pl.BlockSpec((1, r, tm // r, 3 * ATT_WIDTH), lambda b, t: (b, 0, t, 0)))
    ret_shape = jax.ShapeDtypeStruct((batch, seq, 4 * RET_WIDTH), BF16)
    outs = pl.pallas_call(
        _in_proj_kernel,
        out_shape=tuple(att_shapes) + (ret_shape,),
        grid=(batch, seq // tm),
        in_specs=[tok(d), _resident((1, d)), _resident((d, width))] + [tab] * 5,
        out_specs=tuple(att_specs) + (tok(4 * RET_WIDTH),),
        scratch_shapes=[pltpu.VMEM((tm, d), BF16),
                        pltpu.VMEM((2, tm, ATT_WIDTH), F32),
                        pltpu.VMEM((3, ATT_WIDTH // LANES, tm, LANES), F32),
                        pltpu.VMEM((3, ATT_WIDTH // LANES, DILATED_PATTERNS[1][1],
                                    tm // DILATED_PATTERNS[1][1], LANES), F32),
                        pltpu.VMEM((d, width), BF16)],
        compiler_params=_params("arbitrary", "arbitrary"),
        name="in_proj",
    )(x, g, w_in, *_rope_tables(seq))
    return outs[:len(att_shapes)], outs[len(att_shapes)]


def _attn_kernel(*refs, nc, nq, seq_tiles, has_prev, is_last):
    refs = list(refs)
    main_ref = refs.pop(0)
    if seq_tiles > 1:
        halo_ref = refs.pop(0)
    if has_prev:
        po_ref, ps_ref = refs[:2]
        del refs[:2]
    if is_last:
        g_ref = refs.pop(0)
    o_ref = refs.pop(0)
    if not is_last:
        s_ref = refs.pop(0)
    kbuf, vbuf, p_buf, bias, onehot = refs[:5]
    del refs[:5]
    if nc > 1:
        obuf, sbuf = refs

    half = ATT_HEAD_DIM
    groups = ATT_WIDTH // LANES
    kcols = slice(ATT_WIDTH, 2 * ATT_WIDTH)
    vcols = slice(2 * ATT_WIDTH, 3 * ATT_WIDTH)
    for cls in range(nc):
        if seq_tiles > 1:
            kbuf[cls, :BAND] = halo_ref[0, cls, :, kcols]
            vbuf[cls, :BAND] = halo_ref[0, cls, :, vcols]
        else:
            kbuf[cls, :BAND] = jnp.zeros((BAND, ATT_WIDTH), BF16)
            vbuf[cls, :BAND] = jnp.zeros((BAND, ATT_WIDTH), BF16)
        kbuf[cls, BAND:] = main_ref[0, cls, :, kcols]
        vbuf[cls, BAND:] = main_ref[0, cls, :, vcols]

    key = lax.broadcasted_iota(jnp.int32, (2 * BAND, BAND), 0)
    qry = lax.broadcasted_iota(jnp.int32, (2 * BAND, BAND), 1)
    if seq_tiles == 1:
        floor = BAND
    else:
        floor = jnp.where(pl.program_id(1) == 0, BAND, 0)
    upper = key <= qry + BAND
    bias[0] = jnp.where((key >= jnp.maximum(qry, floor)) & upper, 0.0, MASKED).astype(BF16)
    bias[1] = jnp.where((key >= qry) & upper, 0.0, MASKED).astype(BF16)
    eye = (lax.broadcasted_iota(jnp.int32, (BAND, 2 * BAND), 0)
           == (lax.broadcasted_iota(jnp.int32, (BAND, 2 * BAND), 1) & (BAND - 1)))
    onehot[...] = jnp.where(eye, 1.0, 0.0).astype(BF16)

    row8 = lax.broadcasted_iota(jnp.int32, (8, BAND), 0)

    def scores(cls, qstart, slot, which):
        tops = []
        for gi, c0 in enumerate(range(0, ATT_WIDTH, LANES)):
            cols = slice(c0, c0 + LANES)
            qp = main_ref[0, cls, pl.ds(qstart, BAND), cols]
            kp = kbuf[cls, pl.ds(qstart, 2 * BAND), cols]
            qt = qp.T
            zero = jnp.zeros((half, BAND), BF16)
            q_ext = jnp.concatenate([
                jnp.concatenate([qt[:half], zero], axis=0),
                jnp.concatenate([zero, qt[half:]], axis=0)], axis=1)
            q_ext = jnp.concatenate([q_ext, onehot[...]], axis=0)
            s = _dot(jnp.concatenate([kp, bias[which]], axis=1), q_ext)
            pair = []
            for rows in (slice(0, BAND), slice(BAND, 2 * BAND)):
                sh = s[rows, :]
                parts = [sh[r8:r8 + 8, :] for r8 in range(0, BAND, 8)]
                while len(parts) > 1:
                    parts = [jnp.maximum(x, y) for x, y in zip(parts[0::2], parts[1::2])]
                m = jnp.max(parts[0], axis=0, keepdims=True)
                p_buf[slot, gi, rows, :] = jnp.exp2(sh - m).astype(BF16)
                pair.append(m)
            tops.append(pair)
        return tops

    ones_rows = jnp.ones((16, BAND), BF16)

    def finish(cls, qstart, slot, tops):
        qs = pl.ds(qstart, BAND)
        ks = pl.ds(qstart, 2 * BAND)
        if has_prev:
            prev_lse = ps_ref[0, cls, qs, :].T
        stats = jnp.zeros((8, BAND), F32)
        for gi, c0 in enumerate(range(0, ATT_WIDTH, LANES)):
            cols = slice(c0, c0 + LANES)
            m_prev, m_own = tops[gi]
            m = jnp.maximum(m_prev, m_own)
            vt = vbuf[cls, ks, cols].T
            ot = (_dot(jnp.concatenate([vt[:, :BAND], ones_rows], axis=0), p_buf[slot, gi, :BAND, :])
                  * jnp.exp2(m_prev - m)
                  + _dot(jnp.concatenate([vt[:, BAND:], ones_rows], axis=0), p_buf[slot, gi, BAND:, :])
                  * jnp.exp2(m_own - m))
            l = ot[LANES:LANES + 1, :]
            inv = 1.0 / l
            lse = m + jnp.log2(l)
            if has_prev:
                prev_o = po_ref[0, cls, qs, cols].astype(F32).T
            parts = []
            for hh in range(LANES // half):
                head = 2 * gi + hh
                qcols = slice(hh * BAND, (hh + 1) * BAND)
                rows = slice(hh * half, (hh + 1) * half)
                w_new = inv[:, qcols]
                lse_h = lse[:, qcols]
                if has_prev:
                    lp = prev_lse[head:head + 1, :]
                    top = jnp.maximum(lp, lse_h)
                    wp = jnp.exp2(lp - top)
                    wn = jnp.exp2(lse_h - top)
                    den = wp + wn
                    o = (wp / den) * prev_o[rows, :] + (wn * w_new / den) * ot[rows, qcols]
                    lse_h = top + jnp.log2(den)
                else:
                    o = ot[rows, qcols] * w_new
                if is_last:
                    ms = jnp.mean(o * o, axis=0, keepdims=True)
                    o = o * lax.rsqrt(ms + RMS_EPS)
                parts.append(o)
                stats = jnp.where(row8 == head, lse_h, stats)
            o = jnp.concatenate(parts, axis=0).T
            if is_last:
                o = o * g_ref[:, cols]
            if nc > 1:
                obuf[gi, pl.ds(qstart * nc + cls, BAND, stride=nc), :] = o
            else:
                o_ref[0, 0, qs, cols] = o.astype(o_ref.dtype)
        if not is_last:
            st = jnp.concatenate([stats, jnp.zeros((BAND - 8, BAND), F32)], axis=0).T
            if nc > 1:
                sbuf[pl.ds(qstart * nc + cls, BAND, stride=nc), :] = st
            else:
                s_ref[0, 0, qs, :] = st

    blocks = [(cls, n * BAND) for cls in range(nc) for n in range(nq)]
    which = lambda blk: 0 if blk[1] == 0 else 1
    tops = scores(*blocks[0], 0, which(blocks[0]))
    for i, (cls, qstart) in enumerate(blocks):
        if i + 1 < len(blocks):
            tops_next = scores(*blocks[i + 1], (i + 1) % 2, which(blocks[i + 1]))
        finish(cls, qstart, i % 2, tops)
        tops = tops_next
    if nc > 1:
        for gi, c0 in enumerate(range(0, ATT_WIDTH, LANES)):
            o_ref[0, 0, :, c0:c0 + LANES] = obuf[gi].astype(o_ref.dtype)
        if not is_last:
            s_ref[0, 0] = sbuf[...]


def _attention(qkv, gn_g):
    w = ATT_WIDTH
    order = sorted(range(len(DILATED_PATTERNS)), key=lambda p: -DILATED_PATTERNS[p][1])
    prev = None
    for step, p in enumerate(order):
        window, r = DILATED_PATTERNS[p]
        batch, _, length, _ = qkv[p].shape
        is_last = step == len(order) - 1
        r_next = 1 if is_last else DILATED_PATTERNS[order[step + 1]][1]
        nc = r // r_next
        assert window // r == BAND and r == nc * r_next
        nq = 2 if nc > 1 else 8
        seq_tiles = length // (nq * BAND)
        assert length % (nq * BAND) == 0
        rows = nq * BAND

        if nc > 1:
            shape5 = lambda t: t.reshape(batch, nc, r_next, length, t.shape[-1])
            main = lambda wd: pl.BlockSpec((1, nc, None, rows, wd), lambda b, t: (b, 0, t // seq_tiles, t % seq_tiles, 0))
            halo = lambda wd: pl.BlockSpec(
                (1, nc, None, BAND, wd),
                lambda b, t: (b, 0, t // seq_tiles, jnp.maximum(t % seq_tiles * nq - 1, 0), 0))
            grid = (batch, r_next * seq_tiles)
            out_blk = lambda wd: pl.BlockSpec((1, 1, nc * rows, wd), lambda b, t: (b, t // seq_tiles, t % seq_tiles, 0))
        else:
            shape5 = lambda t: t
            main = lambda wd: pl.BlockSpec((1, 1, rows, wd), lambda b, t: (b, 0, t, 0))
            halo = lambda wd: pl.BlockSpec((1, 1, BAND, wd), lambda b, t: (b, 0, jnp.maximum(t * nq - 1, 0), 0))
            grid = (batch, seq_tiles)
            out_blk = lambda wd: pl.BlockSpec((1, 1, rows, wd), lambda b, t: (b, 0, t, 0))
        if nc > 1 and seq_tiles > 1:
            assert r_next == 1

        args = [shape5(qkv[p])]
        in_specs = [main(3 * w)]
        if seq_tiles > 1:
            args.append(shape5(qkv[p]))
            in_specs.append(halo(3 * w))
        if prev is not None:
            args += [shape5(prev[0]), shape5(prev[1])]
            in_specs += [main(w), main(LANES)]
        if is_last:
            args.append(gn_g)
            in_specs.append(_resident((1, w)))
        o_shape = jax.ShapeDtypeStruct((batch, r_next, length * nc, w), BF16)
        s_shape = jax.ShapeDtypeStruct((batch, r_next, length * nc, LANES), F32)
        groups = w // LANES
        scratch = [pltpu.VMEM((nc, rows + BAND, w), BF16)] * 2 + [
            pltpu.VMEM((2, groups, 2 * BAND, 2 * BAND), BF16),
            pltpu.VMEM((2, 2 * BAND, BAND), BF16),
            pltpu.VMEM((BAND, 2 * BAND), BF16)]
        if nc > 1:
            scratch += [pltpu.VMEM((w // LANES, nc * rows, LANES), F32), pltpu.VMEM((nc * rows, LANES), F32)]
        res = pl.pallas_call(
            functools.partial(_attn_kernel, nc=nc, nq=nq, seq_tiles=seq_tiles,
                              has_prev=prev is not None, is_last=is_last),
            out_shape=o_shape if is_last else (o_shape, s_shape),
            grid=grid,
            in_specs=in_specs,
            out_specs=out_blk(w) if is_last else (out_blk(w), out_blk(LANES)),
            scratch_shapes=scratch,
            compiler_params=_params("parallel", "arbitrary"),
            name=f"attn_r{r}",
        )(*args)
        if is_last:
            return res.reshape(batch, length * nc, w)
        prev = res


def _retention_kernel(in_ref, dmask_ref, xi_ref, zeta_ref, cdec_ref, g_ref, *refs, n_cast):
    cast_in = refs[:n_cast]
    o_ref = refs[n_cast]
    cast_out = refs[n_cast + 1:2 * n_cast + 1]
    state_ref, raw_ref, upd_ref, pre_ref = refs[2 * n_cast + 1:]

    @pl.when(pl.program_id(1) == 0)
    def _():
        state_ref[...] = jnp.zeros_like(state_ref)

    for src, dst in zip(cast_in, cast_out):
        dst[...] = src[...].astype(BF16)

    chunk = RET_CHUNK

    def where(t, h):
        return slice(t, t + chunk), slice(h * RET_HEAD_DIM, (h + 1) * RET_HEAD_DIM)

    def stream(j, t, h):
        c0 = j * RET_WIDTH + h * RET_HEAD_DIM
        return in_ref[0, t:t + chunk, c0:c0 + RET_HEAD_DIM]

    def scores(t, h, slot):
        k = stream(1, t, h)
        raw_ref[slot] = _dot_nt(stream(0, t, h), k)
        kz = (k.astype(F32) * zeta_ref[h]).astype(BF16)
        upd_ref[slot] = _dot_tn(kz, stream(2, t, h))

    def mix(t, h, slot):
        inner = (raw_ref[slot] * dmask_ref[h]).astype(BF16)
        state = state_ref[h]
        pre_ref[slot] = (_dot(inner, stream(2, t, h))
                         + _dot(stream(0, t, h), state.astype(BF16)) * xi_ref[h])
        state_ref[h] = state * cdec_ref[h] + upd_ref[slot]

    def finish(t, h, slot):
        rows, c = where(t, h)
        o = pre_ref[slot]
        mu = jnp.mean(o, axis=-1, keepdims=True)
        var = jnp.mean(jnp.square(o - mu), axis=-1, keepdims=True)
        y = (o - mu) * lax.rsqrt(var + GN_EPS) * g_ref[:, c]
        gate = stream(3, t, h).astype(F32)
        o_ref[0, rows, c] = (gate * (1.0 / (1.0 + jnp.exp(-gate))) * y).astype(o_ref.dtype)

    units = [(t, h) for t in range(0, in_ref.shape[1], chunk) for h in range(RET_HEADS)]
    scores(*units[0], 0)
    for i, unit in enumerate(units):
        if i + 1 < len(units):
            scores(*units[i + 1], (i + 1) % 2)
        mix(*unit, i % 2)
        if i > 0:
            finish(*units[i - 1], (i - 1) % 2)
    finish(*units[-1], (len(units) - 1) % 2)


def _retention_tables():
    f32 = np.float32
    c = RET_CHUNK
    log_g = np.log(f32(1.0) - f32(2.0) ** (f32(-5.0) - np.arange(RET_HEADS, dtype=f32))).astype(f32)
    n = np.arange(c, dtype=f32)
    rel = n[:, None] - n[None, :]
    dmask = np.where(rel >= 0, np.exp(log_g[:, None, None] * np.maximum(rel, f32(0.0))), f32(0.0))
    xi = np.exp(log_g[:, None] * (n + f32(1.0)))
    zeta = np.exp(log_g[:, None] * (f32(c - 1.0) - n))
    cdec = np.exp(log_g * f32(c))
    wide = lambda t: np.ascontiguousarray(np.broadcast_to(t[:, :, None], (RET_HEADS, c, RET_HEAD_DIM)), f32)
    cdec = np.ascontiguousarray(np.broadcast_to(cdec[:, None, None], (RET_HEADS, RET_HEAD_DIM, RET_HEAD_DIM)), f32)
    return dmask.astype(f32), wide(xi), wide(zeta), cdec


def _retention(qkvg, gn_g, to_cast=(), chunks_per_step=4):
    batch, seq, _ = qkvg.shape
    width = RET_WIDTH
    c = RET_CHUNK
    rows = c * chunks_per_step
    steps_per_batch = seq // rows
    n_steps = batch * steps_per_batch
    blk = lambda wd: pl.BlockSpec((1, rows, wd), lambda b, n: (b, n, 0))
    tab = _resident((RET_HEADS, c, RET_HEAD_DIM))
    cast_specs, cast_shapes = [], []
    for w in to_cast:
        nb = max(k for k in range(1, n_steps + 1)
                 if n_steps % k == 0 and w.shape[0] % k == 0 and (w.shape[0] // k) % 16 == 0)
        hold = n_steps // nb
        cast_specs.append(pl.BlockSpec((w.shape[0] // nb, w.shape[1]),
                                       lambda b, n, hold=hold: ((b * steps_per_batch + n) // hold, 0)))
        cast_shapes.append(jax.ShapeDtypeStruct(w.shape, BF16))
    res = pl.pallas_call(
        functools.partial(_retention_kernel, n_cast=len(to_cast)),
        out_shape=(jax.ShapeDtypeStruct((batch, seq, width), BF16), *cast_shapes),
        grid=(batch, seq // rows),
        in_specs=[blk(4 * width)] + [tab] * 4 + [_resident((1, width))] + cast_specs,
        out_specs=(blk(width), *cast_specs),
        scratch_shapes=[pltpu.VMEM((RET_HEADS, RET_HEAD_DIM, RET_HEAD_DIM), F32),
                        pltpu.VMEM((2, c, c), F32),
                        pltpu.VMEM((2, RET_HEAD_DIM, RET_HEAD_DIM), F32),
                        pltpu.VMEM((2, c, RET_HEAD_DIM), F32)],
        compiler_params=_params("arbitrary", "arbitrary"),
        name="retention",
    )(qkvg, *_retention_tables(), gn_g, *to_cast)
    return res[0], res[1:]


def _mem_kv_kernel(mem_ref, g_ref, w32_ref, k_ref, v_ref, w_ref):
    d = mem_ref.shape[-1]
    _cast_once(1, [(w32_ref, w_ref)])
    kv = _dot(_rms(mem_ref[0], g_ref[...]).astype(BF16), w_ref[...])
    k_ref[0] = kv[:, :d].astype(BF16)
    v_ref[0] = kv[:, d:].astype(BF16)


def _mem_kv(mem, g, w_kv):
    batch, n_mem, d = mem.shape
    blk = pl.BlockSpec((1, n_mem, d), lambda b: (b, 0, 0))
    out = jax.ShapeDtypeStruct((batch, n_mem, d), BF16)
    return pl.pallas_call(
        _mem_kv_kernel,
        out_shape=(out, out),
        grid=(batch,),
        in_specs=[blk, _resident((1, d)), _resident((d, 2 * d))],
        out_specs=(blk, blk),
        scratch_shapes=[pltpu.VMEM((d, 2 * d), BF16)],
        compiler_params=_params("arbitrary"),
        name="mem_kv",
    )(mem, g, w_kv)


def _mix_out_kernel(a_ref, r_ref, x_ref, wout_ref, gpost_ref, gpre_ref, wq_ref, mk_ref, mv_ref,
                    wo_ref, gpost2_ref, o_ref, x1_ref, h_ref, q_ref, att_ref, *, sub):
    half = a_ref.shape[-1]
    d = x_ref.shape[-1]
    dh = d // XATT_HEADS

    def out_proj(rows):
        y = _dot(a_ref[0, rows, :], wout_ref[:half, :]) + _dot(r_ref[0, rows, :], wout_ref[half:, :])
        x1 = x_ref[0, rows, :] + _rms(y, gpost_ref[...])
        x1_ref[rows, :] = x1
        h_ref[rows, :] = _rms(x1, gpre_ref[...]).astype(BF16)

    def q_proj(rows):
        q_ref[rows, :] = (_dot(h_ref[rows, :], wq_ref[...]) * (dh ** -0.5)).astype(BF16)

    def cross_attn(rows):
        for hd in range(XATT_HEADS):
            c = slice(hd * dh, (hd + 1) * dh)
            s = _dot_nt(q_ref[rows, c], mk_ref[0, :, c])
            p = jnp.exp(s - jnp.max(s, axis=-1, keepdims=True))
            l = jnp.sum(p, axis=-1, keepdims=True)
            att_ref[rows, c] = (_dot(p.astype(BF16), mv_ref[0, :, c]) / l).astype(BF16)

    def o_proj(rows):
        y2 = _dot(att_ref[rows, :], wo_ref[...])
        o_ref[0, rows, :] = x1_ref[rows, :] + _rms(y2, gpost2_ref[...])

    tiles = [slice(r0, r0 + sub) for r0 in range(0, x_ref.shape[1], sub)]
    for stage in (out_proj, q_proj, cross_attn, o_proj):
        for rows in tiles:
            stage(rows)


def _mix_out(a, r, x, w_out, g_post, g_pre, w_q, mem_k, mem_v, w_o, g_post2, tm=1024, sub=256):
    batch, seq, d = x.shape
    half = a.shape[-1]
    n_mem = mem_k.shape[1]
    tok = lambda w: pl.BlockSpec((1, tm, w), lambda b, t: (b, t, 0))
    memblk = pl.BlockSpec((1, n_mem, d), lambda b, t: (b, 0, 0))
    vec = _resident((1, d))
    mat = _resident((d, d))
    return pl.pallas_call(
        functools.partial(_mix_out_kernel, sub=sub),
        out_shape=jax.ShapeDtypeStruct((batch, seq, d), F32),
        grid=(batch, seq // tm),
        in_specs=[tok(half), tok(half), tok(d), mat, vec, vec, mat, memblk, memblk, mat, vec],
        out_specs=tok(d),
        scratch_shapes=[pltpu.VMEM((tm, d), F32),
                        pltpu.VMEM((tm, d), BF16),
                        pltpu.VMEM((tm, d), BF16),
                        pltpu.VMEM((tm, d), BF16)],
        compiler_params=_params("parallel", "parallel"),
        name="mix_out",
    )(a, r, x, w_out, g_post, g_pre, w_q, mem_k, mem_v, w_o, g_post2)


def _ffn_kernel(x_ref, gpre_ref, wgu_ref, wdown_ref, gpost_ref, o_ref, h_ref, y_ref, *, chunks, sub):
    hidden = wdown_ref.shape[0]

    def norm(rows):
        h_ref[rows, :] = _rms(x_ref[rows, :], gpre_ref[...]).astype(BF16)

    def hidden_chunk(ci, rows):
        lo, hi = chunks[ci]
        h = h_ref[rows, :]
        g = _dot(h, wgu_ref[:, lo:hi])
        u = _dot(h, wgu_ref[:, hidden + lo:hidden + hi])
        act = (g * (1.0 / (1.0 + jnp.exp(-g))) * u).astype(BF16)
        part = _dot(act, wdown_ref[lo:hi, :])
        if ci == 0:
            y_ref[rows, :] = part
        else:
            y_ref[rows, :] += part

    def final(rows):
        o_ref[rows, :] = x_ref[rows, :] + _rms(y_ref[rows, :], gpost_ref[...])

    tiles = [slice(r0, r0 + sub) for r0 in range(0, x_ref.shape[0], sub)]
    for rows in tiles:
        norm(rows)
    for ci in range(len(chunks)):
        for rows in tiles:
            hidden_chunk(ci, rows)
    for rows in tiles:
        final(rows)


def _ffn(x2d, g_pre, w_gu, w_down, g_post, tm=1024, sub=256):
    tokens, d = x2d.shape
    hidden = w_down.shape[0]
    mxu = 256
    cut = (hidden // 2) // mxu * mxu
    chunks = ((0, cut), (cut, hidden))
    row = pl.BlockSpec((tm, d), lambda t: (t, 0))
    vec = _resident((1, d))
    return pl.pallas_call(
        functools.partial(_ffn_kernel, chunks=chunks, sub=sub),
        out_shape=jax.ShapeDtypeStruct((tokens, d), F32),
        grid=(tokens // tm,),
        in_specs=[row, vec, _resident((d, 2 * hidden)), _resident((hidden, d)), vec],
        out_specs=row,
        scratch_shapes=[pltpu.VMEM((tm, d), BF16),
                        pltpu.VMEM((tm, d), F32)],
        compiler_params=_params("parallel"),
        name="ffn",
    )(x2d, g_pre, w_gu, w_down, g_post)


def kernel(x, mem, pre_mix_g, post_mix_g, w_in, attn_gn_g, ret_gn_g, w_out,
           pre_mem_g, post_mem_g, mem_norm_g, w_q_mem, w_kv_mem, w_o_mem,
           pre_ffn_g, post_ffn_g, w_gate_up, w_down):
    batch, seq, d = x.shape
    depth = w_in.shape[0]
    for l in range(depth):
        vec = lambda g: g[l][None, :]
        qkv, ret_in = _in_proj(x, vec(pre_mix_g), w_in[l])
        a = _attention(qkv, vec(attn_gn_g))
        r, (w_gu, w_dn, w_o1, w_q, w_o2) = _retention(
            ret_in, vec(ret_gn_g),
            to_cast=(w_gate_up[l], w_down[l], w_out[l], w_q_mem[l], w_o_mem[l]))
        mem_k, mem_v = _mem_kv(mem, vec(mem_norm_g), w_kv_mem[l])
        x = _mix_out(a, r, x, w_o1, vec(post_mix_g), vec(pre_mem_g), w_q,
                     mem_k, mem_v, w_o2, vec(post_mem_g))
        x = _ffn(x.reshape(batch * seq, d), vec(pre_ffn_g), w_gu, w_dn,
                 vec(post_ffn_g)).reshape(batch, seq, d)
    return x
```

```python
import functools

import numpy as np
import jax
import jax.numpy as jnp
from jax import lax
from jax.experimental import pallas as pl
from jax.experimental.pallas import tpu as pltpu

F32 = jnp.float32
BF16 = jnp.bfloat16

RMS_EPS = 1e-6
GN_EPS = 1e-5
ATT_HEADS = 8
ATT_HEAD_DIM = 64
ATT_WIDTH = ATT_HEADS * ATT_HEAD_DIM
ROT_DIM = ATT_HEAD_DIM // 4
ROPE_THETA = 500000.0
DILATED_PATTERNS = ((128, 1), (512, 4), (2048, 16))
RET_HEADS = 4
RET_HEAD_DIM = 128
RET_WIDTH = RET_HEADS * RET_HEAD_DIM
RET_CHUNK = 128
RET_ROPE_THETA = 10000.0
XATT_HEADS = 4

LANES = 128
BAND = 128
MASKED = -1e30
LOG2_E = 1.4426950408889634
VMEM_LIMIT = 56 * 1024 * 1024


def _resident(shape):
    zeros = (0,) * len(shape)
    return pl.BlockSpec(shape, lambda *_: zeros, pipeline_mode=pl.Buffered(1))


def _params(*sem):
    return pltpu.CompilerParams(dimension_semantics=sem, vmem_limit_bytes=VMEM_LIMIT)


def _cast_once(grid_rank, pairs, cols=512):
    first = pl.program_id(0) == 0
    for axis in range(1, grid_rank):
        first = jnp.logical_and(first, pl.program_id(axis) == 0)

    @pl.when(first)
    def _():
        for w_ref, wb_ref in pairs:
            for c in range(0, w_ref.shape[1], cols):
                wb_ref[:, c:c + cols] = w_ref[:, c:c + cols].astype(BF16)


def _rms(x, g):
    return x * lax.rsqrt(jnp.mean(x * x, axis=-1, keepdims=True) + RMS_EPS) * g


def _dot(a, b):
    return jnp.dot(a, b, preferred_element_type=F32)


def _dot_nt(a, b):
    return lax.dot_general(a, b, (((1,), (1,)), ((), ())), preferred_element_type=F32)


def _dot_tn(a, b):
    return lax.dot_general(a, b, (((0,), (0,)), ((), ())), preferred_element_type=F32)


def _in_proj_kernel(x_ref, g_ref, w_ref, ca_ref, sa1_ref, sa2_ref, cr_ref, sr_ref, *refs):
    att_refs = refs[:len(DILATED_PATTERNS)]
    ret_ref, h_ref, acc_ref, scr_ref, scr2_ref, wb_ref = refs[len(att_refs):]
    tm = x_ref.shape[1]
    width = ATT_WIDTH
    n_att = 3
    (_, r_one), (_, r_mid), (_, r_top) = DILATED_PATTERNS
    assert r_one == 1 and r_top % r_mid == 0
    rows_per_step = 64
    _cast_once(2, [(w_ref, wb_ref)])
    h_ref[...] = _rms(x_ref[0], g_ref[...]).astype(BF16)

    def project(i, slot):
        acc_ref[slot] = _dot(h_ref[...], wb_ref[:, i * width:(i + 1) * width])

    def rope_attn(t, rows):
        return (t * ca_ref[rows, :]
                + pltpu.roll(t, LANES - ROT_DIM // 2, 1) * sa1_ref[rows, :]
                + pltpu.roll(t, ROT_DIM // 2, 1) * sa2_ref[rows, :])

    def rope_ret(t, rows):
        return t * cr_ref[rows, :] + pltpu.roll(t, RET_HEAD_DIM // 2, 1) * sr_ref[rows, :]

    def post(i, slot):
        for gi, c0 in enumerate(range(0, width, LANES)):
            cols = slice(c0, c0 + LANES)
            for r0 in range(0, tm, rows_per_step):
                rows = slice(r0, r0 + rows_per_step)
                t = acc_ref[slot, rows, cols]
                if i == 0:
                    t = rope_attn(t, rows) * (ATT_HEAD_DIM ** -0.5 * LOG2_E)
                elif i == 1:
                    t = rope_attn(t, rows)
                elif i == n_att:
                    t = rope_ret(t, rows)
                elif i == n_att + 1:
                    t = rope_ret(t, rows) * (RET_HEAD_DIM ** -0.5)
                if i >= n_att:
                    ocols = slice((i - n_att) * width + c0, (i - n_att) * width + c0 + LANES)
                    ret_ref[0, rows, ocols] = t.astype(BF16)
                else:
                    ocols = slice(i * width + c0, i * width + c0 + LANES)
                    scr_ref[i, gi, rows, :] = t
                    att_refs[0][0, 0, rows, ocols] = t.astype(BF16)
            if i < n_att:
                for c4 in range(r_mid):
                    picked = scr_ref[i, gi, pl.ds(c4, tm // r_mid, stride=r_mid), :]
                    att_refs[1][0, c4, :, ocols] = picked.astype(BF16)
                    scr2_ref[i, gi, c4] = picked
                for c4 in range(r_mid):
                    for m in range(r_top // r_mid):
                        picked = scr2_ref[i, gi, c4, pl.ds(m, tm // r_top, stride=r_top // r_mid), :]
                        att_refs[2][0, m * r_mid + c4, :, ocols] = picked.astype(BF16)

    n_streams = n_att + 4
    project(0, 0)
    for i in range(n_streams):
        if i + 1 < n_streams:
            project(i + 1, (i + 1) % 2)
        post(i, i % 2)


def _rope_tables(seq):
    f32 = np.float32
    pos = np.arange(seq, dtype=f32)[:, None]
    half = ROT_DIM // 2
    inv = f32(ROPE_THETA) ** (-(np.arange(0, ROT_DIM, 2, dtype=f32) / f32(ROT_DIM)))
    ang = pos * inv[None, :]
    cos, sin = np.cos(ang), np.sin(ang)
    one = np.ones((seq, ATT_HEAD_DIM - ROT_DIM), f32)
    zero = np.zeros((seq, ATT_HEAD_DIM - ROT_DIM), f32)
    zh = np.zeros((seq, half), f32)
    ca = np.concatenate([cos, cos, one], axis=1)
    sa1 = np.concatenate([-sin, zh, zero], axis=1)
    sa2 = np.concatenate([zh, sin, zero], axis=1)
    reps = LANES // ATT_HEAD_DIM
    ca, sa1, sa2 = (np.tile(t, (1, reps)) for t in (ca, sa1, sa2))
    inv_r = f32(RET_ROPE_THETA) ** (-(np.arange(0, RET_HEAD_DIM, 2, dtype=f32) / f32(RET_HEAD_DIM)))
    ang_r = pos * inv_r[None, :]
    cr = np.concatenate([np.cos(ang_r), np.cos(ang_r)], axis=1)
    sr = np.concatenate([-np.sin(ang_r), np.sin(ang_r)], axis=1)
    return tuple(t.astype(f32) for t in (ca, sa1, sa2, cr, sr))


def _in_proj(x, g, w_in, tm=512):
    batch, seq, d = x.shape
    width = w_in.shape[1]
    tok = lambda w: pl.BlockSpec((1, tm, w), lambda b, t: (b, t, 0))
    tab = pl.BlockSpec((tm, LANES), lambda b, t: (t, 0))
    att_shapes, att_specs = [], []
    for _, r in DILATED_PATTERNS:
        assert tm % (r * 16) == 0
        att_shapes.append(jax.ShapeDtypeStruct((batch, r, seq // r, 3 * ATT_WIDTH), BF16))
        att_specs.append(pl.BlockSpec((1, r, tm // r, 3 * ATT_WIDTH), lambda b, t: (b, 0, t, 0)))
    ret_shape = jax.ShapeDtypeStruct((batch, seq, 4 * RET_WIDTH), BF16)
    outs = pl.pallas_call(
        _in_proj_kernel,
        out_shape=tuple(att_shapes) + (ret_shape,),
        grid=(batch, seq // tm),
        in_specs=[tok(d), _resident((1, d)), _resident((d, width))] + [tab] * 5,
        out_specs=tuple(att_specs) + (tok(4 * RET_WIDTH),),
        scratch_shapes=[pltpu.VMEM((tm, d), BF16),
                        pltpu.VMEM((2, tm, ATT_WIDTH), F32),
                        pltpu.VMEM((3, ATT_WIDTH // LANES, tm, LANES), F32),
                        pltpu.VMEM((3, ATT_WIDTH // LANES, DILATED_PATTERNS[1][1],
                                    tm // DILATED_PATTERNS[1][1], LANES), F32),
                        pltpu.VMEM((d, width), BF16)],
        compiler_params=_params("arbitrary", "arbitrary"),
        name="in_proj",
    )(x, g, w_in, *_rope_tables(seq))
    return outs[:len(att_shapes)], outs[len(att_shapes)]


def _attn_kernel(*refs, nc, nq, seq_tiles, has_prev, is_last):
    refs = list(refs)
    main_ref = refs.pop(0)
    if seq_tiles > 1:
        halo_ref = refs.pop(0)
    if has_prev:
        po_ref, ps_ref = refs[:2]
        del refs[:2]
    if is_last:
        g_ref = refs.pop(0)
    o_ref = refs.pop(0)
    if not is_last:
        s_ref = refs.pop(0)
    kbuf, vbuf, p_buf, bias, onehot = refs[:5]
    del refs[:5]
    if nc > 1:
        obuf, sbuf = refs

    half = ATT_HEAD_DIM
    groups = ATT_WIDTH // LANES
    kcols = slice(ATT_WIDTH, 2 * ATT_WIDTH)
    vcols = slice(2 * ATT_WIDTH, 3 * ATT_WIDTH)
    for cls in range(nc):
        if seq_tiles > 1:
            kbuf[cls, :BAND] = halo_ref[0, cls, :, kcols]
            vbuf[cls, :BAND] = halo_ref[0, cls, :, vcols]
        else:
            kbuf[cls, :BAND] = jnp.zeros((BAND, ATT_WIDTH), BF16)
            vbuf[cls, :BAND] = jnp.zeros((BAND, ATT_WIDTH), BF16)
        kbuf[cls, BAND:] = main_ref[0, cls, :, kcols]
        vbuf[cls, BAND:] = main_ref[0, cls, :, vcols]

    key = lax.broadcasted_iota(jnp.int32, (2 * BAND, BAND), 0)
    qry = lax.broadcasted_iota(jnp.int32, (2 * BAND, BAND), 1)
    if seq_tiles == 1:
        floor = BAND
    else:
        floor = jnp.where(pl.program_id(1) == 0, BAND, 0)
    upper = key <= qry + BAND
    bias[0] = jnp.where((key >= jnp.maximum(qry, floor)) & upper, 0.0, MASKED).astype(BF16)
    bias[1] = jnp.where((key >= qry) & upper, 0.0, MASKED).astype(BF16)
    eye = (lax.broadcasted_iota(jnp.int32, (BAND, 2 * BAND), 0)
           == (lax.broadcasted_iota(jnp.int32, (BAND, 2 * BAND), 1) & (BAND - 1)))
    onehot[...] = jnp.where(eye, 1.0, 0.0).astype(BF16)

    row8 = lax.broadcasted_iota(jnp.int32, (8, BAND), 0)

    def scores(cls, qstart, slot, which):
        tops = []
        for gi, c0 in enumerate(range(0, ATT_WIDTH, LANES)):
            cols = slice(c0, c0 + LANES)
            qp = main_ref[0, cls, pl.ds(qstart, BAND), cols]
            kp = kbuf[cls, pl.ds(qstart, 2 * BAND), cols]
            qt = qp.T
            zero = jnp.zeros((half, BAND), BF16)
            q_ext = jnp.concatenate([
                jnp.concatenate([qt[:half], zero], axis=0),
                jnp.concatenate([zero, qt[half:]], axis=0)], axis=1)
            q_ext = jnp.concatenate([q_ext, onehot[...]], axis=0)
            s = _dot(jnp.concatenate([kp, bias[which]], axis=1), q_ext)
            pair = []
            for rows in (slice(0, BAND), slice(BAND, 2 * BAND)):
                sh = s[rows, :]
                parts = [sh[r8:r8 + 8, :] for r8 in range(0, BAND, 8)]
                while len(parts) > 1:
                    parts = [jnp.maximum(x, y) for x, y in zip(parts[0::2], parts[1::2])]
                m = jnp.max(parts[0], axis=0, keepdims=True)
                p_buf[slot, gi, rows, :] = jnp.exp2(sh - m).astype(BF16)
                pair.append(m)
            tops.append(pair)
        return tops

    ones_rows = jnp.ones((16, BAND), BF16)

    def finish(cls, qstart, slot, tops):
        qs = pl.ds(qstart, BAND)
        ks = pl.ds(qstart, 2 * BAND)
        if has_prev:
            prev_lse = ps_ref[0, cls, qs, :].T
        stats = jnp.zeros((8, BAND), F32)
        for gi, c0 in enumerate(range(0, ATT_WIDTH, LANES)):
            cols = slice(c0, c0 + LANES)
            m_prev, m_own = tops[gi]
            m = jnp.maximum(m_prev, m_own)
            vt = vbuf[cls, ks, cols].T
            ot = (_dot(jnp.concatenate([vt[:, :BAND], ones_rows], axis=0), p_buf[slot, gi, :BAND, :])
                  * jnp.exp2(m_prev - m)
                  + _dot(jnp.concatenate([vt[:, BAND:], ones_rows], axis=0), p_buf[slot, gi, BAND:, :])
                  * jnp.exp2(m_own - m))
            l = ot[LANES:LANES + 1, :]
            inv = 1.0 / l
            lse = m + jnp.log2(l)
            if has_prev:
                prev_o = po_ref[0, cls, qs, cols].astype(F32).T
            parts = []
            for hh in range(LANES // half):
                head = 2 * gi + hh
                qcols = slice(hh * BAND, (hh + 1) * BAND)
                rows = slice(hh * half, (hh + 1) * half)
                w_new = inv[:, qcols]
                lse_h = lse[:, qcols]
                if has_prev:
                    lp = prev_lse[head:head + 1, :]
                    top = jnp.maximum(lp, lse_h)
                    wp = jnp.exp2(lp - top)
                    wn = jnp.exp2(lse_h - top)
                    den = wp + wn
                    o = (wp / den) * prev_o[rows, :] + (wn * w_new / den) * ot[rows, qcols]
                    lse_h = top + jnp.log2(den)
                else:
                    o = ot[rows, qcols] * w_new
                if is_last:
                    ms = jnp.mean(o * o, axis=0, keepdims=True)
                    o = o * lax.rsqrt(ms + RMS_EPS)
                parts.append(o)
                stats = jnp.where(row8 == head, lse_h, stats)
            o = jnp.concatenate(parts, axis=0).T
            if is_last:
                o = o * g_ref[:, cols]
            if nc > 1:
                obuf[gi, pl.ds(qstart * nc + cls, BAND, stride=nc), :] = o
            else:
                o_ref[0, 0, qs, cols] = o.astype(o_ref.dtype)
        if not is_last:
            st = jnp.concatenate([stats, jnp.zeros((BAND - 8, BAND), F32)], axis=0).T
            if nc > 1:
                sbuf[pl.ds(qstart * nc + cls, BAND, stride=nc), :] = st
            else:
                s_ref[0, 0, qs, :] = st

    blocks = [(cls, n * BAND) for cls in range(nc) for n in range(nq)]
    which = lambda blk: 0 if blk[1] == 0 else 1
    tops = scores(*blocks[0], 0, which(blocks[0]))
    for i, (cls, qstart) in enumerate(blocks):
        if i + 1 < len(blocks):
            tops_next = scores(*blocks[i + 1], (i + 1) % 2, which(blocks[i + 1]))
        finish(cls, qstart, i % 2, tops)
        tops = tops_next
    if nc > 1:
        for gi, c0 in enumerate(range(0, ATT_WIDTH, LANES)):
            o_ref[0, 0, :, c0:c0 + LANES] = obuf[gi].astype(o_ref.dtype)
        if not is_last:
            s_ref[0, 0] = sbuf[...]


def _attention(qkv, gn_g):
    w = ATT_WIDTH
    order = sorted(range(len(DILATED_PATTERNS)), key=lambda p: -DILATED_PATTERNS[p][1])
    prev = None
    for step, p in enumerate(order):
        window, r = DILATED_PATTERNS[p]
        batch, _, length, _ = qkv[p].shape
        is_last = step == len(order) - 1
        r_next = 1 if is_last else DILATED_PATTERNS[order[step + 1]][1]
        nc = r // r_next
        assert window // r == BAND and r == nc * r_next
        nq = 2 if nc > 1 else 8
        seq_tiles = length // (nq * BAND)
        assert length % (nq * BAND) == 0
        rows = nq * BAND

        if nc > 1:
            shape5 = lambda t: t.reshape(batch, nc, r_next, length, t.shape[-1])
            main = lambda wd: pl.BlockSpec((1, nc, None, rows, wd), lambda b, t: (b, 0, t // seq_tiles, t % seq_tiles, 0))
            halo = lambda wd: pl.BlockSpec(
                (1, nc, None, BAND, wd),
                lambda b, t: (b, 0, t // seq_tiles, jnp.maximum(t % seq_tiles * nq - 1, 0), 0))
            grid = (batch, r_next * seq_tiles)
            out_blk = lambda wd: pl.BlockSpec((1, 1, nc * rows, wd), lambda b, t: (b, t // seq_tiles, t % seq_tiles, 0))
        else:
            shape5 = lambda t: t
            main = lambda wd: pl.BlockSpec((1, 1, rows, wd), lambda b, t: (b, 0, t, 0))
            halo = lambda wd: pl.BlockSpec((1, 1, BAND, wd), lambda b, t: (b, 0, jnp.maximum(t * nq - 1, 0), 0))
            grid = (batch, seq_tiles)
            out_blk = lambda wd: pl.BlockSpec((1, 1, rows, wd), lambda b, t: (b, 0, t, 0))
        if nc > 1 and seq_tiles > 1:
            assert r_next == 1

        args = [shape5(qkv[p])]
        in_specs = [main(3 * w)]
        if seq_tiles > 1:
            args.append(shape5(qkv[p]))
            in_specs.append(halo(3 * w))
        if prev is not None:
            args += [shape5(prev[0]), shape5(prev[1])]
            in_specs += [main(w), main(LANES)]
        if is_last:
            args.append(gn_g)
            in_specs.append(_resident((1, w)))
        o_shape = jax.ShapeDtypeStruct((batch, r_next, length * nc, w), BF16)
        s_shape = jax.ShapeDtypeStruct((batch, r_next, length * nc, LANES), F32)
        groups = w // LANES
        scratch = [pltpu.VMEM((nc, rows + BAND, w), BF16)] * 2 + [
            pltpu.VMEM((2, groups, 2 * BAND, 2 * BAND), BF16),
            pltpu.VMEM((2, 2 * BAND, BAND), BF16),
            pltpu.VMEM((BAND, 2 * BAND), BF16)]
        if nc > 1:
            scratch += [pltpu.VMEM((w // LANES, nc * rows, LANES), F32), pltpu.VMEM((nc * rows, LANES), F32)]
        res = pl.pallas_call(
            functools.partial(_attn_kernel, nc=nc, nq=nq, seq_tiles=seq_tiles,
                              has_prev=prev is not None, is_last=is_last),
            out_shape=o_shape if is_last else (o_shape, s_shape),
            grid=grid,
            in_specs=in_specs,
            out_specs=out_blk(w) if is_last else (out_blk(w), out_blk(LANES)),
            scratch_shapes=scratch,
            compiler_params=_params("parallel", "arbitrary"),
            name=f"attn_r{r}",
        )(*args)
        if is_last:
            return res.reshape(batch, length * nc, w)
        prev = res


def _retention_kernel(in_ref, dmask_ref, xi_ref, zeta_ref, cdec_ref, g_ref, *refs, n_cast):
    cast_in = refs[:n_cast]
    o_ref = refs[n_cast]
    cast_out = refs[n_cast + 1:2 * n_cast + 1]
    state_ref, raw_ref, upd_ref, pre_ref = refs[2 * n_cast + 1:]

    @pl.when(pl.program_id(1) == 0)
    def _():
        state_ref[...] = jnp.zeros_like(state_ref)

    for src, dst in zip(cast_in, cast_out):
        dst[...] = src[...].astype(BF16)

    chunk = RET_CHUNK

    def where(t, h):
        return slice(t, t + chunk), slice(h * RET_HEAD_DIM, (h + 1) * RET_HEAD_DIM)

    def stream(j, t, h):
        c0 = j * RET_WIDTH + h * RET_HEAD_DIM
        return in_ref[0, t:t + chunk, c0:c0 + RET_HEAD_DIM]

    def scores(t, h, slot):
        k = stream(1, t, h)
        raw_ref[slot] = _dot_nt(stream(0, t, h), k)
        kz = (k.astype(F32) * zeta_ref[h]).astype(BF16)
        upd_ref[slot] = _dot_tn(kz, stream(2, t, h))

    def mix(t, h, slot):
        inner = (raw_ref[slot] * dmask_ref[h]).astype(BF16)
        state = state_ref[h]
        pre_ref[slot] = (_dot(inner, stream(2, t, h))
                         + _dot(stream(0, t, h), state.astype(BF16)) * xi_ref[h])
        state_ref[h] = state * cdec_ref[h] + upd_ref[slot]

    def finish(t, h, slot):
        rows, c = where(t, h)
        o = pre_ref[slot]
        mu = jnp.mean(o, axis=-1, keepdims=True)
        var = jnp.mean(jnp.square(o - mu), axis=-1, keepdims=True)
        y = (o - mu) * lax.rsqrt(var + GN_EPS) * g_ref[:, c]
        gate = stream(3, t, h).astype(F32)
        o_ref[0, rows, c] = (gate * (1.0 / (1.0 + jnp.exp(-gate))) * y).astype(o_ref.dtype)

    units = [(t, h) for t in range(0, in_ref.shape[1], chunk) for h in range(RET_HEADS)]
    scores(*units[0], 0)
    for i, unit in enumerate(units):
        if i + 1 < len(units):
            scores(*units[i + 1], (i + 1) % 2)
        mix(*unit, i % 2)
        if i > 0:
            finish(*units[i - 1], (i - 1) % 2)
    finish(*units[-1], (len(units) - 1) % 2)


def _retention_tables():
    f32 = np.float32
    c = RET_CHUNK
    log_g = np.log(f32(1.0) - f32(2.0) ** (f32(-5.0) - np.arange(RET_HEADS, dtype=f32))).astype(f32)
    n = np.arange(c, dtype=f32)
    rel = n[:, None] - n[None, :]
    dmask = np.where(rel >= 0, np.exp(log_g[:, None, None] * np.maximum(rel, f32(0.0))), f32(0.0))
    xi = np.exp(log_g[:, None] * (n + f32(1.0)))
    zeta = np.exp(log_g[:, None] * (f32(c - 1.0) - n))
    cdec = np.exp(log_g * f32(c))
    wide = lambda t: np.ascontiguousarray(np.broadcast_to(t[:, :, None], (RET_HEADS, c, RET_HEAD_DIM)), f32)
    cdec = np.ascontiguousarray(np.broadcast_to(cdec[:, None, None], (RET_HEADS, RET_HEAD_DIM, RET_HEAD_DIM)), f32)
    return dmask.astype(f32), wide(xi), wide(zeta), cdec


def _retention(qkvg, gn_g, to_cast=(), chunks_per_step=4):
    batch, seq, _ = qkvg.shape
    width = RET_WIDTH
    c = RET_CHUNK
    rows = c * chunks_per_step
    steps_per_batch = seq // rows
    n_steps = batch * steps_per_batch
    blk = lambda wd: pl.BlockSpec((1, rows, wd), lambda b, n: (b, n, 0))
    tab = _resident((RET_HEADS, c, RET_HEAD_DIM))
    cast_specs, cast_shapes = [], []
    for w in to_cast:
        nb = max(k for k in range(1, n_steps + 1)
                 if n_steps % k == 0 and w.shape[0] % k == 0 and (w.shape[0] // k) % 16 == 0)
        hold = n_steps // nb
        cast_specs.append(pl.BlockSpec((w.shape[0] // nb, w.shape[1]),
                                       lambda b, n, hold=hold: ((b * steps_per_batch + n) // hold, 0)))
        cast_shapes.append(jax.ShapeDtypeStruct(w.shape, BF16))
    res = pl.pallas_call(
        functools.partial(_retention_kernel, n_cast=len(to_cast)),
        out_shape=(jax.ShapeDtypeStruct((batch, seq, width), BF16), *cast_shapes),
        grid=(batch, seq // rows),
        in_specs=[blk(4 * width)] + [tab] * 4 + [_resident((1, width))] + cast_specs,
        out_specs=(blk(width), *cast_specs),
        scratch_shapes=[pltpu.VMEM((RET_HEADS, RET_HEAD_DIM, RET_HEAD_DIM), F32),
                        pltpu.VMEM((2, c, c), F32),
                        pltpu.VMEM((2, RET_HEAD_DIM, RET_HEAD_DIM), F32),
                        pltpu.VMEM((2, c, RET_HEAD_DIM), F32)],
        compiler_params=_params("arbitrary", "arbitrary"),
        name="retention",
    )(qkvg, *_retention_tables(), gn_g, *to_cast)
    return res[0], res[1:]


def _mem_kv_kernel(mem_ref, g_ref, w32_ref, k_ref, v_ref, w_ref):
    d = mem_ref.shape[-1]
    _cast_once(1, [(w32_ref, w_ref)])
    kv = _dot(_rms(mem_ref[0], g_ref[...]).astype(BF16), w_ref[...])
    k_ref[0] = kv[:, :d].astype(BF16)
    v_ref[0] = kv[:, d:].astype(BF16)


def _mem_kv(mem, g, w_kv):
    batch, n_mem, d = mem.shape
    blk = pl.BlockSpec((1, n_mem, d), lambda b: (b, 0, 0))
    out = jax.ShapeDtypeStruct((batch, n_mem, d), BF16)
    return pl.pallas_call(
        _mem_kv_kernel,
        out_shape=(out, out),
        grid=(batch,),
        in_specs=[blk, _resident((1, d)), _resident((d, 2 * d))],
        out_specs=(blk, blk),
        scratch_shapes=[pltpu.VMEM((d, 2 * d), BF16)],
        compiler_params=_params("arbitrary"),
        name="mem_kv",
    )(mem, g, w_kv)


def _mix_out_kernel(a_ref, r_ref, x_ref, wout_ref, gpost_ref, gpre_ref, wq_ref, mk_ref, mv_ref,
                    wo_ref, gpost2_ref, o_ref, x1_ref, h_ref, q_ref, att_ref, *, sub):
    half = a_ref.shape[-1]
    d = x_ref.shape[-1]
    dh = d // XATT_HEADS

    def out_proj(rows):
        y = _dot(a_ref[0, rows, :], wout_ref[:half, :]) + _dot(r_ref[0, rows, :], wout_ref[half:, :])
        x1 = x_ref[0, rows, :] + _rms(y, gpost_ref[...])
        x1_ref[rows, :] = x1
        h_ref[rows, :] = _rms(x1, gpre_ref[...]).astype(BF16)

    def q_proj(rows):
        q_ref[rows, :] = (_dot(h_ref[rows, :], wq_ref[...]) * (dh ** -0.5)).astype(BF16)

    def cross_attn(rows):
        for hd in range(XATT_HEADS):
            c = slice(hd * dh, (hd + 1) * dh)
            s = _dot_nt(q_ref[rows, c], mk_ref[0, :, c])
            p = jnp.exp(s - jnp.max(s, axis=-1, keepdims=True))
            l = jnp.sum(p, axis=-1, keepdims=True)
            att_ref[rows, c] = (_dot(p.astype(BF16), mv_ref[0, :, c]) / l).astype(BF16)

    def o_proj(rows):
        y2 = _dot(att_ref[rows, :], wo_ref[...])
        o_ref[0, rows, :] = x1_ref[rows, :] + _rms(y2, gpost2_ref[...])

    tiles = [slice(r0, r0 + sub) for r0 in range(0, x_ref.shape[1], sub)]
    for stage in (out_proj, q_proj, cross_attn, o_proj):
        for rows in tiles:
            stage(rows)


def _mix_out(a, r, x, w_out, g_post, g_pre, w_q, mem_k, mem_v, w_o, g_post2, tm=1024, sub=512):
    batch, seq, d = x.shape
    half = a.shape[-1]
    n_mem = mem_k.shape[1]
    tok = lambda w: pl.BlockSpec((1, tm, w), lambda b, t: (b, t, 0))
    memblk = pl.BlockSpec((1, n_mem, d), lambda b, t: (b, 0, 0))
    vec = _resident((1, d))
    mat = _resident((d, d))
    return pl.pallas_call(
        functools.partial(_mix_out_kernel, sub=sub),
        out_shape=jax.ShapeDtypeStruct((batch, seq, d), F32),
        grid=(batch, seq // tm),
        in_specs=[tok(half), tok(half), tok(d), mat, vec, vec, mat, memblk, memblk, mat, vec],
        out_specs=tok(d),
        scratch_shapes=[pltpu.VMEM((tm, d), F32),
                        pltpu.VMEM((tm, d), BF16),
                        pltpu.VMEM((tm, d), BF16),
                        pltpu.VMEM((tm, d), BF16)],
        compiler_params=_params("parallel", "parallel"),
        name="mix_out",
    )(a, r, x, w_out, g_post, g_pre, w_q, mem_k, mem_v, w_o, g_post2)


def _ffn_kernel(x_ref, gpre_ref, wgu_ref, wdown_ref, gpost_ref, o_ref, h_ref, y_ref, *, chunks, sub):
    hidden = wdown_ref.shape[0]

    def norm(rows):
        h_ref[rows, :] = _rms(x_ref[rows, :], gpre_ref[...]).astype(BF16)

    def hidden_chunk(ci, rows):
        lo, hi = chunks[ci]
        h = h_ref[rows, :]
        g = _dot(h, wgu_ref[:, lo:hi])
        u = _dot(h, wgu_ref[:, hidden + lo:hidden + hi])
        act = (g * (1.0 / (1.0 + jnp.exp(-g))) * u).astype(BF16)
        part = _dot(act, wdown_ref[lo:hi, :])
        if ci == 0:
            y_ref[rows, :] = part
        else:
            y_ref[rows, :] += part

    def final(rows):
        o_ref[rows, :] = x_ref[rows, :] + _rms(y_ref[rows, :], gpost_ref[...])

    tiles = [slice(r0, r0 + sub) for r0 in range(0, x_ref.shape[0], sub)]
    for rows in tiles:
        norm(rows)
    for ci in range(len(chunks)):
        for rows in tiles:
            hidden_chunk(ci, rows)
    for rows in tiles:
        final(rows)


def _ffn(x2d, g_pre, w_gu, w_down, g_post, tm=1024, sub=256):
    tokens, d = x2d.shape
    hidden = w_down.shape[0]
    mxu = 256
    cut = (hidden // 2) // mxu * mxu
    chunks = ((0, cut), (cut, hidden))
    row = pl.BlockSpec((tm, d), lambda t: (t, 0))
    vec = _resident((1, d))
    return pl.pallas_call(
        functools.partial(_ffn_kernel, chunks=chunks, sub=sub),
        out_shape=jax.ShapeDtypeStruct((tokens, d), F32),
        grid=(tokens // tm,),
        in_specs=[row, vec, _resident((d, 2 * hidden)), _resident((hidden, d)), vec],
        out_specs=row,
        scratch_shapes=[pltpu.VMEM((tm, d), BF16),
                        pltpu.VMEM((tm, d), F32)],
        compiler_params=_params("parallel"),
        name="ffn",
    )(x2d, g_pre, w_gu, w_down, g_post)


def kernel(x, mem, pre_mix_g, post_mix_g, w_in, attn_gn_g, ret_gn_g, w_out,
           pre_mem_g, post_mem_g, mem_norm_g, w_q_mem, w_kv_mem, w_o_mem,
           pre_ffn_g, post_ffn_g, w_gate_up, w_down):
    batch, seq, d = x.shape
    depth = w_in.shape[0]
    for l in range(depth):
        vec = lambda g: g[l][None, :]
        qkv, ret_in = _in_proj(x, vec(pre_mix_g), w_in[l])
        a = _attention(qkv, vec(attn_gn_g))
        r, (w_gu, w_dn, w_o1, w_q, w_o2) = _retention(
            ret_in, vec(ret_gn_g),
            to_cast=(w_gate_up[l], w_down[l], w_out[l], w_q_mem[l], w_o_mem[l]))
        mem_k, mem_v = _mem_kv(mem, vec(mem_norm_g), w_kv_mem[l])
        x = _mix_out(a, r, x, w_o1, vec(post_mix_g), vec(pre_mem_g), w_q,
                     mem_k, mem_v, w_o2, vec(post_mem_g))
        x = _ffn(x.reshape(batch * seq, d), vec(pre_ffn_g), w_gu, w_dn,
                 vec(post_ffn_g)).reshape(batch, seq, d)
    return x
```

```python
import functools

import numpy as np
import jax
import jax.numpy as jnp
from jax import lax
from jax.experimental import pallas as pl
from jax.experimental.pallas import tpu as pltpu

F32 = jnp.float32
BF16 = jnp.bfloat16

RMS_EPS = 1e-6
GN_EPS = 1e-5
ATT_HEADS = 8
ATT_HEAD_DIM = 64
ATT_WIDTH = ATT_HEADS * ATT_HEAD_DIM
ROT_DIM = ATT_HEAD_DIM // 4
ROPE_THETA = 500000.0
DILATED_PATTERNS = ((128, 1), (512, 4), (2048, 16))
RET_HEADS = 4
RET_HEAD_DIM = 128
RET_WIDTH = RET_HEADS * RET_HEAD_DIM
RET_CHUNK = 128
RET_ROPE_THETA = 10000.0
XATT_HEADS = 4

LANES = 128
BAND = 128
MASKED = -1e30
LOG2_E = 1.4426950408889634
VMEM_LIMIT = 56 * 1024 * 1024


def _resident(shape):
    zeros = (0,) * len(shape)
    return pl.BlockSpec(shape, lambda *_: zeros, pipeline_mode=pl.Buffered(1))


def _params(*sem):
    return pltpu.CompilerParams(dimension_semantics=sem, vmem_limit_bytes=VMEM_LIMIT)


def _cast_once(grid_rank, pairs, cols=512):
    first = pl.program_id(0) == 0
    for axis in range(1, grid_rank):
        first = jnp.logical_and(first, pl.program_id(axis) == 0)

    @pl.when(first)
    def _():
        for w_ref, wb_ref in pairs:
            for c in range(0, w_ref.shape[1], cols):
                wb_ref[:, c:c + cols] = w_ref[:, c:c + cols].astype(BF16)


def _rms(x, g):
    return x * lax.rsqrt(jnp.mean(x * x, axis=-1, keepdims=True) + RMS_EPS) * g


def _dot(a, b):
    return jnp.dot(a, b, preferred_element_type=F32)


def _dot_nt(a, b):
    return lax.dot_general(a, b, (((1,), (1,)), ((), ())), preferred_element_type=F32)


def _dot_tn(a, b):
    return lax.dot_general(a, b, (((0,), (0,)), ((), ())), preferred_element_type=F32)


def _in_proj_kernel(x_ref, g_ref, w_ref, ca_ref, sa1_ref, sa2_ref, cr_ref, sr_ref, *refs):
    att_refs = refs[:len(DILATED_PATTERNS)]
    ret_ref, h_ref, acc_ref, scr_ref, scr2_ref, wb_ref = refs[len(att_refs):]
    tm = x_ref.shape[1]
    width = ATT_WIDTH
    n_att = 3
    (_, r_one), (_, r_mid), (_, r_top) = DILATED_PATTERNS
    assert r_one == 1 and r_top % r_mid == 0
    rows_per_step = 64
    _cast_once(2, [(w_ref, wb_ref)])
    h_ref[...] = _rms(x_ref[0], g_ref[...]).astype(BF16)

    def project(i, slot):
        acc_ref[slot] = _dot(h_ref[...], wb_ref[:, i * width:(i + 1) * width])

    def rope_attn(t, rows):
        return (t * ca_ref[rows, :]
                + pltpu.roll(t, LANES - ROT_DIM // 2, 1) * sa1_ref[rows, :]
                + pltpu.roll(t, ROT_DIM // 2, 1) * sa2_ref[rows, :])

    def rope_ret(t, rows):
        return t * cr_ref[rows, :] + pltpu.roll(t, RET_HEAD_DIM // 2, 1) * sr_ref[rows, :]

    def post(i, slot):
        for gi, c0 in enumerate(range(0, width, LANES)):
            cols = slice(c0, c0 + LANES)
            for r0 in range(0, tm, rows_per_step):
                rows = slice(r0, r0 + rows_per_step)
                t = acc_ref[slot, rows, cols]
                if i == 0:
                    t = rope_attn(t, rows) * (ATT_HEAD_DIM ** -0.5 * LOG2_E)
                elif i == 1:
                    t = rope_attn(t, rows)
                elif i == n_att:
                    t = rope_ret(t, rows)
                elif i == n_att + 1:
                    t = rope_ret(t, rows) * (RET_HEAD_DIM ** -0.5)
                if i >= n_att:
                    ocols = slice((i - n_att) * width + c0, (i - n_att) * width + c0 + LANES)
                    ret_ref[0, rows, ocols] = t.astype(BF16)
                else:
                    ocols = slice(i * width + c0, i * width + c0 + LANES)
                    scr_ref[i, gi, rows, :] = t
                    att_refs[0][0, 0, rows, ocols] = t.astype(BF16)
            if i < n_att:
                for c4 in range(r_mid):
                    picked = scr_ref[i, gi, pl.ds(c4, tm // r_mid, stride=r_mid), :]
                    att_refs[1][0, c4, :, ocols] = picked.astype(BF16)
                    scr2_ref[i, gi, c4] = picked
                for c4 in range(r_mid):
                    for m in range(r_top // r_mid):
                        picked = scr2_ref[i, gi, c4, pl.ds(m, tm // r_top, stride=r_top // r_mid), :]
                        att_refs[2][0, m * r_mid + c4, :, ocols] = picked.astype(BF16)

    n_streams = n_att + 4
    project(0, 0)
    for i in range(n_streams):
        if i + 1 < n_streams:
            project(i + 1, (i + 1) % 2)
        post(i, i % 2)


def _rope_tables(seq):
    f32 = np.float32
    pos = np.arange(seq, dtype=f32)[:, None]
    half = ROT_DIM // 2
    inv = f32(ROPE_THETA) ** (-(np.arange(0, ROT_DIM, 2, dtype=f32) / f32(ROT_DIM)))
    ang = pos * inv[None, :]
    cos, sin = np.cos(ang), np.sin(ang)
    one = np.ones((seq, ATT_HEAD_DIM - ROT_DIM), f32)
    zero = np.zeros((seq, ATT_HEAD_DIM - ROT_DIM), f32)
    zh = np.zeros((seq, half), f32)
    ca = np.concatenate([cos, cos, one], axis=1)
    sa1 = np.concatenate([-sin, zh, zero], axis=1)
    sa2 = np.concatenate([zh, sin, zero], axis=1)
    reps = LANES // ATT_HEAD_DIM
    ca, sa1, sa2 = (np.tile(t, (1, reps)) for t in (ca, sa1, sa2))
    inv_r = f32(RET_ROPE_THETA) ** (-(np.arange(0, RET_HEAD_DIM, 2, dtype=f32) / f32(RET_HEAD_DIM)))
    ang_r = pos * inv_r[None, :]
    cr = np.concatenate([np.cos(ang_r), np.cos(ang_r)], axis=1)
    sr = np.concatenate([-np.sin(ang_r), np.sin(ang_r)], axis=1)
    return tuple(t.astype(f32) for t in (ca, sa1, sa2, cr, sr))


def _in_proj(x, g, w_in, tm=512):
    batch, seq, d = x.shape
    width = w_in.shape[1]
    tok = lambda w: pl.BlockSpec((1, tm, w), lambda b, t: (b, t, 0))
    tab = pl.BlockSpec((tm, LANES), lambda b, t: (t, 0))
    att_shapes, att_specs = [], []
    for _, r in DILATED_PATTERNS:
        assert tm % (r * 16) == 0
        att_shapes.append(jax.ShapeDtypeStruct((batch, r, seq // r, 3 * ATT_WIDTH), BF16))
        att_specs.append(pl.BlockSpec((1, r, tm // r, 3 * ATT_WIDTH), lambda b, t: (b, 0, t, 0)))
    ret_shape = jax.ShapeDtypeStruct((batch, seq, 4 * RET_WIDTH), BF16)
    outs = pl.pallas_call(
        _in_proj_kernel,
        out_shape=tuple(att_shapes) + (ret_shape,),
        grid=(batch, seq // tm),
        in_specs=[tok(d), _resident((1, d)), _resident((d, width))] + [tab] * 5,
        out_specs=tuple(att_specs) + (tok(4 * RET_WIDTH),),
        scratch_shapes=[pltpu.VMEM((tm, d), BF16),
                        pltpu.VMEM((2, tm, ATT_WIDTH), F32),
                        pltpu.VMEM((3, ATT_WIDTH // LANES, tm, LANES), F32),
                        pltpu.VMEM((3, ATT_WIDTH // LANES, DILATED_PATTERNS[1][1],
                                    tm // DILATED_PATTERNS[1][1], LANES), F32),
                        pltpu.VMEM((d, width), BF16)],
        compiler_params=_params("arbitrary", "arbitrary"),
        name="in_proj",
    )(x, g, w_in, *_rope_tables(seq))
    return outs[:len(att_shapes)], outs[len(att_shapes)]


def _attn_kernel(*refs, nc, nq, seq_tiles, has_prev, is_last):
    refs = list(refs)
    main_ref = refs.pop(0)
    if seq_tiles > 1:
        halo_ref = refs.pop(0)
    if has_prev:
        po_ref, ps_ref = refs[:2]
        del refs[:2]
    if is_last:
        g_ref = refs.pop(0)
    o_ref = refs.pop(0)
    if not is_last:
        s_ref = refs.pop(0)
    kbuf, vbuf, p_buf, bias, onehot = refs[:5]
    del refs[:5]
    if nc > 1:
        obuf, sbuf = refs

    half = ATT_HEAD_DIM
    groups = ATT_WIDTH // LANES
    kcols = slice(ATT_WIDTH, 2 * ATT_WIDTH)
    vcols = slice(2 * ATT_WIDTH, 3 * ATT_WIDTH)
    for cls in range(nc):
        if seq_tiles > 1:
            kbuf[cls, :BAND] = halo_ref[0, cls, :, kcols]
            vbuf[cls, :BAND] = halo_ref[0, cls, :, vcols]
        else:
            kbuf[cls, :BAND] = jnp.zeros((BAND, ATT_WIDTH), BF16)
            vbuf[cls, :BAND] = jnp.zeros((BAND, ATT_WIDTH), BF16)
        kbuf[cls, BAND:] = main_ref[0, cls, :, kcols]
        vbuf[cls, BAND:] = main_ref[0, cls, :, vcols]

    key = lax.broadcasted_iota(jnp.int32, (2 * BAND, BAND), 0)
    qry = lax.broadcasted_iota(jnp.int32, (2 * BAND, BAND), 1)
    if seq_tiles == 1:
        floor = BAND
    else:
        floor = jnp.where(pl.program_id(1) == 0, BAND, 0)
    upper = key <= qry + BAND
    bias[0] = jnp.where((key >= jnp.maximum(qry, floor)) & upper, 0.0, MASKED).astype(BF16)
    bias[1] = jnp.where((key >= qry) & upper, 0.0, MASKED).astype(BF16)
    eye = (lax.broadcasted_iota(jnp.int32, (BAND, 2 * BAND), 0)
           == (lax.broadcasted_iota(jnp.int32, (BAND, 2 * BAND), 1) & (BAND - 1)))
    onehot[...] = jnp.where(eye, 1.0, 0.0).astype(BF16)

    row8 = lax.broadcasted_iota(jnp.int32, (8, BAND), 0)

    def scores(cls, qstart, slot, which):
        tops = []
        for gi, c0 in enumerate(range(0, ATT_WIDTH, LANES)):
            cols = slice(c0, c0 + LANES)
            qp = main_ref[0, cls, pl.ds(qstart, BAND), cols]
            kp = kbuf[cls, pl.ds(qstart, 2 * BAND), cols]
            qt = qp.T
            zero = jnp.zeros((half, BAND), BF16)
            q_ext = jnp.concatenate([
                jnp.concatenate([qt[:half], zero], axis=0),
                jnp.concatenate([zero, qt[half:]], axis=0)], axis=1)
            q_ext = jnp.concatenate([q_ext, onehot[...]], axis=0)
            s = _dot(jnp.concatenate([kp, bias[which]], axis=1), q_ext)
            pair = []
            for rows in (slice(0, BAND), slice(BAND, 2 * BAND)):
                sh = s[rows, :]
                parts = [sh[r8:r8 + 8, :] for r8 in range(0, BAND, 8)]
                while len(parts) > 1:
                    parts = [jnp.maximum(x, y) for x, y in zip(parts[0::2], parts[1::2])]
                m = jnp.max(parts[0], axis=0, keepdims=True)
                p_buf[slot, gi, rows, :] = jnp.exp2(sh - m).astype(BF16)
                pair.append(m)
            tops.append(pair)
        return tops

    ones_rows = jnp.ones((16, BAND), BF16)

    def finish(cls, qstart, slot, tops):
        qs = pl.ds(qstart, BAND)
        ks = pl.ds(qstart, 2 * BAND)
        if has_prev:
            prev_lse = ps_ref[0, cls, qs, :].T
        stats = jnp.zeros((8, BAND), F32)
        for gi, c0 in enumerate(range(0, ATT_WIDTH, LANES)):
            cols = slice(c0, c0 + LANES)
            m_prev, m_own = tops[gi]
            m = jnp.maximum(m_prev, m_own)
            vt = vbuf[cls, ks, cols].T
            ot = (_dot(jnp.concatenate([vt[:, :BAND], ones_rows], axis=0), p_buf[slot, gi, :BAND, :])
                  * jnp.exp2(m_prev - m)
                  + _dot(jnp.concatenate([vt[:, BAND:], ones_rows], axis=0), p_buf[slot, gi, BAND:, :])
                  * jnp.exp2(m_own - m))
            l = ot[LANES:LANES + 1, :]
            inv = 1.0 / l
            lse = m + jnp.log2(l)
            if has_prev:
                prev_o = po_ref[0, cls, qs, cols].astype(F32).T
            parts = []
            for hh in range(LANES // half):
                head = 2 * gi + hh
                qcols = slice(hh * BAND, (hh + 1) * BAND)
                rows = slice(hh * half, (hh + 1) * half)
                w_new = inv[:, qcols]
                lse_h = lse[:, qcols]
                if has_prev:
                    lp = prev_lse[head:head + 1, :]
                    top = jnp.maximum(lp, lse_h)
                    wp = jnp.exp2(lp - top)
                    wn = jnp.exp2(lse_h - top)
                    den = wp + wn
                    o = (wp / den) * prev_o[rows, :] + (wn * w_new / den) * ot[rows, qcols]
                    lse_h = top + jnp.log2(den)
                else:
                    o = ot[rows, qcols] * w_new
                if is_last:
                    ms = jnp.mean(o * o, axis=0, keepdims=True)
                    o = o * lax.rsqrt(ms + RMS_EPS)
                parts.append(o)
                stats = jnp.where(row8 == head, lse_h, stats)
            o = jnp.concatenate(parts, axis=0).T
            if is_last:
                o = o * g_ref[:, cols]
            if nc > 1:
                obuf[gi, pl.ds(qstart * nc + cls, BAND, stride=nc), :] = o
            else:
                o_ref[0, 0, qs, cols] = o.astype(o_ref.dtype)
        if not is_last:
            st = jnp.concatenate([stats, jnp.zeros((BAND - 8, BAND), F32)], axis=0).T
            if nc > 1:
                sbuf[pl.ds(qstart * nc + cls, BAND, stride=nc), :] = st
            else:
                s_ref[0, 0, qs, :] = st

    blocks = [(cls, n * BAND) for cls in range(nc) for n in range(nq)]
    which = lambda blk: 0 if blk[1] == 0 else 1
    tops = scores(*blocks[0], 0, which(blocks[0]))
    for i, (cls, qstart) in enumerate(blocks):
        if i + 1 < len(blocks):
            tops_next = scores(*blocks[i + 1], (i + 1) % 2, which(blocks[i + 1]))
        finish(cls, qstart, i % 2, tops)
        tops = tops_next
    if nc > 1:
        for gi, c0 in enumerate(range(0, ATT_WIDTH, LANES)):
            o_ref[0, 0, :, c0:c0 + LANES] = obuf[gi].astype(o_ref.dtype)
        if not is_last:
            s_ref[0, 0] = sbuf[...]


def _attention(qkv, gn_g):
    w = ATT_WIDTH
    order = sorted(range(len(DILATED_PATTERNS)), key=lambda p: -DILATED_PATTERNS[p][1])
    prev = None
    for step, p in enumerate(order):
        window, r = DILATED_PATTERNS[p]
        batch, _, length, _ = qkv[p].shape
        is_last = step == len(order) - 1
        r_next = 1 if is_last else DILATED_PATTERNS[order[step + 1]][1]
        nc = r // r_next
        assert window // r == BAND and r == nc * r_next
        nq = 2 if nc > 1 else 8
        seq_tiles = length // (nq * BAND)
        assert length % (nq * BAND) == 0
        rows = nq * BAND

        if nc > 1:
            shape5 = lambda t: t.reshape(batch, nc, r_next, length, t.shape[-1])
            main = lambda wd: pl.BlockSpec((1, nc, None, rows, wd), lambda b, t: (b, 0, t // seq_tiles, t % seq_tiles, 0))
            halo = lambda wd: pl.BlockSpec(
                (1, nc, None, BAND, wd),
                lambda b, t: (b, 0, t // seq_tiles, jnp.maximum(t % seq_tiles * nq - 1, 0), 0))
            grid = (batch, r_next * seq_tiles)
            out_blk = lambda wd: pl.BlockSpec((1, 1, nc * rows, wd), lambda b, t: (b, t // seq_tiles, t % seq_tiles, 0))
        else:
            shape5 = lambda t: t
            main = lambda wd: pl.BlockSpec((1, 1, rows, wd), lambda b, t: (b, 0, t, 0))
            halo = lambda wd: pl.BlockSpec((1, 1, BAND, wd), lambda b, t: (b, 0, jnp.maximum(t * nq - 1, 0), 0))
            grid = (batch, seq_tiles)
            out_blk = lambda wd: pl.BlockSpec((1, 1, rows, wd), lambda b, t: (b, 0, t, 0))
        if nc > 1 and seq_tiles > 1:
            assert r_next == 1

        args = [shape5(qkv[p])]
        in_specs = [main(3 * w)]
        if seq_tiles > 1:
            args.append(shape5(qkv[p]))
            in_specs.append(halo(3 * w))
        if prev is not None:
            args += [shape5(prev[0]), shape5(prev[1])]
            in_specs += [main(w), main(LANES)]
        if is_last:
            args.append(gn_g)
            in_specs.append(_resident((1, w)))
        o_shape = jax.ShapeDtypeStruct((batch, r_next, length * nc, w), BF16)
        s_shape = jax.ShapeDtypeStruct((batch, r_next, length * nc, LANES), F32)
        groups = w // LANES
        scratch = [pltpu.VMEM((nc, rows + BAND, w), BF16)] * 2 + [
            pltpu.VMEM((2, groups, 2 * BAND, 2 * BAND), BF16),
            pltpu.VMEM((2, 2 * BAND, BAND), BF16),
            pltpu.VMEM((BAND, 2 * BAND), BF16)]
        if nc > 1:
            scratch += [pltpu.VMEM((w // LANES, nc * rows, LANES), F32), pltpu.VMEM((nc * rows, LANES), F32)]
        res = pl.pallas_call(
            functools.partial(_attn_kernel, nc=nc, nq=nq, seq_tiles=seq_tiles,
                              has_prev=prev is not None, is_last=is_last),
            out_shape=o_shape if is_last else (o_shape, s_shape),
            grid=grid,
            in_specs=in_specs,
            out_specs=out_blk(w) if is_last else (out_blk(w), out_blk(LANES)),
            scratch_shapes=scratch,
            compiler_params=_params("parallel", "arbitrary"),
            name=f"attn_r{r}",
        )(*args)
        if is_last:
            return res.reshape(batch, length * nc, w)
        prev = res


def _retention_kernel(in_ref, dmask_ref, xi_ref, zeta_ref, cdec_ref, g_ref, *refs, n_cast):
    cast_in = refs[:n_cast]
    o_ref = refs[n_cast]
    cast_out = refs[n_cast + 1:2 * n_cast + 1]
    state_ref, raw_ref, upd_ref, pre_ref = refs[2 * n_cast + 1:]

    @pl.when(pl.program_id(1) == 0)
    def _():
        state_ref[...] = jnp.zeros_like(state_ref)

    for src, dst in zip(cast_in, cast_out):
        dst[...] = src[...].astype(BF16)

    chunk = RET_CHUNK

    def where(t, h):
        return slice(t, t + chunk), slice(h * RET_HEAD_DIM, (h + 1) * RET_HEAD_DIM)

    def stream(j, t, h):
        c0 = j * RET_WIDTH + h * RET_HEAD_DIM
        return in_ref[0, t:t + chunk, c0:c0 + RET_HEAD_DIM]

    def scores(t, h, slot):
        k = stream(1, t, h)
        raw_ref[slot] = _dot_nt(stream(0, t, h), k)
        kz = (k.astype(F32) * zeta_ref[h]).astype(BF16)
        upd_ref[slot] = _dot_tn(kz, stream(2, t, h))

    def mix(t, h, slot):
        inner = (raw_ref[slot] * dmask_ref[h]).astype(BF16)
        state = state_ref[h]
        pre_ref[slot] = (_dot(inner, stream(2, t, h))
                         + _dot(stream(0, t, h), state.astype(BF16)) * xi_ref[h])
        state_ref[h] = state * cdec_ref[h] + upd_ref[slot]

    def finish(t, h, slot):
        rows, c = where(t, h)
        o = pre_ref[slot]
        mu = jnp.mean(o, axis=-1, keepdims=True)
        var = jnp.mean(jnp.square(o - mu), axis=-1, keepdims=True)
        y = (o - mu) * lax.rsqrt(var + GN_EPS) * g_ref[:, c]
        gate = stream(3, t, h).astype(F32)
        o_ref[0, rows, c] = (gate * (1.0 / (1.0 + jnp.exp(-gate))) * y).astype(o_ref.dtype)

    units = [(t, h) for t in range(0, in_ref.shape[1], chunk) for h in range(RET_HEADS)]
    scores(*units[0], 0)
    for i, unit in enumerate(units):
        if i + 1 < len(units):
            scores(*units[i + 1], (i + 1) % 2)
        mix(*unit, i % 2)
        if i > 0:
            finish(*units[i - 1], (i - 1) % 2)
    finish(*units[-1], (len(units) - 1) % 2)


def _retention_tables():
    f32 = np.float32
    c = RET_CHUNK
    log_g = np.log(f32(1.0) - f32(2.0) ** (f32(-5.0) - np.arange(RET_HEADS, dtype=f32))).astype(f32)
    n = np.arange(c, dtype=f32)
    rel = n[:, None] - n[None, :]
    dmask = np.where(rel >= 0, np.exp(log_g[:, None, None] * np.maximum(rel, f32(0.0))), f32(0.0))
    xi = np.exp(log_g[:, None] * (n + f32(1.0)))
    zeta = np.exp(log_g[:, None] * (f32(c - 1.0) - n))
    cdec = np.exp(log_g * f32(c))
    wide = lambda t: np.ascontiguousarray(np.broadcast_to(t[:, :, None], (RET_HEADS, c, RET_HEAD_DIM)), f32)
    cdec = np.ascontiguousarray(np.broadcast_to(cdec[:, None, None], (RET_HEADS, RET_HEAD_DIM, RET_HEAD_DIM)), f32)
    return dmask.astype(f32), wide(xi), wide(zeta), cdec


def _retention(qkvg, gn_g, to_cast=(), chunks_per_step=4):
    batch, seq, _ = qkvg.shape
    width = RET_WIDTH
    c = RET_CHUNK
    rows = c * chunks_per_step
    steps_per_batch = seq // rows
    n_steps = batch * steps_per_batch
    blk = lambda wd: pl.BlockSpec((1, rows, wd), lambda b, n: (b, n, 0))
    tab = _resident((RET_HEADS, c, RET_HEAD_DIM))
    cast_specs, cast_shapes = [], []
    for w in to_cast:
        nb = max(k for k in range(1, n_steps + 1)
                 if n_steps % k == 0 and w.shape[0] % k == 0 and (w.shape[0] // k) % 16 == 0)
        hold = n_steps // nb
        cast_specs.append(pl.BlockSpec((w.shape[0] // nb, w.shape[1]),
                                       lambda b, n, hold=hold: ((b * steps_per_batch + n) // hold, 0)))
        cast_shapes.append(jax.ShapeDtypeStruct(w.shape, BF16))
    res = pl.pallas_call(
        functools.partial(_retention_kernel, n_cast=len(to_cast)),
        out_shape=(jax.ShapeDtypeStruct((batch, seq, width), BF16), *cast_shapes),
        grid=(batch, seq // rows),
        in_specs=[blk(4 * width)] + [tab] * 4 + [_resident((1, width))] + cast_specs,
        out_specs=(blk(width), *cast_specs),
        scratch_shapes=[pltpu.VMEM((RET_HEADS, RET_HEAD_DIM, RET_HEAD_DIM), F32),
                        pltpu.VMEM((2, c, c), F32),
                        pltpu.VMEM((2, RET_HEAD_DIM, RET_HEAD_DIM), F32),
                        pltpu.VMEM((2, c, RET_HEAD_DIM), F32)],
        compiler_params=_params("arbitrary", "arbitrary"),
        name="retention",
    )(qkvg, *_retention_tables(), gn_g, *to_cast)
    return res[0], res[1:]


def _mem_kv_kernel(mem_ref, g_ref, w32_ref, k_ref, v_ref, w_ref):
    d = mem_ref.shape[-1]
    _cast_once(1, [(w32_ref, w_ref)])
    kv = _dot(_rms(mem_ref[0], g_ref[...]).astype(BF16), w_ref[...])
    k_ref[0] = kv[:, :d].astype(BF16)
    v_ref[0] = kv[:, d:].astype(BF16)


def _mem_kv(mem, g, w_kv):
    batch, n_mem, d = mem.shape
    blk = pl.BlockSpec((1, n_mem, d), lambda b: (b, 0, 0))
    out = jax.ShapeDtypeStruct((batch, n_mem, d), BF16)
    return pl.pallas_call(
        _mem_kv_kernel,
        out_shape=(out, out),
        grid=(batch,),
        in_specs=[blk, _resident((1, d)), _resident((d, 2 * d))],
        out_specs=(blk, blk),
        scratch_shapes=[pltpu.VMEM((d, 2 * d), BF16)],
        compiler_params=_params("arbitrary"),
        name="mem_kv",
    )(mem, g, w_kv)


def _mix_out_kernel(a_ref, r_ref, x_ref, wout_ref, gpost_ref, gpre_ref, wq_ref, mem_ref, gmem_ref,
                    wkv_ref, wo_ref, gpost2_ref, o_ref, x1_ref, h_ref, q_ref, att_ref, mk_ref, mv_ref,
                    *, sub):
    half = a_ref.shape[-1]
    d = x_ref.shape[-1]
    dh = d // XATT_HEADS

    @pl.when(pl.program_id(1) == 0)
    def _():
        kv = _dot(_rms(mem_ref[0], gmem_ref[...]).astype(BF16), wkv_ref[...])
        mk_ref[0] = kv[:, :d].astype(BF16)
        mv_ref[0] = kv[:, d:].astype(BF16)

    def out_proj(rows):
        y = _dot(a_ref[0, rows, :], wout_ref[:half, :]) + _dot(r_ref[0, rows, :], wout_ref[half:, :])
        x1 = x_ref[0, rows, :] + _rms(y, gpost_ref[...])
        x1_ref[rows, :] = x1
        h_ref[rows, :] = _rms(x1, gpre_ref[...]).astype(BF16)

    def q_proj(rows):
        q_ref[rows, :] = (_dot(h_ref[rows, :], wq_ref[...]) * (dh ** -0.5)).astype(BF16)

    def cross_attn(rows):
        for hd in range(XATT_HEADS):
            c = slice(hd * dh, (hd + 1) * dh)
            s = _dot_nt(q_ref[rows, c], mk_ref[0, :, c])
            p = jnp.exp(s - jnp.max(s, axis=-1, keepdims=True))
            l = jnp.sum(p, axis=-1, keepdims=True)
            att_ref[rows, c] = (_dot(p.astype(BF16), mv_ref[0, :, c]) / l).astype(BF16)

    def o_proj(rows):
        y2 = _dot(att_ref[rows, :], wo_ref[...])
        o_ref[0, rows, :] = x1_ref[rows, :] + _rms(y2, gpost2_ref[...])

    tiles = [slice(r0, r0 + sub) for r0 in range(0, x_ref.shape[1], sub)]
    for stage in (out_proj, q_proj, cross_attn, o_proj):
        for rows in tiles:
            stage(rows)


def _mix_out(a, r, x, w_out, g_post, g_pre, w_q, mem, g_mem, w_kv, w_o, g_post2, tm=1024, sub=512):
    batch, seq, d = x.shape
    half = a.shape[-1]
    n_mem = mem.shape[1]
    tok = lambda w: pl.BlockSpec((1, tm, w), lambda b, t: (b, t, 0))
    memblk = pl.BlockSpec((1, n_mem, d), lambda b, t: (b, 0, 0))
    vec = _resident((1, d))
    mat = _resident((d, d))
    return pl.pallas_call(
        functools.partial(_mix_out_kernel, sub=sub),
        out_shape=jax.ShapeDtypeStruct((batch, seq, d), F32),
        grid=(batch, seq // tm),
        in_specs=[tok(half), tok(half), tok(d), mat, vec, vec, mat, memblk, vec, _resident((d, 2 * d)),
                  mat, vec],
        out_specs=tok(d),
        scratch_shapes=[pltpu.VMEM((tm, d), F32),
                        pltpu.VMEM((tm, d), BF16),
                        pltpu.VMEM((tm, d), BF16),
                        pltpu.VMEM((tm, d), BF16),
                        pltpu.VMEM((1, n_mem, d), BF16),
                        pltpu.VMEM((1, n_mem, d), BF16)],
        compiler_params=_params("parallel", "arbitrary"),
        name="mix_out",
    )(a, r, x, w_out, g_post, g_pre, w_q, mem, g_mem, w_kv, w_o, g_post2)


def _ffn_kernel(x_ref, gpre_ref, wgu_ref, wdown_ref, gpost_ref, o_ref, h_ref, y_ref, *, chunks, sub):
    hidden = wdown_ref.shape[0]

    def norm(rows):
        h_ref[rows, :] = _rms(x_ref[rows, :], gpre_ref[...]).astype(BF16)

    def hidden_chunk(ci, rows):
        lo, hi = chunks[ci]
        h = h_ref[rows, :]
        g = _dot(h, wgu_ref[:, lo:hi])
        u = _dot(h, wgu_ref[:, hidden + lo:hidden + hi])
        act = (g * (1.0 / (1.0 + jnp.exp(-g))) * u).astype(BF16)
        part = _dot(act, wdown_ref[lo:hi, :])
        if ci == 0:
            y_ref[rows, :] = part
        else:
            y_ref[rows, :] += part

    def final(rows):
        o_ref[rows, :] = x_ref[rows, :] + _rms(y_ref[rows, :], gpost_ref[...])

    tiles = [slice(r0, r0 + sub) for r0 in range(0, x_ref.shape[0], sub)]
    for rows in tiles:
        norm(rows)
    for ci in range(len(chunks)):
        for rows in tiles:
            hidden_chunk(ci, rows)
    for rows in tiles:
        final(rows)


def _ffn(x2d, g_pre, w_gu, w_down, g_post, tm=1024, sub=256):
    tokens, d = x2d.shape
    hidden = w_down.shape[0]
    mxu = 256
    cut = (hidden // 2) // mxu * mxu
    chunks = ((0, cut), (cut, hidden))
    row = pl.BlockSpec((tm, d), lambda t: (t, 0))
    vec = _resident((1, d))
    return pl.pallas_call(
        functools.partial(_ffn_kernel, chunks=chunks, sub=sub),
        out_shape=jax.ShapeDtypeStruct((tokens, d), F32),
        grid=(tokens // tm,),
        in_specs=[row, vec, _resident((d, 2 * hidden)), _resident((hidden, d)), vec],
        out_specs=row,
        scratch_shapes=[pltpu.VMEM((tm, d), BF16),
                        pltpu.VMEM((tm, d), F32)],
        compiler_params=_params("parallel"),
        name="ffn",
    )(x2d, g_pre, w_gu, w_down, g_post)


def kernel(x, mem, pre_mix_g, post_mix_g, w_in, attn_gn_g, ret_gn_g, w_out,
           pre_mem_g, post_mem_g, mem_norm_g, w_q_mem, w_kv_mem, w_o_mem,
           pre_ffn_g, post_ffn_g, w_gate_up, w_down):
    batch, seq, d = x.shape
    depth = w_in.shape[0]
    for l in range(depth):
        vec = lambda g: g[l][None, :]
        qkv, ret_in = _in_proj(x, vec(pre_mix_g), w_in[l])
        a = _attention(qkv, vec(attn_gn_g))
        r, (w_gu, w_dn, w_o1, w_q, w_o2, w_kv) = _retention(
            ret_in, vec(ret_gn_g),
            to_cast=(w_gate_up[l], w_down[l], w_out[l], w_q_mem[l], w_o_mem[l], w_kv_mem[l]))
        x = _mix_out(a, r, x, w_o1, vec(post_mix_g), vec(pre_mem_g), w_q,
                     mem, vec(mem_norm_g), w_kv, w_o2, vec(post_mem_g))
        x = _ffn(x.reshape(batch * seq, d), vec(pre_ffn_g), w_gu, w_dn,
                 vec(post_ffn_g)).reshape(batch, seq, d)
    return x
```
